```python
import jax, jax.numpy as jnp
from jax import lax
import numpy as np

D_MODEL = 1024
BATCH = 32
SEQ = 256
DEPTH = 4
DEC_BATCH = 2
DEC_SEQ = 1024
PAST_LEN = 256

GRID_W = 64
N_MIXERS = 3
N_A_LAYERS = (DEPTH + 2) // 3
N_B_LAYERS = (DEPTH + 1) // 3
N_C_LAYERS = DEPTH // 3
CHUNK = 128
A_INNER = D_MODEL
A_GROUPS = A_INNER // 128
A_GROUP_DIM = A_INNER // A_GROUPS
HEAD_DIM = 128
N_HEADS = D_MODEL // HEAD_DIM
N_KV_HEADS = 2
KV_REP = N_HEADS // N_KV_HEADS
Q_BLOCK = 128
ROPE_THETA = 10000.0
AXIS_DIM = HEAD_DIM // 2
POOL_WINDOWS = (2, 4, 8, 16)
POOL_GROUP_DIM = D_MODEL // len(POOL_WINDOWS)
N_EXPERTS = 32
TOP_K = 4
D_EXPERT = D_MODEL
SWIGLU_ALPHA = 1.702
SWIGLU_LIMIT = 7.0
MOE_BLOCK = 128
EPS = 1e-6

kernel_name = 'hybrid_diffusion_prefix_trunk_step'


def rms_norm(x, gain):
    xf = x.astype(jnp.float32)
    y = xf * lax.rsqrt(jnp.mean(xf * xf, axis=-1, keepdims=True) + EPS)
    return (y * gain.astype(jnp.float32)).astype(x.dtype)


def ada_params(cvec, w_mod, b_mod):
    m = jax.nn.silu(cvec) @ w_mod + b_mod
    return jnp.split(m[:, None, :], 6, axis=-1)


def modulate(x, shift, scale):
    return x * (1 + scale) + shift


def axial_rope(x):
    n = x.shape[1]
    rows = n // GRID_W
    row = jnp.repeat(jnp.arange(rows), GRID_W)
    col = jnp.tile(jnp.arange(GRID_W), rows)
    half = AXIS_DIM // 2
    inv = ROPE_THETA ** (-jnp.arange(half, dtype=jnp.float32) / half)

    def rot(xa, pos):
        ang = pos.astype(jnp.float32)[:, None] * inv[None, :]
        cos = jnp.cos(ang)[None, :, None, :]
        sin = jnp.sin(ang)[None, :, None, :]
        x1, x2 = xa[..., :half], xa[..., half:]
        return jnp.concatenate([x1 * cos - x2 * sin, x2 * cos + x1 * sin], axis=-1)

    xf = x.astype(jnp.float32)
    out = jnp.concatenate([rot(xf[..., :AXIS_DIM], row), rot(xf[..., AXIS_DIM:], col)], axis=-1)
    return out.astype(x.dtype)


def block_attention(q, k, v):
    B, S, _, _ = q.shape
    n_blk = S // Q_BLOCK
    qb = jnp.moveaxis(q.reshape(B, n_blk, Q_BLOCK, N_KV_HEADS, KV_REP, HEAD_DIM), 1, 0)
    scale = HEAD_DIM ** -0.5

    def one_block(qblk):
        s = jnp.einsum('bqgrd,bkgd->bgrqk', qblk, k).astype(jnp.float32) * scale
        p = jax.nn.softmax(s, axis=-1).astype(v.dtype)
        return jnp.einsum('bgrqk,bkgd->bqgrd', p, v)

    o = lax.map(one_block, qb)
    return jnp.moveaxis(o, 0, 1).reshape(B, S, N_HEADS * HEAD_DIM)


def qkv_project(h, w_qkv, q_norm, k_norm):
    B, S, _ = h.shape
    qkv = h @ w_qkv
    nq = N_HEADS * HEAD_DIM
    nk = N_KV_HEADS * HEAD_DIM
    q = qkv[..., :nq].reshape(B, S, N_HEADS, HEAD_DIM)
    k = qkv[..., nq:nq + nk].reshape(B, S, N_KV_HEADS, HEAD_DIM)
    v = qkv[..., nq + nk:].reshape(B, S, N_KV_HEADS, HEAD_DIM)
    return rms_norm(q, q_norm), rms_norm(k, k_norm), v


def attention_context(h, w_qkv, q_norm, k_norm, w_o):
    q, k, v = qkv_project(h, w_qkv, q_norm, k_norm)
    return block_attention(q, k, v) @ w_o, k, v


def attention_latent(h, k_ctx, v_ctx, w_qkv, q_norm, k_norm, w_o):
    q, k, v = qkv_project(h, w_qkv, q_norm, k_norm)
    q = axial_rope(q)
    k = axial_rope(k)
    k_all = jnp.concatenate([k_ctx.astype(k.dtype), k], axis=1)
    v_all = jnp.concatenate([v_ctx.astype(v.dtype), v], axis=1)
    return block_attention(q, k_all, v_all) @ w_o


def chunk_gmlp(h, w_in, norm_v, w_s, b_s, w_out):
    B, S, _ = h.shape
    z = jax.nn.gelu(h @ w_in)
    u, v = jnp.split(z, 2, axis=-1)
    v = rms_norm(v, norm_v)
    vb = v.reshape(B, S // CHUNK, CHUNK, A_GROUPS, A_GROUP_DIM)
    mixed = jnp.einsum('gpq,bnqgc->bnpgc', w_s, vb) + b_s.T[None, None, :, :, None]
    return (u * mixed.reshape(B, S, A_INNER)) @ w_out


def multiscale_pool(h, w_grp, scale):
    B, S, D = h.shape
    hf = h.astype(jnp.float32)
    cs = jnp.concatenate([jnp.zeros((B, 1, D), jnp.float32), jnp.cumsum(hf, axis=1)], axis=1)
    t = jnp.arange(S)
    parts = []
    for g, w in enumerate(POOL_WINDOWS):
        lo = jnp.clip(t - w // 2, 0, S)
        hi = jnp.clip(t - w // 2 + w, 0, S)
        sl = slice(g * POOL_GROUP_DIM, (g + 1) * POOL_GROUP_DIM)
        mean = (cs[:, hi, sl] - cs[:, lo, sl]) / (hi - lo).astype(jnp.float32)[None, :, None]
        parts.append(mean - hf[:, :, sl])
    pooled = jnp.stack(parts, axis=2)
    mixed = jnp.einsum('bsgc,gcd->bsgd', pooled, w_grp.astype(jnp.float32)).reshape(B, S, D)
    return (mixed * scale.astype(jnp.float32)).astype(h.dtype)


def clamped_swiglu(gu):
    gate, up = jnp.split(gu, 2, axis=-1)
    gate = jnp.minimum(gate, SWIGLU_LIMIT)
    up = jnp.clip(up, -SWIGLU_LIMIT, SWIGLU_LIMIT)
    return (up + 1) * gate * jax.nn.sigmoid(SWIGLU_ALPHA * gate)


def moe_ffn(h, router_w, router_b, w_gu, b_gu, w_dn, b_dn):
    B, S, D = h.shape
    T = B * S
    xt = h.reshape(T, D)
    logits = (xt @ router_w + router_b).astype(jnp.float32)
    top_v, top_i = lax.top_k(logits, TOP_K)
    gates = jax.nn.softmax(top_v, axis=-1)
    A = T * TOP_K
    flat_e = top_i.reshape(A)
    flat_tok = jnp.repeat(jnp.arange(T, dtype=jnp.int32), TOP_K)
    flat_g = gates.reshape(A)
    order = jnp.argsort(flat_e)
    sorted_e = flat_e[order]
    counts = jnp.bincount(flat_e, length=N_EXPERTS)
    padded = (counts + MOE_BLOCK - 1) // MOE_BLOCK * MOE_BLOCK
    start_sorted = jnp.cumsum(counts) - counts
    end_padded = jnp.cumsum(padded)
    start_padded = end_padded - padded
    dest = start_padded[sorted_e] + jnp.arange(A) - start_sorted[sorted_e]
    n_blocks = -(-A // MOE_BLOCK) + N_EXPERTS
    n_slots = n_blocks * MOE_BLOCK
    slot_tok = jnp.full((n_slots,), T, jnp.int32).at[dest].set(flat_tok[order])
    slot_gate = jnp.zeros((n_slots,), jnp.float32).at[dest].set(flat_g[order])
    block_start = jnp.arange(n_blocks, dtype=end_padded.dtype) * MOE_BLOCK
    block_e = jnp.minimum(jnp.searchsorted(end_padded, block_start, side='right'), N_EXPERTS - 1)
    x_pad = jnp.concatenate([xt, jnp.zeros((1, D), xt.dtype)], axis=0)
    xb = x_pad[slot_tok].reshape(n_blocks, MOE_BLOCK, D)

    def expert_block(args):
        xblk, e = args
        hid = clamped_swiglu(xblk @ w_gu[e] + b_gu[e])
        return hid @ w_dn[e] + b_dn[e]

    yb = lax.map(expert_block, (xb, block_e)).reshape(n_slots, D)
    yw = yb * slot_gate[:, None].astype(yb.dtype)
    out = jax.ops.segment_sum(yw, slot_tok, num_segments=T + 1)[:T]
    return out.reshape(B, S, D)


def setup_inputs(seed: int = 0) -> dict:
    key = jax.random.key(seed)
    ks = iter(jax.random.split(key, 32))
    f32 = jnp.float32

    def nrm(shape, scale):
        return jax.random.normal(next(ks), shape, f32) * scale

    def gain(shape):
        return 1.0 + nrm(shape, 0.02)

    D = D_MODEL
    qkv_w = (N_HEADS + 2 * N_KV_HEADS) * HEAD_DIM
    return {
        'x_prompt': nrm((BATCH, SEQ, D), 1.0),
        'x_sample': nrm((DEC_BATCH, DEC_SEQ, D), 1.0),
        'cache_k': nrm((DEC_BATCH, N_B_LAYERS, PAST_LEN, N_KV_HEADS, HEAD_DIM), 1.0),
        'cache_v': nrm((DEC_BATCH, N_B_LAYERS, PAST_LEN, N_KV_HEADS, HEAD_DIM), 1.0),
        'c': nrm((DEC_BATCH, D), 1.0),
        'c_ctx': nrm((D,), 1.0),
        'w_mod': nrm((DEPTH, D, 6 * D), 0.5 * D ** -0.5),
        'b_mod': nrm((DEPTH, 6 * D), 0.02),
        'norm_mix': gain((DEPTH, D)),
        'norm_ffn': gain((DEPTH, D)),
        'gm_w_in': nrm((N_A_LAYERS, D, 2 * A_INNER), D ** -0.5),
        'gm_norm_v': gain((N_A_LAYERS, A_INNER)),
        'gm_w_s': nrm((N_A_LAYERS, A_GROUPS, CHUNK, CHUNK), CHUNK ** -0.5),
        'gm_b_s': 1.0 + nrm((N_A_LAYERS, A_GROUPS, CHUNK), 0.1),
        'gm_w_out': nrm((N_A_LAYERS, A_INNER, D), A_INNER ** -0.5),
        'at_w_qkv': nrm((N_B_LAYERS, D, qkv_w), D ** -0.5),
        'at_q_norm': gain((N_B_LAYERS, HEAD_DIM)),
        'at_k_norm': gain((N_B_LAYERS, HEAD_DIM)),
        'at_w_o': nrm((N_B_LAYERS, N_HEADS * HEAD_DIM, D), (N_HEADS * HEAD_DIM) ** -0.5),
        'pool_w_grp': nrm((N_C_LAYERS, len(POOL_WINDOWS), POOL_GROUP_DIM, POOL_GROUP_DIM), POOL_GROUP_DIM ** -0.5),
        'pool_scale': 1.0 + nrm((N_C_LAYERS, D), 0.1),
        'router_w': nrm((DEPTH, D, N_EXPERTS), D ** -0.5),
        'router_b': nrm((DEPTH, N_EXPERTS), 0.01),
        'w_gate_up': nrm((DEPTH, N_EXPERTS, D, 2 * D_EXPERT), D ** -0.5),
        'b_gate_up': nrm((DEPTH, N_EXPERTS, 2 * D_EXPERT), 0.02),
        'w_down': nrm((DEPTH, N_EXPERTS, D_EXPERT, D), D_EXPERT ** -0.5),
        'b_down': nrm((DEPTH, N_EXPERTS, D), 0.02),
    }


def reference(x_prompt, x_sample, cache_k, cache_v, c, c_ctx, w_mod, b_mod, norm_mix, norm_ffn,
              gm_w_in, gm_norm_v, gm_w_s, gm_b_s, gm_w_out,
              at_w_qkv, at_q_norm, at_k_norm, at_w_o,
              pool_w_grp, pool_scale,
              router_w, router_b, w_gate_up, b_gate_up, w_down, b_down):
    y_p = x_prompt
    y_s = x_sample
    ks_new, vs_new = [], []
    for l in range(DEPTH):
        kind, j = l % N_MIXERS, l // N_MIXERS
        sh1p, sc1p, g1p, sh2p, sc2p, g2p = ada_params(c_ctx[None, :], w_mod[l], b_mod[l])
        sh1s, sc1s, g1s, sh2s, sc2s, g2s = ada_params(c, w_mod[l], b_mod[l])
        hp = modulate(rms_norm(y_p, norm_mix[l]), sh1p, sc1p)
        hs = modulate(rms_norm(y_s, norm_mix[l]), sh1s, sc1s)
        if kind == 0:
            op = chunk_gmlp(hp, gm_w_in[j], gm_norm_v[j], gm_w_s[j], gm_b_s[j], gm_w_out[j])
            os_ = chunk_gmlp(hs, gm_w_in[j], gm_norm_v[j], gm_w_s[j], gm_b_s[j], gm_w_out[j])
        elif kind == 1:
            op, k_ctx, v_ctx = attention_context(hp, at_w_qkv[j], at_q_norm[j], at_k_norm[j], at_w_o[j])
            ks_new.append(k_ctx)
            vs_new.append(v_ctx)
            os_ = attention_latent(hs, cache_k[:, j], cache_v[:, j], at_w_qkv[j], at_q_norm[j],
                                   at_k_norm[j], at_w_o[j])
        else:
            op = multiscale_pool(hp, pool_w_grp[j], pool_scale[j])
            os_ = multiscale_pool(hs, pool_w_grp[j], pool_scale[j])
        y_p = y_p + g1p * op
        y_s = y_s + g1s * os_
        hp = modulate(rms_norm(y_p, norm_ffn[l]), sh2p, sc2p)
        hs = modulate(rms_norm(y_s, norm_ffn[l]), sh2s, sc2s)
        y_p = y_p + g2p * moe_ffn(hp, router_w[l], router_b[l], w_gate_up[l], b_gate_up[l], w_down[l], b_down[l])
        y_s = y_s + g2s * moe_ffn(hs, router_w[l], router_b[l], w_gate_up[l], b_gate_up[l], w_down[l], b_down[l])
    new_cache_k = jnp.stack(ks_new, axis=1)
    new_cache_v = jnp.stack(vs_new, axis=1)
    return (y_p, y_s, new_cache_k, new_cache_v)
```

```python
import functools

import jax
import jax.numpy as jnp
from jax import lax
from jax.experimental import pallas as pl
from jax.experimental.pallas import tpu as pltpu

F32 = jnp.float32
BF16 = jnp.bfloat16
HIGHEST = lax.Precision.HIGHEST

LANES = 128
SUBLANES = 8
VMEM_LIMIT_BYTES = 56 * 1024 * 1024

EPS = 1e-6
GRID_W = 64
CHUNK = 128
HEAD_DIM = 128
N_KV_HEADS = 2
ROPE_THETA = 10000.0
POOL_WINDOWS = (2, 4, 8, 16)
TOP_K = 4
SWIGLU_ALPHA = 1.702
SWIGLU_LIMIT = 7.0

TM = 256
MOE_ROWS = 256
META_STRIDE = 1024


def _params(n_axes=1):
    return pltpu.CompilerParams(dimension_semantics=("arbitrary",) * n_axes,
                                vmem_limit_bytes=VMEM_LIMIT_BYTES)


def _rms(x, gain):
    return x * lax.rsqrt(jnp.mean(x * x, axis=-1, keepdims=True) + EPS) * gain


def _norm_mod(y, gain, shift, scale):
    return _rms(y, gain) * (1.0 + scale) + shift


def _gelu_tanh(x):
    return 0.5 * x * (1.0 + jnp.tanh(0.7978845608028654 * (x + 0.044715 * (x * x * x))))


def _dot(a, b):
    return jnp.dot(a, b, preferred_element_type=F32)


def _dot_nt(a, b):
    return lax.dot_general(a, b, (((1,), (1,)), ((), ())), preferred_element_type=F32)


def _ada_kernel(cv_ref, w_ref, b_ref, o_ref):
    cv = cv_ref[...]
    s = cv * jax.nn.sigmoid(cv)
    o_ref[0, 0] = jnp.dot(s, w_ref[0], precision=HIGHEST, preferred_element_type=F32) + b_ref[0, 0]


def _ada_params(cvecs, w_mod, b_mod):
    depth, d, _ = w_mod.shape
    rows = cvecs.shape[0]
    return pl.pallas_call(
        _ada_kernel,
        grid=(depth, 6),
        in_specs=[
            pl.BlockSpec((rows, d), lambda l, j: (0, 0)),
            pl.BlockSpec((1, d, d), lambda l, j: (l, 0, j)),
            pl.BlockSpec((1, 1, 1, d), lambda l, j: (l, j, 0, 0)),
        ],
        out_specs=pl.BlockSpec((1, 1, rows, d), lambda l, j: (l, j, 0, 0)),
        out_shape=jax.ShapeDtypeStruct((depth, 6, rows, d), F32),
        compiler_params=_params(2),
        name="ada_params",
    )(cvecs, w_mod, b_mod.reshape(depth, 6, 1, d))


def _gmlp_kernel(y_ref, mod_ref, gain_ref, win_ref, nv_ref, ws_ref, bexp_ref, wout_ref, o_ref):
    y = y_ref[...]
    inner = nv_ref.shape[-1]
    groups = ws_ref.shape[0]
    h = _norm_mod(y, gain_ref[...], mod_ref[0, 0:1, :], mod_ref[0, 1:2, :])
    z = _gelu_tanh(_dot(h.astype(BF16), win_ref[...]))
    u = z[:, :inner]
    v = _rms(z[:, inner:], nv_ref[...]).astype(BF16)
    bexp = bexp_ref[...]
    chunks = []
    for c in range(y.shape[0] // CHUNK):
        cols = []
        for g in range(groups):
            vg = v[c * CHUNK:(c + 1) * CHUNK, g * LANES:(g + 1) * LANES]
            cols.append(_dot(ws_ref[g], vg))
        chunks.append(jnp.concatenate(cols, axis=1) + bexp)
    mixed = jnp.concatenate(chunks, axis=0)
    o = _dot((u * mixed).astype(BF16), wout_ref[...])
    o_ref[...] = y + mod_ref[0, 2:3, :] * o


def _gmlp(y, mod, gain, w_in, norm_v, w_s, b_s, w_out, mod_row):
    t_all, d = y.shape
    inner = norm_v.shape[-1]
    groups = w_s.shape[0]
    bexp = jnp.repeat(b_s.T, inner // groups, axis=1)
    return pl.pallas_call(
        _gmlp_kernel,
        grid=(t_all // TM,),
        in_specs=[
            pl.BlockSpec((TM, d), lambda i: (i, 0)),
            pl.BlockSpec((1, 6, d), lambda i: (mod_row(i), 0, 0)),
            pl.BlockSpec((1, d), lambda i: (0, 0)),
            pl.BlockSpec((d, 2 * inner), lambda i: (0, 0)),
            pl.BlockSpec((1, inner), lambda i: (0, 0)),
            pl.BlockSpec((groups, CHUNK, CHUNK), lambda i: (0, 0, 0)),
            pl.BlockSpec((CHUNK, inner), lambda i: (0, 0)),
            pl.BlockSpec((inner, d), lambda i: (0, 0)),
        ],
        out_specs=pl.BlockSpec((TM, d), lambda i: (i, 0)),
        out_shape=jax.ShapeDtypeStruct((t_all, d), F32),
        compiler_params=_params(),
        name="gmlp",
    )(y, mod, gain.reshape(1, d), w_in.astype(BF16), norm_v.reshape(1, inner),
      w_s.astype(BF16), bexp, w_out.astype(BF16))


def _qkv_kernel(y_ref, mod_ref, gain_ref, w_ref, qn_ref, kn_ref, cos_ref, sin_ref,
                q_ref, k_ref, v_ref):
    y = y_ref[...]
    n_q = q_ref.shape[-1] // HEAD_DIM
    n_kv = k_ref.shape[-1] // HEAD_DIM
    h = _norm_mod(y, gain_ref[...], mod_ref[0, 0:1, :], mod_ref[0, 1:2, :])
    qkv = _dot(h.astype(BF16), w_ref[...])
    cos = cos_ref[...]
    sin = sin_ref[...]
    lane = lax.broadcasted_iota(jnp.int32, cos.shape, 1)
    low_half = (lane % (HEAD_DIM // 2)) < (HEAD_DIM // 4)

    def head(idx, norm):
        x = _rms(qkv[:, idx * HEAD_DIM:(idx + 1) * HEAD_DIM], norm)
        partner = jnp.where(low_half,
                            pltpu.roll(x, HEAD_DIM - HEAD_DIM // 4, 1),
                            pltpu.roll(x, HEAD_DIM // 4, 1))
        return x * cos + partner * sin

    qn = qn_ref[...]
    kn = kn_ref[...]
    q_ref[...] = jnp.concatenate([head(i, qn) for i in range(n_q)], axis=1).astype(q_ref.dtype)
    k_ref[...] = jnp.concatenate([head(n_q + i, kn) for i in range(n_kv)], axis=1)
    v_ref[...] = qkv[:, (n_q + n_kv) * HEAD_DIM:]


def _qkv(y, mod, gain, w_qkv, q_norm, k_norm, cos_t, sin_t, mod_row, rope_row):
    t_all, d = y.shape
    n_kv_cols = N_KV_HEADS * HEAD_DIM
    return pl.pallas_call(
        _qkv_kernel,
        grid=(t_all // TM,),
        in_specs=[
            pl.BlockSpec((TM, d), lambda i: (i, 0)),
            pl.BlockSpec((1, 6, d), lambda i: (mod_row(i), 0, 0)),
            pl.BlockSpec((1, d), lambda i: (0, 0)),
            pl.BlockSpec(w_qkv.shape, lambda i: (0, 0)),
            pl.BlockSpec((1, HEAD_DIM), lambda i: (0, 0)),
            pl.BlockSpec((1, HEAD_DIM), lambda i: (0, 0)),
            pl.BlockSpec((TM, HEAD_DIM), lambda i: (rope_row(i), 0)),
            pl.BlockSpec((TM, HEAD_DIM), lambda i: (rope_row(i), 0)),
        ],
        out_specs=[
            pl.BlockSpec((TM, d), lambda i: (i, 0)),
            pl.BlockSpec((TM, n_kv_cols), lambda i: (i, 0)),
            pl.BlockSpec((TM, n_kv_cols), lambda i: (i, 0)),
        ],
        out_shape=[
            jax.ShapeDtypeStruct((t_all, d), BF16),
            jax.ShapeDtypeStruct((t_all, n_kv_cols), F32),
            jax.ShapeDtypeStruct((t_all, n_kv_cols), F32),
        ],
        compiler_params=_params(),
        name="qkv_project",
    )(y, mod, gain.reshape(1, d), w_qkv.astype(BF16), q_norm.reshape(1, HEAD_DIM),
      k_norm.reshape(1, HEAD_DIM), cos_t, sin_t)


def _attend(q, key_sets):
    n_heads = q.shape[1] // HEAD_DIM
    rep = n_heads // N_KV_HEADS
    scale = HEAD_DIM ** -0.5
    outs = []
    for hd in range(n_heads):
        g = hd // rep
        qh = q[:, hd * HEAD_DIM:(hd + 1) * HEAD_DIM]
        cols = slice(g * HEAD_DIM, (g + 1) * HEAD_DIM)
        scores = [_dot_nt(qh, k[:, cols]) * scale for k, _ in key_sets]
        m = functools.reduce(jnp.maximum, [jnp.max(s, axis=-1, keepdims=True) for s in scores])
        probs = [jnp.exp(s - m) for s in scores]
        denom = functools.reduce(jnp.add, [jnp.sum(p, axis=-1, keepdims=True) for p in probs])
        o = functools.reduce(jnp.add, [_dot(p.astype(BF16), v[:, cols])
                                       for p, (_, v) in zip(probs, key_sets)])
        outs.append(o / denom)
    return jnp.concatenate(outs, axis=1)


def _attn_ctx_kernel(y_ref, mod_ref, q_ref, k_ref, v_ref, wo_ref, o_ref):
    att = _attend(q_ref[...], [(k_ref[...].astype(BF16), v_ref[...].astype(BF16))])
    o_ref[...] = y_ref[...] + mod_ref[0, 2:3, :] * _dot(att.astype(BF16), wo_ref[...])


def _attn_lat_kernel(y_ref, mod_ref, q_ref, k_ref, v_ref, ck_ref, cv_ref, wo_ref, o_ref):
    sets = [(ck_ref[0].astype(BF16), cv_ref[0].astype(BF16)),
            (k_ref[...].astype(BF16), v_ref[...].astype(BF16))]
    att = _attend(q_ref[...], sets)
    o_ref[...] = y_ref[...] + mod_ref[0, 2:3, :] * _dot(att.astype(BF16), wo_ref[...])


def _attention(y, mod, q, k, v, cache_k, cache_v, w_o, n_ctx, ctx_len, n_lat, lat_len):
    t_all, d = y.shape
    kvc = k.shape[1]
    tp = n_ctx * ctx_len
    wo = w_o.astype(BF16)
    y = pl.pallas_call(
        _attn_ctx_kernel,
        grid=(n_ctx,),
        in_specs=[
            pl.BlockSpec((ctx_len, d), lambda b: (b, 0)),
            pl.BlockSpec((1, 6, d), lambda b: (0, 0, 0)),
            pl.BlockSpec((ctx_len, d), lambda b: (b, 0)),
            pl.BlockSpec((ctx_len, kvc), lambda b: (b, 0)),
            pl.BlockSpec((ctx_len, kvc), lambda b: (b, 0)),
            pl.BlockSpec((d, d), lambda b: (0, 0)),
        ],
        out_specs=pl.BlockSpec((ctx_len, d), lambda b: (b, 0)),
        out_shape=jax.ShapeDtypeStruct((t_all, d), F32),
        input_output_aliases={0: 0},
        compiler_params=_params(),
        name="attn_context",
    )(y, mod, q, k, v, wo)
    tq = TM
    qb = lat_len // tq
    past = cache_k.shape[1]
    y = pl.pallas_call(
        _attn_lat_kernel,
        grid=(n_lat, qb),
        in_specs=[
            pl.BlockSpec((tq, d), lambda b, i: (tp // tq + b * qb + i, 0)),
            pl.BlockSpec((1, 6, d), lambda b, i: (1 + b, 0, 0)),
            pl.BlockSpec((tq, d), lambda b, i: (tp // tq + b * qb + i, 0)),
            pl.BlockSpec((lat_len, kvc), lambda b, i: (tp // lat_len + b, 0)),
            pl.BlockSpec((lat_len, kvc), lambda b, i: (tp // lat_len + b, 0)),
            pl.BlockSpec((1, past, kvc), lambda b, i: (b, 0, 0)),
            pl.BlockSpec((1, past, kvc), lambda b, i: (b, 0, 0)),
            pl.BlockSpec((d, d), lambda b, i: (0, 0)),
        ],
        out_specs=pl.BlockSpec((tq, d), lambda b, i: (tp // tq + b * qb + i, 0)),
        out_shape=jax.ShapeDtypeStruct((t_all, d), F32),
        input_output_aliases={0: 0},
        compiler_params=_params(2),
        name="attn_latent",
    )(y, mod, q, k, v, cache_k, cache_v, wo)
    return y


def _pool_kernel(y_ref, mod_ref, gain_ref, wg_ref, scale_ref, o_ref):
    y = y_ref[...]
    s_len, d = y.shape
    gd = d // len(POOL_WINDOWS)
    h = _norm_mod(y, gain_ref[...], mod_ref[0, 0:1, :], mod_ref[0, 1:2, :])
    t_idx = lax.broadcasted_iota(jnp.int32, (s_len, s_len), 0)
    s_idx = lax.broadcasted_iota(jnp.int32, (s_len, s_len), 1)
    t_col = lax.broadcasted_iota(jnp.int32, (s_len, 1), 0)
    parts = []
    for g, w in enumerate(POOL_WINDOWS):
        lo = t_idx - w // 2
        window = ((s_idx >= lo) & (s_idx < lo + w)).astype(BF16)
        count = (jnp.minimum(t_col - w // 2 + w, s_len) - jnp.maximum(t_col - w // 2, 0)).astype(F32)
        hg = h[:, g * gd:(g + 1) * gd]
        h_hi = hg.astype(BF16)
        h_lo = (hg - h_hi.astype(F32)).astype(BF16)
        mean = (_dot(window, h_hi) + _dot(window, h_lo)) / count
        parts.append(_dot((mean - hg).astype(BF16), wg_ref[g]))
    mixed = jnp.concatenate(parts, axis=1) * scale_ref[...]
    o_ref[...] = y + mod_ref[0, 2:3, :] * mixed


def _pool(y, mod, gain, w_grp, scale, n_seq, s_len, row_off, mod_off):
    t_all, d = y.shape
    blk_off = row_off // s_len
    return pl.pallas_call(
        _pool_kernel,
        grid=(n_seq,),
        in_specs=[
            pl.BlockSpec((s_len, d), lambda b: (blk_off + b, 0)),
            pl.BlockSpec((1, 6, d), lambda b: (mod_off(b), 0, 0)),
            pl.BlockSpec((1, d), lambda b: (0, 0)),
            pl.BlockSpec(w_grp.shape, lambda b: (0, 0, 0)),
            pl.BlockSpec((1, d), lambda b: (0, 0)),
        ],
        out_specs=pl.BlockSpec((s_len, d), lambda b: (blk_off + b, 0)),
        out_shape=jax.ShapeDtypeStruct((t_all, d), F32),
        input_output_aliases={0: 0},
        compiler_params=_params(),
        name="pool_mixer",
    )(y, mod, gain.reshape(1, d), w_grp.astype(BF16), scale.reshape(1, d))


def _route_kernel(y_ref, mod_ref, gain_ref, rw_ref, rb_ref, h_ref, route_ref, cnt_ref, carry):
    i = pl.program_id(0)

    @pl.when(i == 0)
    def _():
        carry[...] = jnp.zeros_like(carry)

    y = y_ref[...]
    tm = y.shape[0]
    h = _norm_mod(y, gain_ref[...], mod_ref[0, 3:4, :], mod_ref[0, 4:5, :])
    h_ref[...] = h
    logits = jnp.dot(h, rw_ref[...], precision=HIGHEST, preferred_element_type=F32) + rb_ref[...]
    lane = lax.broadcasted_iota(jnp.int32, logits.shape, 1).astype(F32)
    work = logits
    vals, ids = [], []
    for _ in range(TOP_K):
        m = jnp.max(work, axis=-1, keepdims=True)
        idx = jnp.min(jnp.where(work == m, lane, float(LANES)), axis=-1, keepdims=True)
        vals.append(m)
        ids.append(idx)
        work = jnp.where(lane == idx, -jnp.inf, work)
    exps = [jnp.exp(v - vals[0]) for v in vals]
    denom = functools.reduce(jnp.add, exps)
    onehot = functools.reduce(jnp.add, [(lane == idx).astype(F32) for idx in ids])
    row = lax.broadcasted_iota(jnp.int32, (tm, tm), 0)
    col = lax.broadcasted_iota(jnp.int32, (tm, tm), 1)
    before = (col < row).astype(BF16)
    rank_mat = _dot(before, onehot.astype(BF16)) + carry[0:1, :]
    route = jnp.zeros(logits.shape, F32)
    for k in range(TOP_K):
        rank_k = jnp.sum(jnp.where(lane == ids[k], rank_mat, 0.0), axis=-1, keepdims=True)
        route = jnp.where(lane == k, exps[k] / denom, route)
        route = jnp.where(lane == TOP_K + k, ids[k], route)
        route = jnp.where(lane == 2 * TOP_K + k, rank_k, route)
    route_ref[...] = route
    carry[...] = carry[...] + jnp.sum(onehot, axis=0, keepdims=True)
    cnt_ref[...] = carry[...]


def _route(y, mod, gain, router_w, router_b, mod_row):
    t_all, d = y.shape
    n_exp = router_w.shape[1]
    rw = jnp.pad(router_w, ((0, 0), (0, LANES - n_exp)))
    rb = jnp.pad(router_b, (0, LANES - n_exp), constant_values=-1e30).reshape(1, LANES)
    return pl.pallas_call(
        _route_kernel,
        grid=(t_all // TM,),
        in_specs=[
            pl.BlockSpec((TM, d), lambda i: (i, 0)),
            pl.BlockSpec((1, 6, d), lambda i: (mod_row(i), 0, 0)),
            pl.BlockSpec((1, d), lambda i: (0, 0)),
            pl.BlockSpec((d, LANES), lambda i: (0, 0)),
            pl.BlockSpec((1, LANES), lambda i: (0, 0)),
        ],
        out_specs=[
            pl.BlockSpec((TM, d), lambda i: (i, 0)),
            pl.BlockSpec((TM, LANES), lambda i: (i, 0)),
            pl.BlockSpec((SUBLANES, LANES), lambda i: (0, 0)),
        ],
        out_shape=[
            jax.ShapeDtypeStruct((t_all, d), F32),
            jax.ShapeDtypeStruct((t_all, LANES), F32),
            jax.ShapeDtypeStruct((SUBLANES, LANES), F32),
        ],
        scratch_shapes=[pltpu.VMEM((SUBLANES, LANES), F32)],
        compiler_params=_params(),
        name="moe_route",
    )(y, mod, gain.reshape(1, d), rw, rb)


def _expert_kernel(be_ref, nu_ref, meta_ref, h_ref, wgu_ref, bgu_ref, wdn_ref, bdn_ref, yk_ref,
                   wgu_bf, wdn_bf, xbuf, ybuf, meta_s, gsem, ssem, msem):
    i = pl.program_id(0)
    n_used = nu_ref[0]
    slot = i % 2
    rows = xbuf.shape[1]
    d_exp = wdn_bf.shape[0]

    def meta_copy(blk, s):
        return pltpu.make_async_copy(meta_ref.at[pl.ds(blk * META_STRIDE, META_STRIDE)],
                                     meta_s.at[pl.ds(s * META_STRIDE, META_STRIDE)], msem.at[s])

    def issue_gather(s):
        def body(j, c):
            tok = meta_s[s * META_STRIDE + j]
            pltpu.make_async_copy(h_ref.at[pl.ds(tok, 1)], xbuf.at[s, pl.ds(j, 1)], gsem.at[s]).start()
            return c
        lax.fori_loop(0, rows, body, 0, unroll=8)

    def issue_scatter(s):
        def body(j, c):
            dst = meta_s[s * META_STRIDE + rows + j]
            pltpu.make_async_copy(ybuf.at[s, pl.ds(j, 1)], yk_ref.at[pl.ds(dst, 1)], ssem.at[s]).start()
            return c
        lax.fori_loop(0, rows, body, 0, unroll=8)

    def wait_rows(buf, sem, s):
        pltpu.make_async_copy(buf.at[s], buf.at[s], sem.at[s]).wait()

    @pl.when(i < n_used)
    def _():
        @pl.when(i == 0)
        def _():
            first = meta_copy(0, 0)
            first.start()
            first.wait()
            issue_gather(0)

            @pl.when(n_used > 1)
            def _():
                meta_copy(1, 1).start()

        @pl.when(i + 1 < n_used)
        def _():
            meta_copy(i + 1, 1 - slot).wait()
            issue_gather(1 - slot)

        new_expert = jnp.logical_or(i == 0, be_ref[i] != be_ref[jnp.maximum(i - 1, 0)])

        @pl.when(new_expert)
        def _():
            wgu_bf[...] = wgu_ref[0].astype(BF16)
            wdn_bf[...] = wdn_ref[0].astype(BF16)

        wait_rows(xbuf, gsem, slot)
        x = xbuf[slot].astype(BF16)
        gu = _dot(x, wgu_bf[...]) + bgu_ref[0]
        gate = jnp.minimum(gu[:, :d_exp], SWIGLU_LIMIT)
        up = jnp.clip(gu[:, d_exp:], -SWIGLU_LIMIT, SWIGLU_LIMIT)
        hid = (up + 1.0) * gate * jax.nn.sigmoid(SWIGLU_ALPHA * gate)
        y = _dot(hid.astype(BF16), wdn_bf[...]) + bdn_ref[0]

        @pl.when(i >= 2)
        def _():
            wait_rows(ybuf, ssem, slot)

        ybuf[slot] = y
        issue_scatter(slot)

        @pl.when(i + 2 < n_used)
        def _():
            meta_copy(i + 2, slot).start()

        @pl.when(i == n_used - 1)
        def _():
            wait_rows(ybuf, ssem, slot)

            @pl.when(i >= 1)
            def _():
                wait_rows(ybuf, ssem, 1 - slot)


def _experts(h, meta, block_e, n_used, w_gu, b_gu, w_dn, b_dn, n_blocks, out_rows):
    t_all, d = h.shape
    n_exp, _, two_f = w_gu.shape
    d_exp = two_f // 2
    grid_spec = pltpu.PrefetchScalarGridSpec(
        num_scalar_prefetch=2,
        grid=(n_blocks,),
        in_specs=[
            pl.BlockSpec(memory_space=pl.ANY),
            pl.BlockSpec(memory_space=pl.ANY),
            pl.BlockSpec((1, d, two_f), lambda i, be, nu: (be[i], 0, 0)),
            pl.BlockSpec((1, 1, two_f), lambda i, be, nu: (be[i], 0, 0)),
            pl.BlockSpec((1, d_exp, d), lambda i, be, nu: (be[i], 0, 0)),
            pl.BlockSpec((1, 1, d), lambda i, be, nu: (be[i], 0, 0)),
        ],
        out_specs=pl.BlockSpec(memory_space=pl.ANY),
        scratch_shapes=[
            pltpu.VMEM((d, two_f), BF16),
            pltpu.VMEM((d_exp, d), BF16),
            pltpu.VMEM((2, MOE_ROWS, d), F32),
            pltpu.VMEM((2, MOE_ROWS, d), F32),
            pltpu.SMEM((2 * META_STRIDE,), jnp.int32),
            pltpu.SemaphoreType.DMA((2,)),
            pltpu.SemaphoreType.DMA((2,)),
            pltpu.SemaphoreType.DMA((2,)),
        ],
    )
    return pl.pallas_call(
        _expert_kernel,
        grid_spec=grid_spec,
        out_shape=jax.ShapeDtypeStruct((out_rows, d), F32),
        compiler_params=_params(),
        name="moe_experts",
    )(block_e, n_used, meta, h, w_gu, b_gu.reshape(n_exp, 1, two_f), w_dn, b_dn.reshape(n_exp, 1, d))


def _combine_kernel(y_ref, mod_ref, route_ref, y0_ref, y1_ref, y2_ref, y3_ref, o_ref):
    route = route_ref[...]
    acc = route[:, 0:1] * y0_ref[...]
    for k, ref in ((1, y1_ref), (2, y2_ref), (3, y3_ref)):
        acc = acc + route[:, k:k + 1] * ref[...]
    o_ref[...] = y_ref[...] + mod_ref[0, 5:6, :] * acc


def _combine(y, mod, route, yk, mod_row):
    t_all, d = y.shape
    nt = t_all // TM
    yk_specs = [pl.BlockSpec((TM, d), functools.partial(lambda i, k: (k * nt + i, 0), k=k))
                for k in range(TOP_K)]
    return pl.pallas_call(
        _combine_kernel,
        grid=(nt,),
        in_specs=[
            pl.BlockSpec((TM, d), lambda i: (i, 0)),
            pl.BlockSpec((1, 6, d), lambda i: (mod_row(i), 0, 0)),
            pl.BlockSpec((TM, LANES), lambda i: (i, 0)),
        ] + yk_specs,
        out_specs=pl.BlockSpec((TM, d), lambda i: (i, 0)),
        out_shape=jax.ShapeDtypeStruct((t_all, d), F32),
        compiler_params=_params(),
        name="moe_combine",
    )(y, mod, route, yk, yk, yk, yk)


def _moe(y, mod, gain, router_w, router_b, w_gu, b_gu, w_dn, b_dn, mod_row):
    t_all, d = y.shape
    n_exp = router_w.shape[1]
    n_assign = t_all * TOP_K
    n_blocks = n_assign // MOE_ROWS + n_exp
    n_slots = n_blocks * MOE_ROWS

    h, route, cnt = _route(y, mod, gain, router_w, router_b, mod_row)

    counts = cnt[0, :n_exp].astype(jnp.int32)
    padded = (counts + MOE_ROWS - 1) // MOE_ROWS * MOE_ROWS
    end_padded = jnp.cumsum(padded)
    start_padded = end_padded - padded
    ids = route[:, TOP_K:2 * TOP_K].astype(jnp.int32)
    rank = route[:, 2 * TOP_K:3 * TOP_K].astype(jnp.int32)
    dest = (start_padded[ids] + rank).reshape(-1)
    tok = jnp.repeat(jnp.arange(t_all, dtype=jnp.int32), TOP_K)
    out_row = (jnp.tile(jnp.arange(TOP_K, dtype=jnp.int32), t_all) * t_all + tok)
    slot_ids = jnp.arange(n_slots, dtype=jnp.int32)
    dump = n_assign + (slot_ids // MOE_ROWS % 2) * MOE_ROWS + slot_ids % MOE_ROWS
    src = jnp.zeros((n_slots,), jnp.int32).at[dest].set(tok)
    dst = dump.at[dest].set(out_row)
    meta = jnp.concatenate([src.reshape(n_blocks, MOE_ROWS), dst.reshape(n_blocks, MOE_ROWS),
                            jnp.zeros((n_blocks, META_STRIDE - 2 * MOE_ROWS), jnp.int32)], axis=1).reshape(-1)
    n_used = (end_padded[-1] // MOE_ROWS).astype(jnp.int32)
    blk = jnp.arange(n_blocks, dtype=jnp.int32)
    block_e = jnp.minimum(jnp.searchsorted(end_padded, blk * MOE_ROWS, side="right"), n_exp - 1)
    block_e = jnp.where(blk < n_used, block_e, block_e[jnp.maximum(n_used - 1, 0)]).astype(jnp.int32)

    yk = _experts(h, meta, block_e, n_used.reshape(1), w_gu, b_gu, w_dn, b_dn,
                  n_blocks, n_assign + 2 * MOE_ROWS)
    return _combine(y, mod, route, yk, mod_row)


def kernel(x_prompt, x_sample, cache_k, cache_v, c, c_ctx, w_mod, b_mod, norm_mix, norm_ffn, gm_w_in, gm_norm_v, gm_w_s, gm_b_s, gm_w_out, at_w_qkv, at_q_norm, at_k_norm, at_w_o, pool_w_grp, pool_scale, router_w, router_b, w_gate_up, b_gate_up, w_down, b_down):
    n_ctx, ctx_len, d = x_prompt.shape
    n_lat, lat_len, _ = x_sample.shape
    depth = w_mod.shape[0]
    tp = n_ctx * ctx_len
    assert tp % lat_len == 0 and ctx_len % TM == 0 and lat_len % TM == 0 and ctx_len == TM
    assert 1 + n_lat <= SUBLANES

    y = jnp.concatenate([x_prompt.reshape(tp, d), x_sample.reshape(n_lat * lat_len, d)], axis=0)

    cvecs = jnp.zeros((SUBLANES, d), F32).at[0].set(c_ctx).at[1:1 + n_lat].set(c)
    mod_all = jnp.transpose(_ada_params(cvecs, w_mod, b_mod), (0, 2, 1, 3))

    def mod_row(i):
        return jnp.where(i * TM < tp, 0, 1 + (i * TM - tp) // lat_len)

    half = HEAD_DIM // 4
    inv = ROPE_THETA ** (-jnp.arange(half, dtype=F32) / half)
    pos = jnp.arange(lat_len)
    ang_r = (pos // GRID_W).astype(F32)[:, None] * inv[None, :]
    ang_c = (pos % GRID_W).astype(F32)[:, None] * inv[None, :]
    cos_t = jnp.concatenate([jnp.ones((TM, HEAD_DIM), F32),
                             jnp.concatenate([jnp.cos(ang_r)] * 2 + [jnp.cos(ang_c)] * 2, axis=1)], axis=0)
    sin_t = jnp.concatenate([jnp.zeros((TM, HEAD_DIM), F32),
                             jnp.concatenate([-jnp.sin(ang_r), jnp.sin(ang_r),
                                              -jnp.sin(ang_c), jnp.sin(ang_c)], axis=1)], axis=0)

    def rope_row(i):
        return jnp.where(i * TM < tp, 0, 1 + ((i * TM - tp) % lat_len) // TM)

    new_k, new_v = [], []
    for l in range(depth):
        kind, j = l % 3, l // 3
        mod = mod_all[l]
        if kind == 0:
            y = _gmlp(y, mod, norm_mix[l], gm_w_in[j], gm_norm_v[j], gm_w_s[j], gm_b_s[j], gm_w_out[j], mod_row)
        elif kind == 1:
            q, k, v = _qkv(y, mod, norm_mix[l], at_w_qkv[j], at_q_norm[j], at_k_norm[j],
                           cos_t, sin_t, mod_row, rope_row)
            new_k.append(k[:tp].reshape(n_ctx, ctx_len, N_KV_HEADS, HEAD_DIM))
            new_v.append(v[:tp].reshape(n_ctx, ctx_len, N_KV_HEADS, HEAD_DIM))
            past = cache_k.shape[2]
            ck = cache_k[:, j].reshape(n_lat, past, N_KV_HEADS * HEAD_DIM)
            cv = cache_v[:, j].reshape(n_lat, past, N_KV_HEADS * HEAD_DIM)
            y = _attention(y, mod, q, k, v, ck, cv, at_w_o[j], n_ctx, ctx_len, n_lat, lat_len)
        else:
            y = _pool(y, mod, norm_mix[l], pool_w_grp[j], pool_scale[j], n_ctx, ctx_len, 0, lambda b: 0)
            y = _pool(y, mod, norm_mix[l], pool_w_grp[j], pool_scale[j], n_lat, lat_len, tp, lambda b: 1 + b)
        y = _moe(y, mod, norm_ffn[l], router_w[l], router_b[l], w_gate_up[l], b_gate_up[l],
                 w_down[l], b_down[l], mod_row)

    return (y[:tp].reshape(n_ctx, ctx_len, d), y[tp:].reshape(n_lat, lat_len, d),
            jnp.stack(new_k, axis=1), jnp.stack(new_v, axis=1))
```

```python
import functools

import jax
import jax.numpy as jnp
from jax import lax
from jax.experimental import pallas as pl
from jax.experimental.pallas import tpu as pltpu

F32 = jnp.float32
BF16 = jnp.bfloat16
HIGHEST = lax.Precision.HIGHEST

LANES = 128
SUBLANES = 8
VMEM_LIMIT_BYTES = 56 * 1024 * 1024

EPS = 1e-6
GRID_W = 64
CHUNK = 128
HEAD_DIM = 128
N_KV_HEADS = 2
ROPE_THETA = 10000.0
POOL_WINDOWS = (2, 4, 8, 16)
TOP_K = 4
SWIGLU_ALPHA = 1.702
SWIGLU_LIMIT = 7.0

TM = 256
MOE_ROWS = 256
META_STRIDE = 1024


def _params(n_axes=1):
    return pltpu.CompilerParams(dimension_semantics=("arbitrary",) * n_axes,
                                vmem_limit_bytes=VMEM_LIMIT_BYTES)


def _rms(x, gain):
    return x * lax.rsqrt(jnp.mean(x * x, axis=-1, keepdims=True) + EPS) * gain


def _norm_mod(y, gain, shift, scale):
    return _rms(y, gain) * (1.0 + scale) + shift


def _gelu_tanh(x):
    return 0.5 * x * (1.0 + jnp.tanh(0.7978845608028654 * (x + 0.044715 * (x * x * x))))


def _dot(a, b):
    return jnp.dot(a, b, preferred_element_type=F32)


def _dot_nt(a, b):
    return lax.dot_general(a, b, (((1,), (1,)), ((), ())), preferred_element_type=F32)


def _ada_kernel(cv_ref, w_ref, b_ref, o_ref):
    cv = cv_ref[...]
    s = cv * jax.nn.sigmoid(cv)
    o_ref[0, 0] = jnp.dot(s, w_ref[0], precision=HIGHEST, preferred_element_type=F32) + b_ref[0, 0]


def _ada_params(cvecs, w_mod, b_mod):
    depth, d, _ = w_mod.shape
    rows = cvecs.shape[0]
    return pl.pallas_call(
        _ada_kernel,
        grid=(depth, 6),
        in_specs=[
            pl.BlockSpec((rows, d), lambda l, j: (0, 0)),
            pl.BlockSpec((1, d, d), lambda l, j: (l, 0, j)),
            pl.BlockSpec((1, 1, 1, d), lambda l, j: (l, j, 0, 0)),
        ],
        out_specs=pl.BlockSpec((1, 1, rows, d), lambda l, j: (l, j, 0, 0)),
        out_shape=jax.ShapeDtypeStruct((depth, 6, rows, d), F32),
        compiler_params=_params(2),
        name="ada_params",
    )(cvecs, w_mod, b_mod.reshape(depth, 6, 1, d))


def _gmlp_kernel(y_ref, mod_ref, gain_ref, win_ref, nv_ref, ws_ref, bexp_ref, wout_ref, o_ref):
    y = y_ref[...]
    inner = nv_ref.shape[-1]
    groups = ws_ref.shape[0]
    h = _norm_mod(y, gain_ref[...], mod_ref[0, 0:1, :], mod_ref[0, 1:2, :])
    z = _gelu_tanh(_dot(h.astype(BF16), win_ref[...]))
    u = z[:, :inner]
    v = _rms(z[:, inner:], nv_ref[...]).astype(BF16)
    bexp = bexp_ref[...]
    chunks = []
    for c in range(y.shape[0] // CHUNK):
        cols = []
        for g in range(groups):
            vg = v[c * CHUNK:(c + 1) * CHUNK, g * LANES:(g + 1) * LANES]
            cols.append(_dot(ws_ref[g], vg))
        chunks.append(jnp.concatenate(cols, axis=1) + bexp)
    mixed = jnp.concatenate(chunks, axis=0)
    o = _dot((u * mixed).astype(BF16), wout_ref[...])
    o_ref[...] = y + mod_ref[0, 2:3, :] * o


def _gmlp(y, mod, gain, w_in, norm_v, w_s, b_s, w_out, mod_row):
    t_all, d = y.shape
    inner = norm_v.shape[-1]
    groups = w_s.shape[0]
    bexp = jnp.repeat(b_s.T, inner // groups, axis=1)
    return pl.pallas_call(
        _gmlp_kernel,
        grid=(t_all // TM,),
        in_specs=[
            pl.BlockSpec((TM, d), lambda i: (i, 0)),
            pl.BlockSpec((1, 6, d), lambda i: (mod_row(i), 0, 0)),
            pl.BlockSpec((1, d), lambda i: (0, 0)),
            pl.BlockSpec((d, 2 * inner), lambda i: (0, 0)),
            pl.BlockSpec((1, inner), lambda i: (0, 0)),
            pl.BlockSpec((groups, CHUNK, CHUNK), lambda i: (0, 0, 0)),
            pl.BlockSpec((CHUNK, inner), lambda i: (0, 0)),
            pl.BlockSpec((inner, d), lambda i: (0, 0)),
        ],
        out_specs=pl.BlockSpec((TM, d), lambda i: (i, 0)),
        out_shape=jax.ShapeDtypeStruct((t_all, d), F32),
        compiler_params=_params(),
        name="gmlp",
    )(y, mod, gain.reshape(1, d), w_in.astype(BF16), norm_v.reshape(1, inner),
      w_s.astype(BF16), bexp, w_out.astype(BF16))


def _qkv_kernel(y_ref, mod_ref, gain_ref, w_ref, qn_ref, kn_ref, cos_ref, sin_ref,
                q_ref, k_ref, v_ref):
    y = y_ref[...]
    n_q = q_ref.shape[-1] // HEAD_DIM
    n_kv = k_ref.shape[-1] // HEAD_DIM
    h = _norm_mod(y, gain_ref[...], mod_ref[0, 0:1, :], mod_ref[0, 1:2, :])
    qkv = _dot(h.astype(BF16), w_ref[...])
    cos = cos_ref[...]
    sin = sin_ref[...]
    lane = lax.broadcasted_iota(jnp.int32, cos.shape, 1)
    low_half = (lane % (HEAD_DIM // 2)) < (HEAD_DIM // 4)

    def head(idx, norm):
        x = _rms(qkv[:, idx * HEAD_DIM:(idx + 1) * HEAD_DIM], norm)
        partner = jnp.where(low_half,
                            pltpu.roll(x, HEAD_DIM - HEAD_DIM // 4, 1),
                            pltpu.roll(x, HEAD_DIM // 4, 1))
        return x * cos + partner * sin

    qn = qn_ref[...]
    kn = kn_ref[...]
    q_ref[...] = jnp.concatenate([head(i, qn) for i in range(n_q)], axis=1).astype(q_ref.dtype)
    k_ref[...] = jnp.concatenate([head(n_q + i, kn) for i in range(n_kv)], axis=1)
    v_ref[...] = qkv[:, (n_q + n_kv) * HEAD_DIM:]


def _qkv(y, mod, gain, w_qkv, q_norm, k_norm, cos_t, sin_t, mod_row, rope_row):
    t_all, d = y.shape
    n_kv_cols = N_KV_HEADS * HEAD_DIM
    return pl.pallas_call(
        _qkv_kernel,
        grid=(t_all // TM,),
        in_specs=[
            pl.BlockSpec((TM, d), lambda i: (i, 0)),
            pl.BlockSpec((1, 6, d), lambda i: (mod_row(i), 0, 0)),
            pl.BlockSpec((1, d), lambda i: (0, 0)),
            pl.BlockSpec(w_qkv.shape, lambda i: (0, 0)),
            pl.BlockSpec((1, HEAD_DIM), lambda i: (0, 0)),
            pl.BlockSpec((1, HEAD_DIM), lambda i: (0, 0)),
            pl.BlockSpec((TM, HEAD_DIM), lambda i: (rope_row(i), 0)),
            pl.BlockSpec((TM, HEAD_DIM), lambda i: (rope_row(i), 0)),
        ],
        out_specs=[
            pl.BlockSpec((TM, d), lambda i: (i, 0)),
            pl.BlockSpec((TM, n_kv_cols), lambda i: (i, 0)),
            pl.BlockSpec((TM, n_kv_cols), lambda i: (i, 0)),
        ],
        out_shape=[
            jax.ShapeDtypeStruct((t_all, d), BF16),
            jax.ShapeDtypeStruct((t_all, n_kv_cols), F32),
            jax.ShapeDtypeStruct((t_all, n_kv_cols), F32),
        ],
        compiler_params=_params(),
        name="qkv_project",
    )(y, mod, gain.reshape(1, d), w_qkv.astype(BF16), q_norm.reshape(1, HEAD_DIM),
      k_norm.reshape(1, HEAD_DIM), cos_t, sin_t)


def _attend(q, key_sets):
    n_heads = q.shape[1] // HEAD_DIM
    rep = n_heads // N_KV_HEADS
    scale = HEAD_DIM ** -0.5
    outs = []
    for hd in range(n_heads):
        g = hd // rep
        qh = q[:, hd * HEAD_DIM:(hd + 1) * HEAD_DIM]
        cols = slice(g * HEAD_DIM, (g + 1) * HEAD_DIM)
        scores = [_dot_nt(qh, k[:, cols]) * scale for k, _ in key_sets]
        m = functools.reduce(jnp.maximum, [jnp.max(s, axis=-1, keepdims=True) for s in scores])
        probs = [jnp.exp(s - m) for s in scores]
        denom = functools.reduce(jnp.add, [jnp.sum(p, axis=-1, keepdims=True) for p in probs])
        o = functools.reduce(jnp.add, [_dot(p.astype(BF16), v[:, cols])
                                       for p, (_, v) in zip(probs, key_sets)])
        outs.append(o / denom)
    return jnp.concatenate(outs, axis=1)


def _attn_ctx_kernel(y_ref, mod_ref, q_ref, k_ref, v_ref, wo_ref, o_ref):
    att = _attend(q_ref[...], [(k_ref[...].astype(BF16), v_ref[...].astype(BF16))])
    o_ref[...] = y_ref[...] + mod_ref[0, 2:3, :] * _dot(att.astype(BF16), wo_ref[...])


def _attn_lat_kernel(y_ref, mod_ref, q_ref, k_ref, v_ref, ck_ref, cv_ref, wo_ref, o_ref):
    sets = [(ck_ref[0].astype(BF16), cv_ref[0].astype(BF16)),
            (k_ref[...].astype(BF16), v_ref[...].astype(BF16))]
    att = _attend(q_ref[...], sets)
    o_ref[...] = y_ref[...] + mod_ref[0, 2:3, :] * _dot(att.astype(BF16), wo_ref[...])


def _attention(y, mod, q, k, v, cache_k, cache_v, w_o, n_ctx, ctx_len, n_lat, lat_len):
    t_all, d = y.shape
    kvc = k.shape[1]
    tp = n_ctx * ctx_len
    wo = w_o.astype(BF16)
    y = pl.pallas_call(
        _attn_ctx_kernel,
        grid=(n_ctx,),
        in_specs=[
            pl.BlockSpec((ctx_len, d), lambda b: (b, 0)),
            pl.BlockSpec((1, 6, d), lambda b: (0, 0, 0)),
            pl.BlockSpec((ctx_len, d), lambda b: (b, 0)),
            pl.BlockSpec((ctx_len, kvc), lambda b: (b, 0)),
            pl.BlockSpec((ctx_len, kvc), lambda b: (b, 0)),
            pl.BlockSpec((d, d), lambda b: (0, 0)),
        ],
        out_specs=pl.BlockSpec((ctx_len, d), lambda b: (b, 0)),
        out_shape=jax.ShapeDtypeStruct((t_all, d), F32),
        input_output_aliases={0: 0},
        compiler_params=_params(),
        name="attn_context",
    )(y, mod, q, k, v, wo)
    tq = TM
    qb = lat_len // tq
    past = cache_k.shape[1]
    y = pl.pallas_call(
        _attn_lat_kernel,
        grid=(n_lat, qb),
        in_specs=[
            pl.BlockSpec((tq, d), lambda b, i: (tp // tq + b * qb + i, 0)),
            pl.BlockSpec((1, 6, d), lambda b, i: (1 + b, 0, 0)),
            pl.BlockSpec((tq, d), lambda b, i: (tp // tq + b * qb + i, 0)),
            pl.BlockSpec((lat_len, kvc), lambda b, i: (tp // lat_len + b, 0)),
            pl.BlockSpec((lat_len, kvc), lambda b, i: (tp // lat_len + b, 0)),
            pl.BlockSpec((1, past, kvc), lambda b, i: (b, 0, 0)),
            pl.BlockSpec((1, past, kvc), lambda b, i: (b, 0, 0)),
            pl.BlockSpec((d, d), lambda b, i: (0, 0)),
        ],
        out_specs=pl.BlockSpec((tq, d), lambda b, i: (tp // tq + b * qb + i, 0)),
        out_shape=jax.ShapeDtypeStruct((t_all, d), F32),
        input_output_aliases={0: 0},
        compiler_params=_params(2),
        name="attn_latent",
    )(y, mod, q, k, v, cache_k, cache_v, wo)
    return y


def _pool_kernel(y_ref, mod_ref, gain_ref, wg_ref, scale_ref, o_ref):
    y = y_ref[...]
    s_len, d = y.shape
    gd = d // len(POOL_WINDOWS)
    h = _norm_mod(y, gain_ref[...], mod_ref[0, 0:1, :], mod_ref[0, 1:2, :])
    t_idx = lax.broadcasted_iota(jnp.int32, (s_len, s_len), 0)
    s_idx = lax.broadcasted_iota(jnp.int32, (s_len, s_len), 1)
    t_col = lax.broadcasted_iota(jnp.int32, (s_len, 1), 0)
    parts = []
    for g, w in enumerate(POOL_WINDOWS):
        lo = t_idx - w // 2
        window = ((s_idx >= lo) & (s_idx < lo + w)).astype(BF16)
        count = (jnp.minimum(t_col - w // 2 + w, s_len) - jnp.maximum(t_col - w // 2, 0)).astype(F32)
        hg = h[:, g * gd:(g + 1) * gd]
        h_hi = hg.astype(BF16)
        h_lo = (hg - h_hi.astype(F32)).astype(BF16)
        mean = (_dot(window, h_hi) + _dot(window, h_lo)) / count
        parts.append(_dot((mean - hg).astype(BF16), wg_ref[g]))
    mixed = jnp.concatenate(parts, axis=1) * scale_ref[...]
    o_ref[...] = y + mod_ref[0, 2:3, :] * mixed


def _pool(y, mod, gain, w_grp, scale, n_seq, s_len, row_off, mod_off):
    t_all, d = y.shape
    blk_off = row_off // s_len
    return pl.pallas_call(
        _pool_kernel,
        grid=(n_seq,),
        in_specs=[
            pl.BlockSpec((s_len, d), lambda b: (blk_off + b, 0)),
            pl.BlockSpec((1, 6, d), lambda b: (mod_off(b), 0, 0)),
            pl.BlockSpec((1, d), lambda b: (0, 0)),
            pl.BlockSpec(w_grp.shape, lambda b: (0, 0, 0)),
            pl.BlockSpec((1, d), lambda b: (0, 0)),
        ],
        out_specs=pl.BlockSpec((s_len, d), lambda b: (blk_off + b, 0)),
        out_shape=jax.ShapeDtypeStruct((t_all, d), F32),
        input_output_aliases={0: 0},
        compiler_params=_params(),
        name="pool_mixer",
    )(y, mod, gain.reshape(1, d), w_grp.astype(BF16), scale.reshape(1, d))


def _route_kernel(y_ref, mod_ref, gain_ref, rw_ref, rb_ref, h_ref, route_ref, cnt_ref, carry):
    i = pl.program_id(0)

    @pl.when(i == 0)
    def _():
        carry[...] = jnp.zeros_like(carry)

    y = y_ref[...]
    tm = y.shape[0]
    h = _norm_mod(y, gain_ref[...], mod_ref[0, 3:4, :], mod_ref[0, 4:5, :])
    h_ref[...] = h
    logits = jnp.dot(h, rw_ref[...], precision=HIGHEST, preferred_element_type=F32) + rb_ref[...]
    lane = lax.broadcasted_iota(jnp.int32, logits.shape, 1).astype(F32)
    work = logits
    vals, ids = [], []
    for _ in range(TOP_K):
        m = jnp.max(work, axis=-1, keepdims=True)
        idx = jnp.min(jnp.where(work == m, lane, float(LANES)), axis=-1, keepdims=True)
        vals.append(m)
        ids.append(idx)
        work = jnp.where(lane == idx, -jnp.inf, work)
    exps = [jnp.exp(v - vals[0]) for v in vals]
    denom = functools.reduce(jnp.add, exps)
    onehot = functools.reduce(jnp.add, [(lane == idx).astype(F32) for idx in ids])
    row = lax.broadcasted_iota(jnp.int32, (tm, tm), 0)
    col = lax.broadcasted_iota(jnp.int32, (tm, tm), 1)
    before = (col < row).astype(BF16)
    rank_mat = _dot(before, onehot.astype(BF16)) + carry[0:1, :]
    route = jnp.zeros(logits.shape, F32)
    for k in range(TOP_K):
        rank_k = jnp.sum(jnp.where(lane == ids[k], rank_mat, 0.0), axis=-1, keepdims=True)
        route = jnp.where(lane == k, exps[k] / denom, route)
        route = jnp.where(lane == TOP_K + k, ids[k], route)
        route = jnp.where(lane == 2 * TOP_K + k, rank_k, route)
    route_ref[...] = route
    carry[...] = carry[...] + jnp.sum(onehot, axis=0, keepdims=True)
    cnt_ref[...] = carry[...]


def _route(y, mod, gain, router_w, router_b, mod_row):
    t_all, d = y.shape
    n_exp = router_w.shape[1]
    rw = jnp.pad(router_w, ((0, 0), (0, LANES - n_exp)))
    rb = jnp.pad(router_b, (0, LANES - n_exp), constant_values=-1e30).reshape(1, LANES)
    return pl.pallas_call(
        _route_kernel,
        grid=(t_all // TM,),
        in_specs=[
            pl.BlockSpec((TM, d), lambda i: (i, 0)),
            pl.BlockSpec((1, 6, d), lambda i: (mod_row(i), 0, 0)),
            pl.BlockSpec((1, d), lambda i: (0, 0)),
            pl.BlockSpec((d, LANES), lambda i: (0, 0)),
            pl.BlockSpec((1, LANES), lambda i: (0, 0)),
        ],
        out_specs=[
            pl.BlockSpec((TM, d), lambda i: (i, 0)),
            pl.BlockSpec((TM, LANES), lambda i: (i, 0)),
            pl.BlockSpec((SUBLANES, LANES), lambda i: (0, 0)),
        ],
        out_shape=[
            jax.ShapeDtypeStruct((t_all, d), F32),
            jax.ShapeDtypeStruct((t_all, LANES), F32),
            jax.ShapeDtypeStruct((SUBLANES, LANES), F32),
        ],
        scratch_shapes=[pltpu.VMEM((SUBLANES, LANES), F32)],
        compiler_params=_params(),
        name="moe_route",
    )(y, mod, gain.reshape(1, d), rw, rb)


def _expert_kernel(be_ref, nu_ref, meta_ref, h_ref, wgu_ref, bgu_ref, wdn_ref, bdn_ref, yk_ref,
                   wgu_bf, wdn_bf, x0, x1, y0, y1, meta_s, gsem, ssem, msem):
    i = pl.program_id(0)
    n_used = nu_ref[0]
    n_blocks = pl.num_programs(0)
    rows, _ = x0.shape
    d_exp = wdn_bf.shape[0]
    n_assign = yk_ref.shape[0] - 2 * rows

    def meta_copy(blk, s):
        entry = jnp.minimum(blk, n_blocks - 1) + 1
        return pltpu.make_async_copy(meta_ref.at[pl.ds(entry * META_STRIDE, META_STRIDE)],
                                     meta_s.at[pl.ds(s * META_STRIDE, META_STRIDE)], msem.at[s])

    def issue_gather(xbuf, sem, s):
        for j in range(rows):
            tok = meta_s[s * META_STRIDE + j]
            pltpu.make_async_copy(h_ref.at[pl.ds(tok, 1)], xbuf.at[pl.ds(j, 1)], sem).start()

    def issue_scatter(ybuf, sem, s):
        for j in range(rows):
            dst = meta_s[s * META_STRIDE + rows + j]
            pltpu.make_async_copy(ybuf.at[pl.ds(j, 1)], yk_ref.at[pl.ds(dst, 1)], sem).start()

    def wait_rows(buf, sem):
        pltpu.make_async_copy(buf, buf, sem).wait()

    def run_block(xcur, ycur, xoth, yoth, p):
        q = 1 - p
        s_next, s_prev, s_next2 = (i + 1) % 4, (i + 3) % 4, (i + 2) % 4
        meta_copy(i + 1, s_next).wait()
        wait_rows(xcur, gsem.at[p])
        wait_rows(ycur, ssem.at[p])
        issue_gather(xoth, gsem.at[q], s_next)
        issue_scatter(yoth, ssem.at[q], s_prev)
        meta_copy(i + 2, s_next2).start()

        x = xcur[...].astype(BF16)
        gu = _dot(x, wgu_bf[...]) + bgu_ref[0, 0]
        gate = jnp.minimum(gu[:, :d_exp], SWIGLU_LIMIT)
        up = jnp.clip(gu[:, d_exp:], -SWIGLU_LIMIT, SWIGLU_LIMIT)
        hid = (up + 1.0) * gate * jax.nn.sigmoid(SWIGLU_ALPHA * gate)
        ycur[...] = _dot(hid.astype(BF16), wdn_bf[...]) + bdn_ref[0, 0]

        @pl.when(i == n_used - 1)
        def _():
            wait_rows(xoth, gsem.at[q])
            meta_copy(i + 2, s_next2).wait()
            issue_scatter(ycur, ssem.at[p], i % 4)
            wait_rows(yoth, ssem.at[q])
            wait_rows(ycur, ssem.at[p])

    @pl.when(i < n_used)
    def _():
        @pl.when(i == 0)
        def _():
            before, first = meta_copy(-1, 3), meta_copy(0, 0)
            before.start()
            first.start()
            before.wait()
            first.wait()
            meta_copy(1, 1).start()
            y0[...] = jnp.zeros_like(y0)
            y1[...] = jnp.zeros_like(y1)
            for j in range(rows):
                pltpu.make_async_copy(y0.at[pl.ds(j, 1)], yk_ref.at[pl.ds(n_assign + j, 1)],
                                      ssem.at[0]).start()
            issue_gather(x0, gsem.at[0], 0)

        new_expert = jnp.logical_or(i == 0, be_ref[i] != be_ref[jnp.maximum(i - 1, 0)])

        @pl.when(new_expert)
        def _():
            wgu_bf[...] = wgu_ref[0, 0].astype(BF16)
            wdn_bf[...] = wdn_ref[0, 0].astype(BF16)

        @pl.when(i % 2 == 0)
        def _():
            run_block(x0, y0, x1, y1, 0)

        @pl.when(i % 2 == 1)
        def _():
            run_block(x1, y1, x0, y0, 1)


def _experts(layer, h, meta, block_e, n_used, w_gu, b_gu, w_dn, b_dn, n_blocks, out_rows):
    t_all, d = h.shape
    depth, n_exp, _, two_f = w_gu.shape
    d_exp = two_f // 2
    grid_spec = pltpu.PrefetchScalarGridSpec(
        num_scalar_prefetch=2,
        grid=(n_blocks,),
        in_specs=[
            pl.BlockSpec(memory_space=pl.ANY),
            pl.BlockSpec(memory_space=pl.ANY),
            pl.BlockSpec((1, 1, d, two_f), lambda i, be, nu: (layer, be[i], 0, 0)),
            pl.BlockSpec((1, 1, 1, two_f), lambda i, be, nu: (layer, be[i], 0, 0)),
            pl.BlockSpec((1, 1, d_exp, d), lambda i, be, nu: (layer, be[i], 0, 0)),
            pl.BlockSpec((1, 1, 1, d), lambda i, be, nu: (layer, be[i], 0, 0)),
        ],
        out_specs=pl.BlockSpec(memory_space=pl.ANY),
        scratch_shapes=[
            pltpu.VMEM((d, two_f), BF16),
            pltpu.VMEM((d_exp, d), BF16),
            pltpu.VMEM((MOE_ROWS, d), F32),
            pltpu.VMEM((MOE_ROWS, d), F32),
            pltpu.VMEM((MOE_ROWS, d), F32),
            pltpu.VMEM((MOE_ROWS, d), F32),
            pltpu.SMEM((4 * META_STRIDE,), jnp.int32),
            pltpu.SemaphoreType.DMA((2,)),
            pltpu.SemaphoreType.DMA((2,)),
            pltpu.SemaphoreType.DMA((4,)),
        ],
    )
    return pl.pallas_call(
        _expert_kernel,
        grid_spec=grid_spec,
        out_shape=jax.ShapeDtypeStruct((out_rows, d), F32),
        compiler_params=_params(),
        name="moe_experts",
    )(block_e, n_used, meta, h, w_gu, b_gu.reshape(depth, n_exp, 1, two_f),
      w_dn, b_dn.reshape(depth, n_exp, 1, d))


def _combine_kernel(y_ref, mod_ref, route_ref, y0_ref, y1_ref, y2_ref, y3_ref, o_ref):
    route = route_ref[...]
    acc = route[:, 0:1] * y0_ref[...]
    for k, ref in ((1, y1_ref), (2, y2_ref), (3, y3_ref)):
        acc = acc + route[:, k:k + 1] * ref[...]
    o_ref[...] = y_ref[...] + mod_ref[0, 5:6, :] * acc


def _combine(y, mod, route, yk, mod_row):
    t_all, d = y.shape
    nt = t_all // TM
    yk_specs = [pl.BlockSpec((TM, d), functools.partial(lambda i, k: (k * nt + i, 0), k=k))
                for k in range(TOP_K)]
    return pl.pallas_call(
        _combine_kernel,
        grid=(nt,),
        in_specs=[
            pl.BlockSpec((TM, d), lambda i: (i, 0)),
            pl.BlockSpec((1, 6, d), lambda i: (mod_row(i), 0, 0)),
            pl.BlockSpec((TM, LANES), lambda i: (i, 0)),
        ] + yk_specs,
        out_specs=pl.BlockSpec((TM, d), lambda i: (i, 0)),
        out_shape=jax.ShapeDtypeStruct((t_all, d), F32),
        compiler_params=_params(),
        name="moe_combine",
    )(y, mod, route, yk, yk, yk, yk)


def _moe(layer, y, mod, gain, router_w, router_b, w_gu, b_gu, w_dn, b_dn, mod_row):
    t_all, d = y.shape
    n_exp = router_w.shape[1]
    n_assign = t_all * TOP_K
    n_blocks = n_assign // MOE_ROWS + n_exp
    n_slots = n_blocks * MOE_ROWS

    h, route, cnt = _route(y, mod, gain, router_w, router_b, mod_row)

    counts = cnt[0, :n_exp].astype(jnp.int32)
    padded = (counts + MOE_ROWS - 1) // MOE_ROWS * MOE_ROWS
    end_padded = jnp.cumsum(padded)
    start_padded = end_padded - padded
    ids = route[:, TOP_K:2 * TOP_K].astype(jnp.int32)
    rank = route[:, 2 * TOP_K:3 * TOP_K].astype(jnp.int32)
    dest = (start_padded[ids] + rank).reshape(-1)
    assign = jnp.full((n_slots,), -1, jnp.int32).at[dest].set(
        jnp.arange(n_assign, dtype=jnp.int32), unique_indices=True, mode="promise_in_bounds")
    real = assign >= 0
    tok = assign // TOP_K
    slot_ids = jnp.arange(n_slots, dtype=jnp.int32)
    scratch_row = n_assign + (slot_ids // MOE_ROWS % 2) * MOE_ROWS + slot_ids % MOE_ROWS
    src = jnp.where(real, tok, 0)
    dst = jnp.where(real, (assign % TOP_K) * t_all + tok, scratch_row)
    meta = jnp.concatenate([src.reshape(n_blocks, MOE_ROWS), dst.reshape(n_blocks, MOE_ROWS)], axis=1)
    before = jnp.concatenate([jnp.zeros((1, MOE_ROWS), jnp.int32),
                              (n_assign + MOE_ROWS + jnp.arange(MOE_ROWS, dtype=jnp.int32))[None]], axis=1)
    meta = jnp.pad(jnp.concatenate([before, meta], axis=0),
                   ((0, 0), (0, META_STRIDE - 2 * MOE_ROWS))).reshape(-1)
    n_used = (end_padded[-1] // MOE_ROWS).astype(jnp.int32)
    blk = jnp.arange(n_blocks, dtype=jnp.int32)
    block_e = jnp.sum((end_padded[None, :] <= (blk * MOE_ROWS)[:, None]).astype(jnp.int32), axis=1)
    block_e = jnp.minimum(block_e, n_exp - 1)
    block_e = jnp.where(blk < n_used, block_e, block_e[jnp.maximum(n_used - 1, 0)]).astype(jnp.int32)

    yk = _experts(layer, h, meta, block_e, n_used.reshape(1), w_gu, b_gu, w_dn, b_dn,
                  n_blocks, n_assign + 2 * MOE_ROWS)
    return _combine(y, mod, route, yk, mod_row)


def kernel(x_prompt, x_sample, cache_k, cache_v, c, c_ctx, w_mod, b_mod, norm_mix, norm_ffn, gm_w_in, gm_norm_v, gm_w_s, gm_b_s, gm_w_out, at_w_qkv, at_q_norm, at_k_norm, at_w_o, pool_w_grp, pool_scale, router_w, router_b, w_gate_up, b_gate_up, w_down, b_down):
    n_ctx, ctx_len, d = x_prompt.shape
    n_lat, lat_len, _ = x_sample.shape
    depth = w_mod.shape[0]
    tp = n_ctx * ctx_len
    assert tp % lat_len == 0 and ctx_len % TM == 0 and lat_len % TM == 0 and ctx_len == TM
    assert 1 + n_lat <= SUBLANES

    y = jnp.concatenate([x_prompt.reshape(tp, d), x_sample.reshape(n_lat * lat_len, d)], axis=0)

    cvecs = jnp.zeros((SUBLANES, d), F32).at[0].set(c_ctx).at[1:1 + n_lat].set(c)
    mod_all = jnp.transpose(_ada_params(cvecs, w_mod, b_mod), (0, 2, 1, 3))

    def mod_row(i):
        return jnp.where(i * TM < tp, 0, 1 + (i * TM - tp) // lat_len)

    half = HEAD_DIM // 4
    inv = ROPE_THETA ** (-jnp.arange(half, dtype=F32) / half)
    pos = jnp.arange(lat_len)
    ang_r = (pos // GRID_W).astype(F32)[:, None] * inv[None, :]
    ang_c = (pos % GRID_W).astype(F32)[:, None] * inv[None, :]
    cos_t = jnp.concatenate([jnp.ones((TM, HEAD_DIM), F32),
                             jnp.concatenate([jnp.cos(ang_r)] * 2 + [jnp.cos(ang_c)] * 2, axis=1)], axis=0)
    sin_t = jnp.concatenate([jnp.zeros((TM, HEAD_DIM), F32),
                             jnp.concatenate([-jnp.sin(ang_r), jnp.sin(ang_r),
                                              -jnp.sin(ang_c), jnp.sin(ang_c)], axis=1)], axis=0)

    def rope_row(i):
        return jnp.where(i * TM < tp, 0, 1 + ((i * TM - tp) % lat_len) // TM)

    new_k, new_v = [], []
    for l in range(depth):
        kind, j = l % 3, l // 3
        mod = mod_all[l]
        if kind == 0:
            y = _gmlp(y, mod, norm_mix[l], gm_w_in[j], gm_norm_v[j], gm_w_s[j], gm_b_s[j], gm_w_out[j], mod_row)
        elif kind == 1:
            q, k, v = _qkv(y, mod, norm_mix[l], at_w_qkv[j], at_q_norm[j], at_k_norm[j],
                           cos_t, sin_t, mod_row, rope_row)
            new_k.append(k[:tp].reshape(n_ctx, ctx_len, N_KV_HEADS, HEAD_DIM))
            new_v.append(v[:tp].reshape(n_ctx, ctx_len, N_KV_HEADS, HEAD_DIM))
            past = cache_k.shape[2]
            ck = cache_k[:, j].reshape(n_lat, past, N_KV_HEADS * HEAD_DIM)
            cv = cache_v[:, j].reshape(n_lat, past, N_KV_HEADS * HEAD_DIM)
            y = _attention(y, mod, q, k, v, ck, cv, at_w_o[j], n_ctx, ctx_len, n_lat, lat_len)
        else:
            y = _pool(y, mod, norm_mix[l], pool_w_grp[j], pool_scale[j], n_ctx, ctx_len, 0, lambda b: 0)
            y = _pool(y, mod, norm_mix[l], pool_w_grp[j], pool_scale[j], n_lat, lat_len, tp, lambda b: 1 + b)
        y = _moe(l, y, mod, norm_ffn[l], router_w[l], router_b[l], w_gate_up, b_gate_up,
                 w_down, b_down, mod_row)

    return (y[:tp].reshape(n_ctx, ctx_len, d), y[tp:].reshape(n_lat, lat_len, d),
            jnp.stack(new_k, axis=1), jnp.stack(new_v, axis=1))
```

```python
import functools

import jax
import jax.numpy as jnp
from jax import lax
from jax.experimental import pallas as pl
from jax.experimental.pallas import tpu as pltpu

F32 = jnp.float32
BF16 = jnp.bfloat16
HIGHEST = lax.Precision.HIGHEST

LANES = 128
SUBLANES = 8
VMEM_LIMIT_BYTES = 56 * 1024 * 1024

EPS = 1e-6
GRID_W = 64
CHUNK = 128
HEAD_DIM = 128
N_KV_HEADS = 2
ROPE_THETA = 10000.0
POOL_WINDOWS = (2, 4, 8, 16)
TOP_K = 4
SWIGLU_ALPHA = 1.702
SWIGLU_LIMIT = 7.0

TM = 256
MOE_ROWS = 256
UNIT = SUBLANES


def _params(n_axes=1):
    return pltpu.CompilerParams(dimension_semantics=("arbitrary",) * n_axes,
                                vmem_limit_bytes=VMEM_LIMIT_BYTES)


def _rms(x, gain):
    return x * lax.rsqrt(jnp.mean(x * x, axis=-1, keepdims=True) + EPS) * gain


def _norm_mod(y, gain, shift, scale):
    return _rms(y, gain) * (1.0 + scale) + shift


def _gelu_tanh(x):
    return 0.5 * x * (1.0 + jnp.tanh(0.7978845608028654 * (x + 0.044715 * (x * x * x))))


def _dot(a, b):
    return jnp.dot(a, b, preferred_element_type=F32)


def _dot_nt(a, b):
    return lax.dot_general(a, b, (((1,), (1,)), ((), ())), preferred_element_type=F32)


def _ada_kernel(cv_ref, w_ref, b_ref, o_ref):
    cv = cv_ref[...]
    s = cv * jax.nn.sigmoid(cv)
    o_ref[0, 0] = jnp.dot(s, w_ref[0], precision=HIGHEST, preferred_element_type=F32) + b_ref[0, 0]


def _ada_params(cvecs, w_mod, b_mod):
    depth, d, _ = w_mod.shape
    rows = cvecs.shape[0]
    return pl.pallas_call(
        _ada_kernel,
        grid=(depth, 6),
        in_specs=[
            pl.BlockSpec((rows, d), lambda l, j: (0, 0)),
            pl.BlockSpec((1, d, d), lambda l, j: (l, 0, j)),
            pl.BlockSpec((1, 1, 1, d), lambda l, j: (l, j, 0, 0)),
        ],
        out_specs=pl.BlockSpec((1, 1, rows, d), lambda l, j: (l, j, 0, 0)),
        out_shape=jax.ShapeDtypeStruct((depth, 6, rows, d), F32),
        compiler_params=_params(2),
        name="ada_params",
    )(cvecs, w_mod, b_mod.reshape(depth, 6, 1, d))


def _gmlp_kernel(y_ref, mod_ref, gain_ref, win_ref, nv_ref, ws_ref, bexp_ref, wout_ref, o_ref):
    y = y_ref[...]
    inner = nv_ref.shape[-1]
    groups = ws_ref.shape[0]
    h = _norm_mod(y, gain_ref[...], mod_ref[0, 0:1, :], mod_ref[0, 1:2, :])
    z = _gelu_tanh(_dot(h.astype(BF16), win_ref[...]))
    u = z[:, :inner]
    v = _rms(z[:, inner:], nv_ref[...]).astype(BF16)
    bexp = bexp_ref[...]
    chunks = []
    for c in range(y.shape[0] // CHUNK):
        cols = []
        for g in range(groups):
            vg = v[c * CHUNK:(c + 1) * CHUNK, g * LANES:(g + 1) * LANES]
            cols.append(_dot(ws_ref[g], vg))
        chunks.append(jnp.concatenate(cols, axis=1) + bexp)
    mixed = jnp.concatenate(chunks, axis=0)
    o = _dot((u * mixed).astype(BF16), wout_ref[...])
    o_ref[...] = y + mod_ref[0, 2:3, :] * o


def _gmlp(y, mod, gain, w_in, norm_v, w_s, b_s, w_out, mod_row):
    t_all, d = y.shape
    inner = norm_v.shape[-1]
    groups = w_s.shape[0]
    bexp = jnp.repeat(b_s.T, inner // groups, axis=1)
    return pl.pallas_call(
        _gmlp_kernel,
        grid=(t_all // TM,),
        in_specs=[
            pl.BlockSpec((TM, d), lambda i: (i, 0)),
            pl.BlockSpec((1, 6, d), lambda i: (mod_row(i), 0, 0)),
            pl.BlockSpec((1, d), lambda i: (0, 0)),
            pl.BlockSpec((d, 2 * inner), lambda i: (0, 0)),
            pl.BlockSpec((1, inner), lambda i: (0, 0)),
            pl.BlockSpec((groups, CHUNK, CHUNK), lambda i: (0, 0, 0)),
            pl.BlockSpec((CHUNK, inner), lambda i: (0, 0)),
            pl.BlockSpec((inner, d), lambda i: (0, 0)),
        ],
        out_specs=pl.BlockSpec((TM, d), lambda i: (i, 0)),
        out_shape=jax.ShapeDtypeStruct((t_all, d), F32),
        compiler_params=_params(),
        name="gmlp",
    )(y, mod, gain.reshape(1, d), w_in.astype(BF16), norm_v.reshape(1, inner),
      w_s.astype(BF16), bexp, w_out.astype(BF16))


def _qkv_kernel(y_ref, mod_ref, gain_ref, w_ref, qn_ref, kn_ref, cos_ref, sin_ref,
                q_ref, k_ref, v_ref):
    y = y_ref[...]
    n_q = q_ref.shape[-1] // HEAD_DIM
    n_kv = k_ref.shape[-1] // HEAD_DIM
    h = _norm_mod(y, gain_ref[...], mod_ref[0, 0:1, :], mod_ref[0, 1:2, :])
    qkv = _dot(h.astype(BF16), w_ref[...])
    cos = cos_ref[...]
    sin = sin_ref[...]
    lane = lax.broadcasted_iota(jnp.int32, cos.shape, 1)
    low_half = (lane % (HEAD_DIM // 2)) < (HEAD_DIM // 4)

    def head(idx, norm):
        x = _rms(qkv[:, idx * HEAD_DIM:(idx + 1) * HEAD_DIM], norm)
        partner = jnp.where(low_half,
                            pltpu.roll(x, HEAD_DIM - HEAD_DIM // 4, 1),
                            pltpu.roll(x, HEAD_DIM // 4, 1))
        return x * cos + partner * sin

    qn = qn_ref[...]
    kn = kn_ref[...]
    q_ref[...] = jnp.concatenate([head(i, qn) for i in range(n_q)], axis=1).astype(q_ref.dtype)
    k_ref[...] = jnp.concatenate([head(n_q + i, kn) for i in range(n_kv)], axis=1)
    v_ref[...] = qkv[:, (n_q + n_kv) * HEAD_DIM:]


def _qkv(y, mod, gain, w_qkv, q_norm, k_norm, cos_t, sin_t, mod_row, rope_row):
    t_all, d = y.shape
    n_kv_cols = N_KV_HEADS * HEAD_DIM
    return pl.pallas_call(
        _qkv_kernel,
        grid=(t_all // TM,),
        in_specs=[
            pl.BlockSpec((TM, d), lambda i: (i, 0)),
            pl.BlockSpec((1, 6, d), lambda i: (mod_row(i), 0, 0)),
            pl.BlockSpec((1, d), lambda i: (0, 0)),
            pl.BlockSpec(w_qkv.shape, lambda i: (0, 0)),
            pl.BlockSpec((1, HEAD_DIM), lambda i: (0, 0)),
            pl.BlockSpec((1, HEAD_DIM), lambda i: (0, 0)),
            pl.BlockSpec((TM, HEAD_DIM), lambda i: (rope_row(i), 0)),
            pl.BlockSpec((TM, HEAD_DIM), lambda i: (rope_row(i), 0)),
        ],
        out_specs=[
            pl.BlockSpec((TM, d), lambda i: (i, 0)),
            pl.BlockSpec((TM, n_kv_cols), lambda i: (i, 0)),
            pl.BlockSpec((TM, n_kv_cols), lambda i: (i, 0)),
        ],
        out_shape=[
            jax.ShapeDtypeStruct((t_all, d), BF16),
            jax.ShapeDtypeStruct((t_all, n_kv_cols), F32),
            jax.ShapeDtypeStruct((t_all, n_kv_cols), F32),
        ],
        compiler_params=_params(),
        name="qkv_project",
    )(y, mod, gain.reshape(1, d), w_qkv.astype(BF16), q_norm.reshape(1, HEAD_DIM),
      k_norm.reshape(1, HEAD_DIM), cos_t, sin_t)


def _attend(q, key_sets):
    n_heads = q.shape[1] // HEAD_DIM
    rep = n_heads // N_KV_HEADS
    scale = HEAD_DIM ** -0.5
    outs = []
    for hd in range(n_heads):
        g = hd // rep
        qh = q[:, hd * HEAD_DIM:(hd + 1) * HEAD_DIM]
        cols = slice(g * HEAD_DIM, (g + 1) * HEAD_DIM)
        scores = [_dot_nt(qh, k[:, cols]) * scale for k, _ in key_sets]
        m = functools.reduce(jnp.maximum, [jnp.max(s, axis=-1, keepdims=True) for s in scores])
        probs = [jnp.exp(s - m) for s in scores]
        denom = functools.reduce(jnp.add, [jnp.sum(p, axis=-1, keepdims=True) for p in probs])
        o = functools.reduce(jnp.add, [_dot(p.astype(BF16), v[:, cols])
                                       for p, (_, v) in zip(probs, key_sets)])
        outs.append(o / denom)
    return jnp.concatenate(outs, axis=1)


def _attn_ctx_kernel(y_ref, mod_ref, q_ref, k_ref, v_ref, wo_ref, o_ref):
    att = _attend(q_ref[...], [(k_ref[...].astype(BF16), v_ref[...].astype(BF16))])
    o_ref[...] = y_ref[...] + mod_ref[0, 2:3, :] * _dot(att.astype(BF16), wo_ref[...])


def _attn_lat_kernel(y_ref, mod_ref, q_ref, k_ref, v_ref, ck_ref, cv_ref, wo_ref, o_ref):
    sets = [(ck_ref[0].astype(BF16), cv_ref[0].astype(BF16)),
            (k_ref[...].astype(BF16), v_ref[...].astype(BF16))]
    att = _attend(q_ref[...], sets)
    o_ref[...] = y_ref[...] + mod_ref[0, 2:3, :] * _dot(att.astype(BF16), wo_ref[...])


def _attention(y, mod, q, k, v, cache_k, cache_v, w_o, n_ctx, ctx_len, n_lat, lat_len):
    t_all, d = y.shape
    kvc = k.shape[1]
    tp = n_ctx * ctx_len
    wo = w_o.astype(BF16)
    y = pl.pallas_call(
        _attn_ctx_kernel,
        grid=(n_ctx,),
        in_specs=[
            pl.BlockSpec((ctx_len, d), lambda b: (b, 0)),
            pl.BlockSpec((1, 6, d), lambda b: (0, 0, 0)),
            pl.BlockSpec((ctx_len, d), lambda b: (b, 0)),
            pl.BlockSpec((ctx_len, kvc), lambda b: (b, 0)),
            pl.BlockSpec((ctx_len, kvc), lambda b: (b, 0)),
            pl.BlockSpec((d, d), lambda b: (0, 0)),
        ],
        out_specs=pl.BlockSpec((ctx_len, d), lambda b: (b, 0)),
        out_shape=jax.ShapeDtypeStruct((t_all, d), F32),
        input_output_aliases={0: 0},
        compiler_params=_params(),
        name="attn_context",
    )(y, mod, q, k, v, wo)
    tq = TM
    qb = lat_len // tq
    past = cache_k.shape[1]
    y = pl.pallas_call(
        _attn_lat_kernel,
        grid=(n_lat, qb),
        in_specs=[
            pl.BlockSpec((tq, d), lambda b, i: (tp // tq + b * qb + i, 0)),
            pl.BlockSpec((1, 6, d), lambda b, i: (1 + b, 0, 0)),
            pl.BlockSpec((tq, d), lambda b, i: (tp // tq + b * qb + i, 0)),
            pl.BlockSpec((lat_len, kvc), lambda b, i: (tp // lat_len + b, 0)),
            pl.BlockSpec((lat_len, kvc), lambda b, i: (tp // lat_len + b, 0)),
            pl.BlockSpec((1, past, kvc), lambda b, i: (b, 0, 0)),
            pl.BlockSpec((1, past, kvc), lambda b, i: (b, 0, 0)),
            pl.BlockSpec((d, d), lambda b, i: (0, 0)),
        ],
        out_specs=pl.BlockSpec((tq, d), lambda b, i: (tp // tq + b * qb + i, 0)),
        out_shape=jax.ShapeDtypeStruct((t_all, d), F32),
        input_output_aliases={0: 0},
        compiler_params=_params(2),
        name="attn_latent",
    )(y, mod, q, k, v, cache_k, cache_v, wo)
    return y


def _pool_kernel(y_ref, mod_ref, gain_ref, wg_ref, scale_ref, o_ref):
    y = y_ref[...]
    s_len, d = y.shape
    gd = d // len(POOL_WINDOWS)
    h = _norm_mod(y, gain_ref[...], mod_ref[0, 0:1, :], mod_ref[0, 1:2, :])
    t_idx = lax.broadcasted_iota(jnp.int32, (s_len, s_len), 0)
    s_idx = lax.broadcasted_iota(jnp.int32, (s_len, s_len), 1)
    t_col = lax.broadcasted_iota(jnp.int32, (s_len, 1), 0)
    parts = []
    for g, w in enumerate(POOL_WINDOWS):
        lo = t_idx - w // 2
        window = ((s_idx >= lo) & (s_idx < lo + w)).astype(BF16)
        count = (jnp.minimum(t_col - w // 2 + w, s_len) - jnp.maximum(t_col - w // 2, 0)).astype(F32)
        hg = h[:, g * gd:(g + 1) * gd]
        h_hi = hg.astype(BF16)
        h_lo = (hg - h_hi.astype(F32)).astype(BF16)
        mean = (_dot(window, h_hi) + _dot(window, h_lo)) / count
        parts.append(_dot((mean - hg).astype(BF16), wg_ref[g]))
    mixed = jnp.concatenate(parts, axis=1) * scale_ref[...]
    o_ref[...] = y + mod_ref[0, 2:3, :] * mixed


def _pool(y, mod, gain, w_grp, scale, n_seq, s_len, row_off, mod_off):
    t_all, d = y.shape
    blk_off = row_off // s_len
    return pl.pallas_call(
        _pool_kernel,
        grid=(n_seq,),
        in_specs=[
            pl.BlockSpec((s_len, d), lambda b: (blk_off + b, 0)),
            pl.BlockSpec((1, 6, d), lambda b: (mod_off(b), 0, 0)),
            pl.BlockSpec((1, d), lambda b: (0, 0)),
            pl.BlockSpec(w_grp.shape, lambda b: (0, 0, 0)),
            pl.BlockSpec((1, d), lambda b: (0, 0)),
        ],
        out_specs=pl.BlockSpec((s_len, d), lambda b: (blk_off + b, 0)),
        out_shape=jax.ShapeDtypeStruct((t_all, d), F32),
        input_output_aliases={0: 0},
        compiler_params=_params(),
        name="pool_mixer",
    )(y, mod, gain.reshape(1, d), w_grp.astype(BF16), scale.reshape(1, d))


def _sorted_rows(n_exp):
    worst = TM * TOP_K + n_exp * (UNIT - 1)
    return -(-worst // LANES) * LANES


def _route_kernel(y_ref, mod_ref, gain_ref, rw_ref, rb_ref, xs_ref, route_ref, cnt_ref):
    y = y_ref[...]
    tm = y.shape[0]
    n_rows = xs_ref.shape[0]
    h = _norm_mod(y, gain_ref[...], mod_ref[0, 3:4, :], mod_ref[0, 4:5, :])
    h_hi = h.astype(BF16)
    h_lo = (h - h_hi.astype(F32)).astype(BF16)
    hi_terms = _dot(h_hi, rw_ref[...])
    logits = hi_terms[:, :LANES] + hi_terms[:, LANES:] + _dot(h_lo, rw_ref[:, :LANES]) + rb_ref[...]
    lane = lax.broadcasted_iota(jnp.int32, logits.shape, 1).astype(F32)
    work = logits
    vals, ids = [], []
    for _ in range(TOP_K):
        m = jnp.max(work, axis=-1, keepdims=True)
        idx = jnp.min(jnp.where(work == m, lane, float(LANES)), axis=-1, keepdims=True)
        vals.append(m)
        ids.append(idx)
        work = jnp.where(lane == idx, -jnp.inf, work)
    exps = [jnp.exp(v - vals[0]) for v in vals]
    denom = functools.reduce(jnp.add, exps)
    onehot = functools.reduce(jnp.add, [(lane == idx).astype(F32) for idx in ids])

    counts = jnp.sum(onehot, axis=0, keepdims=True)
    seg = jnp.ceil(counts * (1.0 / UNIT)) * UNIT
    e_row = lax.broadcasted_iota(jnp.int32, (LANES, LANES), 0)
    e_col = lax.broadcasted_iota(jnp.int32, (LANES, LANES), 1)
    earlier = (e_row < e_col).astype(BF16)
    seg_off = _dot(jnp.broadcast_to(seg, (SUBLANES, LANES)).astype(BF16), earlier)[0:1, :]
    row = lax.broadcasted_iota(jnp.int32, (tm, tm), 0)
    col = lax.broadcasted_iota(jnp.int32, (tm, tm), 1)
    before = (col < row).astype(BF16)
    pos_mat = _dot(before, onehot.astype(BF16)) + seg_off

    route = jnp.zeros(logits.shape, F32)
    for k in range(TOP_K):
        pos_k = jnp.sum(jnp.where(lane == ids[k], pos_mat, 0.0), axis=-1, keepdims=True)
        route = jnp.where(lane == k, exps[k] / denom, route)
        route = jnp.where(lane == TOP_K + k, ids[k], route)
        route = jnp.where(lane == 2 * TOP_K + k, pos_k, route)
    route_ref[...] = route
    cnt_ref[...] = jnp.broadcast_to(counts, cnt_ref.shape)

    route_t = jnp.transpose(route)
    out_row = lax.broadcasted_iota(jnp.int32, (n_rows, tm), 0).astype(F32)
    perm = functools.reduce(jnp.add, [(out_row == route_t[2 * TOP_K + k:2 * TOP_K + k + 1, :]).astype(F32)
                                      for k in range(TOP_K)])
    xs_ref[...] = _dot(perm.astype(BF16), h_hi)


def _route(y, mod, gain, router_w, router_b, mod_row):
    t_all, d = y.shape
    n_exp = router_w.shape[1]
    n_tiles = t_all // TM
    n_rows = _sorted_rows(n_exp)
    rw = jnp.pad(router_w, ((0, 0), (0, LANES - n_exp)))
    rw_hi = rw.astype(BF16)
    rw = jnp.concatenate([rw_hi, (rw - rw_hi.astype(F32)).astype(BF16)], axis=1)
    rb = jnp.pad(router_b, (0, LANES - n_exp), constant_values=-1e30).reshape(1, LANES)
    return pl.pallas_call(
        _route_kernel,
        grid=(n_tiles,),
        in_specs=[
            pl.BlockSpec((TM, d), lambda i: (i, 0)),
            pl.BlockSpec((1, 6, d), lambda i: (mod_row(i), 0, 0)),
            pl.BlockSpec((1, d), lambda i: (0, 0)),
            pl.BlockSpec((d, 2 * LANES), lambda i: (0, 0)),
            pl.BlockSpec((1, LANES), lambda i: (0, 0)),
        ],
        out_specs=[
            pl.BlockSpec((n_rows, d), lambda i: (i, 0)),
            pl.BlockSpec((TM, LANES), lambda i: (i, 0)),
            pl.BlockSpec((SUBLANES, LANES), lambda i: (i, 0)),
        ],
        out_shape=[
            jax.ShapeDtypeStruct((n_tiles * n_rows, d), F32),
            jax.ShapeDtypeStruct((t_all, LANES), F32),
            jax.ShapeDtypeStruct((n_tiles * SUBLANES, LANES), F32),
        ],
        compiler_params=_params(),
        name="moe_route",
    )(y, mod, gain.reshape(1, d), rw, rb)


def _issue_units(src_ref, base, table_ref, dst, sem):
    for u in range(dst.shape[0] // UNIT):
        start = pl.multiple_of(table_ref[base + u], UNIT)
        pltpu.make_async_copy(src_ref.at[pl.ds(start, UNIT)], dst.at[pl.ds(u * UNIT, UNIT)], sem).start()


def _wait_units(buf, sem):
    pltpu.make_async_copy(buf, buf, sem).wait()


def _expert_kernel(be_ref, nu_ref, src_ref, xs_ref, wgu_ref, bgu_ref, wdn_ref, bdn_ref, o_ref,
                   wgu_bf, wdn_bf, x0, x1, gsem):
    i = pl.program_id(0)
    n_used = nu_ref[0]
    n_blocks = pl.num_programs(0)
    units = x0.shape[0] // UNIT
    d_exp = wdn_bf.shape[0]

    def run_block(xcur, xoth, p):
        _wait_units(xcur, gsem.at[p])
        _issue_units(xs_ref, jnp.minimum(i + 1, n_blocks - 1) * units, src_ref, xoth, gsem.at[1 - p])
        x = xcur[...].astype(BF16)
        gu = _dot(x, wgu_bf[...]) + bgu_ref[0, 0]
        gate = jnp.minimum(gu[:, :d_exp], SWIGLU_LIMIT)
        up = jnp.clip(gu[:, d_exp:], -SWIGLU_LIMIT, SWIGLU_LIMIT)
        hid = (up + 1.0) * gate * jax.nn.sigmoid(SWIGLU_ALPHA * gate)
        o_ref[...] = _dot(hid.astype(BF16), wdn_bf[...]) + bdn_ref[0, 0]

        @pl.when(i == n_used - 1)
        def _():
            _wait_units(xoth, gsem.at[1 - p])

    @pl.when(i < n_used)
    def _():
        @pl.when(i == 0)
        def _():
            _issue_units(xs_ref, 0, src_ref, x0, gsem.at[0])

        new_expert = jnp.logical_or(i == 0, be_ref[i] != be_ref[jnp.maximum(i - 1, 0)])

        @pl.when(new_expert)
        def _():
            wgu_bf[...] = wgu_ref[0, 0].astype(BF16)
            wdn_bf[...] = wdn_ref[0, 0].astype(BF16)

        @pl.when(i % 2 == 0)
        def _():
            run_block(x0, x1, 0)

        @pl.when(i % 2 == 1)
        def _():
            run_block(x1, x0, 1)


def _experts(layer, xs, src_units, block_e, n_used, w_gu, b_gu, w_dn, b_dn, n_blocks):
    _, d = xs.shape
    depth, n_exp, _, two_f = w_gu.shape
    d_exp = two_f // 2

    def out_block(i, be, nu, src):
        return (jnp.minimum(i, nu[0] - 1), 0)

    grid_spec = pltpu.PrefetchScalarGridSpec(
        num_scalar_prefetch=3,
        grid=(n_blocks,),
        in_specs=[
            pl.BlockSpec(memory_space=pl.ANY),
            pl.BlockSpec((1, 1, d, two_f), lambda i, be, nu, src: (layer, be[i], 0, 0)),
            pl.BlockSpec((1, 1, 1, two_f), lambda i, be, nu, src: (layer, be[i], 0, 0)),
            pl.BlockSpec((1, 1, d_exp, d), lambda i, be, nu, src: (layer, be[i], 0, 0)),
            pl.BlockSpec((1, 1, 1, d), lambda i, be, nu, src: (layer, be[i], 0, 0)),
        ],
        out_specs=pl.BlockSpec((MOE_ROWS, d), out_block),
        scratch_shapes=[
            pltpu.VMEM((d, two_f), BF16),
            pltpu.VMEM((d_exp, d), BF16),
            pltpu.VMEM((MOE_ROWS, d), F32),
            pltpu.VMEM((MOE_ROWS, d), F32),
            pltpu.SemaphoreType.DMA((2,)),
        ],
    )
    return pl.pallas_call(
        _expert_kernel,
        grid_spec=grid_spec,
        out_shape=jax.ShapeDtypeStruct((n_blocks * MOE_ROWS, d), F32),
        compiler_params=_params(),
        name="moe_experts",
    )(block_e, n_used, src_units, xs, w_gu, b_gu.reshape(depth, n_exp, 1, two_f),
      w_dn, b_dn.reshape(depth, n_exp, 1, d))


def _combine_kernel(src_ref, y_ref, mod_ref, route_ref, ys_ref, o_ref, t0, t1, gsem):
    i = pl.program_id(0)
    n_tiles = pl.num_programs(0)
    n_rows = t0.shape[0]
    units = n_rows // UNIT

    def run_tile(tcur, toth, p):
        _wait_units(tcur, gsem.at[p])
        _issue_units(ys_ref, jnp.minimum(i + 1, n_tiles - 1) * units, src_ref, toth, gsem.at[1 - p])
        route = route_ref[...]
        tm = route.shape[0]
        col = lax.broadcasted_iota(jnp.int32, (tm, n_rows), 1).astype(F32)
        weights = functools.reduce(jnp.add, [
            jnp.where(col == route[:, 2 * TOP_K + k:2 * TOP_K + k + 1], route[:, k:k + 1], 0.0)
            for k in range(TOP_K)])
        acc = _dot(weights.astype(BF16), tcur[...].astype(BF16))
        o_ref[...] = y_ref[...] + mod_ref[0, 5:6, :] * acc

        @pl.when(i == n_tiles - 1)
        def _():
            _wait_units(toth, gsem.at[1 - p])

    @pl.when(i == 0)
    def _():
        _issue_units(ys_ref, 0, src_ref, t0, gsem.at[0])

    @pl.when(i % 2 == 0)
    def _():
        run_tile(t0, t1, 0)

    @pl.when(i % 2 == 1)
    def _():
        run_tile(t1, t0, 1)


def _combine(y, mod, route, ys, src_units, n_rows, mod_row):
    t_all, d = y.shape
    grid_spec = pltpu.PrefetchScalarGridSpec(
        num_scalar_prefetch=1,
        grid=(t_all // TM,),
        in_specs=[
            pl.BlockSpec((TM, d), lambda i, src: (i, 0)),
            pl.BlockSpec((1, 6, d), lambda i, src: (mod_row(i), 0, 0)),
            pl.BlockSpec((TM, LANES), lambda i, src: (i, 0)),
            pl.BlockSpec(memory_space=pl.ANY),
        ],
        out_specs=pl.BlockSpec((TM, d), lambda i, src: (i, 0)),
        scratch_shapes=[
            pltpu.VMEM((n_rows, d), F32),
            pltpu.VMEM((n_rows, d), F32),
            pltpu.SemaphoreType.DMA((2,)),
        ],
    )
    return pl.pallas_call(
        _combine_kernel,
        grid_spec=grid_spec,
        out_shape=jax.ShapeDtypeStruct((t_all, d), F32),
        compiler_params=_params(),
        name="moe_combine",
    )(src_units, y, mod, route, ys)


def _moe(layer, y, mod, gain, router_w, router_b, w_gu, b_gu, w_dn, b_dn, mod_row):
    t_all, d = y.shape
    n_exp = router_w.shape[1]
    n_tiles = t_all // TM
    n_rows = _sorted_rows(n_exp)
    blk_units = MOE_ROWS // UNIT
    tile_units = n_rows // UNIT
    n_blocks = -(-(t_all * TOP_K + n_tiles * n_exp * (UNIT - 1)) // MOE_ROWS) + n_exp

    xs, route, cnt = _route(y, mod, gain, router_w, router_b, mod_row)

    c = cnt.reshape(n_tiles, SUBLANES, LANES)[:, 0, :n_exp].astype(jnp.int32)
    seg = (c + UNIT - 1) // UNIT * UNIT
    seg_end_t = jnp.cumsum(seg, axis=1)
    seg_off_t = seg_end_t - seg
    seg_end_e = jnp.cumsum(seg, axis=0)
    seg_off_e = seg_end_e - seg
    rows_e = seg_end_e[-1]
    padded = (rows_e + MOE_ROWS - 1) // MOE_ROWS * MOE_ROWS
    end_e = jnp.cumsum(padded)
    start_e = end_e - padded
    n_used = (end_e[-1] // MOE_ROWS).astype(jnp.int32)
    blk = jnp.arange(n_blocks, dtype=jnp.int32)
    block_e = jnp.sum((end_e[None, :] <= (blk * MOE_ROWS)[:, None]).astype(jnp.int32), axis=1)
    block_e = jnp.minimum(block_e, n_exp - 1)
    block_e = jnp.where(blk < n_used, block_e, block_e[jnp.maximum(n_used - 1, 0)]).astype(jnp.int32)

    r = (blk * MOE_ROWS - start_e[block_e])[:, None] + UNIT * jnp.arange(blk_units, dtype=jnp.int32)[None, :]
    ends_b = seg_end_e[:, block_e].T
    tile_of = jnp.sum((ends_b[:, None, :] <= r[:, :, None]).astype(jnp.int32), axis=2)
    valid = (r < rows_e[block_e][:, None]) & (blk < n_used)[:, None]
    tile_c = jnp.minimum(tile_of, n_tiles - 1)
    off_t = jnp.take_along_axis(seg_off_t[:, block_e].T, tile_c, axis=1)
    off_e = jnp.take_along_axis(seg_off_e[:, block_e].T, tile_c, axis=1)
    src_x = jnp.where(valid, tile_c * n_rows + off_t + r - off_e, 0).astype(jnp.int32).reshape(-1)

    rt = UNIT * jnp.arange(tile_units, dtype=jnp.int32)[None, :]
    exp_of = jnp.sum((seg_end_t[:, None, :] <= rt[:, :, None]).astype(jnp.int32), axis=2)
    valid_t = rt < seg_end_t[:, -1:]
    exp_c = jnp.minimum(exp_of, n_exp - 1)
    src_y = (start_e[exp_c] + jnp.take_along_axis(seg_off_e, exp_c, axis=1)
             + rt - jnp.take_along_axis(seg_off_t, exp_c, axis=1))
    src_y = jnp.where(valid_t, src_y, 0).astype(jnp.int32).reshape(-1)

    ys = _experts(layer, xs, src_x, block_e, n_used.reshape(1), w_gu, b_gu, w_dn, b_dn, n_blocks)
    return _combine(y, mod, route, ys, src_y, n_rows, mod_row)


def kernel(x_prompt, x_sample, cache_k, cache_v, c, c_ctx, w_mod, b_mod, norm_mix, norm_ffn, gm_w_in, gm_norm_v, gm_w_s, gm_b_s, gm_w_out, at_w_qkv, at_q_norm, at_k_norm, at_w_o, pool_w_grp, pool_scale, router_w, router_b, w_gate_up, b_gate_up, w_down, b_down):
    n_ctx, ctx_len, d = x_prompt.shape
    n_lat, lat_len, _ = x_sample.shape
    depth = w_mod.shape[0]
    tp = n_ctx * ctx_len
    assert tp % lat_len == 0 and ctx_len % TM == 0 and lat_len % TM == 0 and ctx_len == TM
    assert 1 + n_lat <= SUBLANES

    y = jnp.concatenate([x_prompt.reshape(tp, d), x_sample.reshape(n_lat * lat_len, d)], axis=0)

    cvecs = jnp.zeros((SUBLANES, d), F32).at[0].set(c_ctx).at[1:1 + n_lat].set(c)
    mod_all = jnp.transpose(_ada_params(cvecs, w_mod, b_mod), (0, 2, 1, 3))

    def mod_row(i):
        return jnp.where(i * TM < tp, 0, 1 + (i * TM - tp) // lat_len)

    half = HEAD_DIM // 4
    inv = ROPE_THETA ** (-jnp.arange(half, dtype=F32) / half)
    pos = jnp.arange(lat_len)
    ang_r = (pos // GRID_W).astype(F32)[:, None] * inv[None, :]
    ang_c = (pos % GRID_W).astype(F32)[:, None] * inv[None, :]
    cos_t = jnp.concatenate([jnp.ones((TM, HEAD_DIM), F32),
                             jnp.concatenate([jnp.cos(ang_r)] * 2 + [jnp.cos(ang_c)] * 2, axis=1)], axis=0)
    sin_t = jnp.concatenate([jnp.zeros((TM, HEAD_DIM), F32),
                             jnp.concatenate([-jnp.sin(ang_r), jnp.sin(ang_r),
                                              -jnp.sin(ang_c), jnp.sin(ang_c)], axis=1)], axis=0)

    def rope_row(i):
        return jnp.where(i * TM < tp, 0, 1 + ((i * TM - tp) % lat_len) // TM)

    new_k, new_v = [], []
    for l in range(depth):
        kind, j = l % 3, l // 3
        mod = mod_all[l]
        if kind == 0:
            y = _gmlp(y, mod, norm_mix[l], gm_w_in[j], gm_norm_v[j], gm_w_s[j], gm_b_s[j], gm_w_out[j], mod_row)
        elif kind == 1:
            q, k, v = _qkv(y, mod, norm_mix[l], at_w_qkv[j], at_q_norm[j], at_k_norm[j],
                           cos_t, sin_t, mod_row, rope_row)
            new_k.append(k[:tp].reshape(n_ctx, ctx_len, N_KV_HEADS, HEAD_DIM))
            new_v.append(v[:tp].reshape(n_ctx, ctx_len, N_KV_HEADS, HEAD_DIM))
            past = cache_k.shape[2]
            ck = cache_k[:, j].reshape(n_lat, past, N_KV_HEADS * HEAD_DIM)
            cv = cache_v[:, j].reshape(n_lat, past, N_KV_HEADS * HEAD_DIM)
            y = _attention(y, mod, q, k, v, ck, cv, at_w_o[j], n_ctx, ctx_len, n_lat, lat_len)
        else:
            y = _pool(y, mod, norm_mix[l], pool_w_grp[j], pool_scale[j], n_ctx, ctx_len, 0, lambda b: 0)
            y = _pool(y, mod, norm_mix[l], pool_w_grp[j], pool_scale[j], n_lat, lat_len, tp, lambda b: 1 + b)
        y = _moe(l, y, mod, norm_ffn[l], router_w[l], router_b[l], w_gate_up, b_gate_up,
                 w_down, b_down, mod_row)

    return (y[:tp].reshape(n_ctx, ctx_len, d), y[tp:].reshape(n_lat, lat_len, d),
            jnp.stack(new_k, axis=1), jnp.stack(new_v, axis=1))
```

```python
import functools

import jax
import jax.numpy as jnp
from jax import lax
from jax.experimental import pallas as pl
from jax.experimental.pallas import tpu as pltpu

F32 = jnp.float32
BF16 = jnp.bfloat16
HIGHEST = lax.Precision.HIGHEST

LANES = 128
SUBLANES = 8
VMEM_LIMIT_BYTES = 56 * 1024 * 1024

EPS = 1e-6
GRID_W = 64
CHUNK = 128
HEAD_DIM = 128
N_KV_HEADS = 2
ROPE_THETA = 10000.0
POOL_WINDOWS = (2, 4, 8, 16)
TOP_K = 4
SWIGLU_ALPHA = 1.702
SWIGLU_LIMIT = 7.0

TM = 256
MOE_ROWS = 256
UNIT = SUBLANES


def _params(n_axes=1):
    return pltpu.CompilerParams(dimension_semantics=("arbitrary",) * n_axes,
                                vmem_limit_bytes=VMEM_LIMIT_BYTES)


def _rms(x, gain):
    return x * lax.rsqrt(jnp.mean(x * x, axis=-1, keepdims=True) + EPS) * gain


def _norm_mod(y, gain, shift, scale):
    return _rms(y, gain) * (1.0 + scale) + shift


def _gelu_tanh(x):
    return 0.5 * x * (1.0 + jnp.tanh(0.7978845608028654 * (x + 0.044715 * (x * x * x))))


def _dot(a, b):
    return jnp.dot(a, b, preferred_element_type=F32)


def _dot_nt(a, b):
    return lax.dot_general(a, b, (((1,), (1,)), ((), ())), preferred_element_type=F32)


def _ada_kernel(cv_ref, w_ref, b_ref, o_ref):
    cv = cv_ref[...]
    s = cv * jax.nn.sigmoid(cv)
    o_ref[0, 0] = jnp.dot(s, w_ref[0], precision=HIGHEST, preferred_element_type=F32) + b_ref[0, 0]


def _ada_params(cvecs, w_mod, b_mod):
    depth, d, _ = w_mod.shape
    rows = cvecs.shape[0]
    return pl.pallas_call(
        _ada_kernel,
        grid=(depth, 6),
        in_specs=[
            pl.BlockSpec((rows, d), lambda l, j: (0, 0)),
            pl.BlockSpec((1, d, d), lambda l, j: (l, 0, j)),
            pl.BlockSpec((1, 1, 1, d), lambda l, j: (l, j, 0, 0)),
        ],
        out_specs=pl.BlockSpec((1, 1, rows, d), lambda l, j: (l, j, 0, 0)),
        out_shape=jax.ShapeDtypeStruct((depth, 6, rows, d), F32),
        compiler_params=_params(2),
        name="ada_params",
    )(cvecs, w_mod, b_mod.reshape(depth, 6, 1, d))


def _gmlp_kernel(y_ref, mod_ref, gain_ref, win_ref, nv_ref, ws_ref, bexp_ref, wout_ref, o_ref):
    y = y_ref[...]
    inner = nv_ref.shape[-1]
    groups = ws_ref.shape[0]
    h = _norm_mod(y, gain_ref[...], mod_ref[0, 0:1, :], mod_ref[0, 1:2, :])
    z = _gelu_tanh(_dot(h.astype(BF16), win_ref[...]))
    u = z[:, :inner]
    v = _rms(z[:, inner:], nv_ref[...]).astype(BF16)
    bexp = bexp_ref[...]
    chunks = []
    for c in range(y.shape[0] // CHUNK):
        cols = []
        for g in range(groups):
            vg = v[c * CHUNK:(c + 1) * CHUNK, g * LANES:(g + 1) * LANES]
            cols.append(_dot(ws_ref[g], vg))
        chunks.append(jnp.concatenate(cols, axis=1) + bexp)
    mixed = jnp.concatenate(chunks, axis=0)
    o = _dot((u * mixed).astype(BF16), wout_ref[...])
    o_ref[...] = y + mod_ref[0, 2:3, :] * o


def _gmlp(y, mod, gain, w_in, norm_v, w_s, b_s, w_out, mod_row):
    t_all, d = y.shape
    inner = norm_v.shape[-1]
    groups = w_s.shape[0]
    bexp = jnp.repeat(b_s.T, inner // groups, axis=1)
    return pl.pallas_call(
        _gmlp_kernel,
        grid=(t_all // TM,),
        in_specs=[
            pl.BlockSpec((TM, d), lambda i: (i, 0)),
            pl.BlockSpec((1, 6, d), lambda i: (mod_row(i), 0, 0)),
            pl.BlockSpec((1, d), lambda i: (0, 0)),
            pl.BlockSpec((d, 2 * inner), lambda i: (0, 0)),
            pl.BlockSpec((1, inner), lambda i: (0, 0)),
            pl.BlockSpec((groups, CHUNK, CHUNK), lambda i: (0, 0, 0)),
            pl.BlockSpec((CHUNK, inner), lambda i: (0, 0)),
            pl.BlockSpec((inner, d), lambda i: (0, 0)),
        ],
        out_specs=pl.BlockSpec((TM, d), lambda i: (i, 0)),
        out_shape=jax.ShapeDtypeStruct((t_all, d), F32),
        compiler_params=_params(),
        name="gmlp",
    )(y, mod, gain.reshape(1, d), w_in.astype(BF16), norm_v.reshape(1, inner),
      w_s.astype(BF16), bexp, w_out.astype(BF16))


def _qkv_kernel(y_ref, mod_ref, gain_ref, w_ref, qn_ref, kn_ref, cos_ref, sin_ref,
                q_ref, k_ref, v_ref):
    y = y_ref[...]
    n_q = q_ref.shape[-1] // HEAD_DIM
    n_kv = k_ref.shape[-1] // HEAD_DIM
    h = _norm_mod(y, gain_ref[...], mod_ref[0, 0:1, :], mod_ref[0, 1:2, :])
    qkv = _dot(h.astype(BF16), w_ref[...])
    cos = cos_ref[...]
    sin = sin_ref[...]
    lane = lax.broadcasted_iota(jnp.int32, cos.shape, 1)
    low_half = (lane % (HEAD_DIM // 2)) < (HEAD_DIM // 4)

    def head(idx, norm):
        x = _rms(qkv[:, idx * HEAD_DIM:(idx + 1) * HEAD_DIM], norm)
        partner = jnp.where(low_half,
                            pltpu.roll(x, HEAD_DIM - HEAD_DIM // 4, 1),
                            pltpu.roll(x, HEAD_DIM // 4, 1))
        return x * cos + partner * sin

    qn = qn_ref[...]
    kn = kn_ref[...]
    q_ref[...] = jnp.concatenate([head(i, qn) for i in range(n_q)], axis=1).astype(q_ref.dtype)
    k_ref[...] = jnp.concatenate([head(n_q + i, kn) for i in range(n_kv)], axis=1)
    v_ref[...] = qkv[:, (n_q + n_kv) * HEAD_DIM:]


def _qkv(y, mod, gain, w_qkv, q_norm, k_norm, cos_t, sin_t, mod_row, rope_row):
    t_all, d = y.shape
    n_kv_cols = N_KV_HEADS * HEAD_DIM
    return pl.pallas_call(
        _qkv_kernel,
        grid=(t_all // TM,),
        in_specs=[
            pl.BlockSpec((TM, d), lambda i: (i, 0)),
            pl.BlockSpec((1, 6, d), lambda i: (mod_row(i), 0, 0)),
            pl.BlockSpec((1, d), lambda i: (0, 0)),
            pl.BlockSpec(w_qkv.shape, lambda i: (0, 0)),
            pl.BlockSpec((1, HEAD_DIM), lambda i: (0, 0)),
            pl.BlockSpec((1, HEAD_DIM), lambda i: (0, 0)),
            pl.BlockSpec((TM, HEAD_DIM), lambda i: (rope_row(i), 0)),
            pl.BlockSpec((TM, HEAD_DIM), lambda i: (rope_row(i), 0)),
        ],
        out_specs=[
            pl.BlockSpec((TM, d), lambda i: (i, 0)),
            pl.BlockSpec((TM, n_kv_cols), lambda i: (i, 0)),
            pl.BlockSpec((TM, n_kv_cols), lambda i: (i, 0)),
        ],
        out_shape=[
            jax.ShapeDtypeStruct((t_all, d), BF16),
            jax.ShapeDtypeStruct((t_all, n_kv_cols), F32),
            jax.ShapeDtypeStruct((t_all, n_kv_cols), F32),
        ],
        compiler_params=_params(),
        name="qkv_project",
    )(y, mod, gain.reshape(1, d), w_qkv.astype(BF16), q_norm.reshape(1, HEAD_DIM),
      k_norm.reshape(1, HEAD_DIM), cos_t, sin_t)


def _attend(q, key_sets):
    n_heads = q.shape[1] // HEAD_DIM
    rep = n_heads // N_KV_HEADS
    scale = HEAD_DIM ** -0.5
    outs = []
    for hd in range(n_heads):
        g = hd // rep
        qh = q[:, hd * HEAD_DIM:(hd + 1) * HEAD_DIM]
        cols = slice(g * HEAD_DIM, (g + 1) * HEAD_DIM)
        scores = [_dot_nt(qh, k[:, cols]) * scale for k, _ in key_sets]
        m = functools.reduce(jnp.maximum, [jnp.max(s, axis=-1, keepdims=True) for s in scores])
        probs = [jnp.exp(s - m) for s in scores]
        denom = functools.reduce(jnp.add, [jnp.sum(p, axis=-1, keepdims=True) for p in probs])
        o = functools.reduce(jnp.add, [_dot(p.astype(BF16), v[:, cols])
                                       for p, (_, v) in zip(probs, key_sets)])
        outs.append(o / denom)
    return jnp.concatenate(outs, axis=1)


def _attn_ctx_kernel(y_ref, mod_ref, q_ref, k_ref, v_ref, wo_ref, o_ref):
    att = _attend(q_ref[...], [(k_ref[...].astype(BF16), v_ref[...].astype(BF16))])
    o_ref[...] = y_ref[...] + mod_ref[0, 2:3, :] * _dot(att.astype(BF16), wo_ref[...])


def _attn_lat_kernel(y_ref, mod_ref, q_ref, k_ref, v_ref, ck_ref, cv_ref, wo_ref, o_ref):
    sets = [(ck_ref[0].astype(BF16), cv_ref[0].astype(BF16)),
            (k_ref[...].astype(BF16), v_ref[...].astype(BF16))]
    att = _attend(q_ref[...], sets)
    o_ref[...] = y_ref[...] + mod_ref[0, 2:3, :] * _dot(att.astype(BF16), wo_ref[...])


def _attention(y, mod, q, k, v, cache_k, cache_v, w_o, n_ctx, ctx_len, n_lat, lat_len):
    t_all, d = y.shape
    kvc = k.shape[1]
    tp = n_ctx * ctx_len
    wo = w_o.astype(BF16)
    y = pl.pallas_call(
        _attn_ctx_kernel,
        grid=(n_ctx,),
        in_specs=[
            pl.BlockSpec((ctx_len, d), lambda b: (b, 0)),
            pl.BlockSpec((1, 6, d), lambda b: (0, 0, 0)),
            pl.BlockSpec((ctx_len, d), lambda b: (b, 0)),
            pl.BlockSpec((ctx_len, kvc), lambda b: (b, 0)),
            pl.BlockSpec((ctx_len, kvc), lambda b: (b, 0)),
            pl.BlockSpec((d, d), lambda b: (0, 0)),
        ],
        out_specs=pl.BlockSpec((ctx_len, d), lambda b: (b, 0)),
        out_shape=jax.ShapeDtypeStruct((t_all, d), F32),
        input_output_aliases={0: 0},
        compiler_params=_params(),
        name="attn_context",
    )(y, mod, q, k, v, wo)
    tq = TM
    qb = lat_len // tq
    past = cache_k.shape[1]
    y = pl.pallas_call(
        _attn_lat_kernel,
        grid=(n_lat, qb),
        in_specs=[
            pl.BlockSpec((tq, d), lambda b, i: (tp // tq + b * qb + i, 0)),
            pl.BlockSpec((1, 6, d), lambda b, i: (1 + b, 0, 0)),
            pl.BlockSpec((tq, d), lambda b, i: (tp // tq + b * qb + i, 0)),
            pl.BlockSpec((lat_len, kvc), lambda b, i: (tp // lat_len + b, 0)),
            pl.BlockSpec((lat_len, kvc), lambda b, i: (tp // lat_len + b, 0)),
            pl.BlockSpec((1, past, kvc), lambda b, i: (b, 0, 0)),
            pl.BlockSpec((1, past, kvc), lambda b, i: (b, 0, 0)),
            pl.BlockSpec((d, d), lambda b, i: (0, 0)),
        ],
        out_specs=pl.BlockSpec((tq, d), lambda b, i: (tp // tq + b * qb + i, 0)),
        out_shape=jax.ShapeDtypeStruct((t_all, d), F32),
        input_output_aliases={0: 0},
        compiler_params=_params(2),
        name="attn_latent",
    )(y, mod, q, k, v, cache_k, cache_v, wo)
    return y


def _pool_kernel(y_ref, mod_ref, gain_ref, wg_ref, scale_ref, o_ref):
    y = y_ref[...]
    s_len, d = y.shape
    gd = d // len(POOL_WINDOWS)
    h = _norm_mod(y, gain_ref[...], mod_ref[0, 0:1, :], mod_ref[0, 1:2, :])
    t_idx = lax.broadcasted_iota(jnp.int32, (s_len, s_len), 0)
    s_idx = lax.broadcasted_iota(jnp.int32, (s_len, s_len), 1)
    t_col = lax.broadcasted_iota(jnp.int32, (s_len, 1), 0)
    parts = []
    for g, w in enumerate(POOL_WINDOWS):
        lo = t_idx - w // 2
        window = ((s_idx >= lo) & (s_idx < lo + w)).astype(BF16)
        count = (jnp.minimum(t_col - w // 2 + w, s_len) - jnp.maximum(t_col - w // 2, 0)).astype(F32)
        hg = h[:, g * gd:(g + 1) * gd]
        h_hi = hg.astype(BF16)
        h_lo = (hg - h_hi.astype(F32)).astype(BF16)
        mean = (_dot(window, h_hi) + _dot(window, h_lo)) / count
        parts.append(_dot((mean - hg).astype(BF16), wg_ref[g]))
    mixed = jnp.concatenate(parts, axis=1) * scale_ref[...]
    o_ref[...] = y + mod_ref[0, 2:3, :] * mixed


def _pool(y, mod, gain, w_grp, scale, n_seq, s_len, row_off, mod_off):
    t_all, d = y.shape
    blk_off = row_off // s_len
    return pl.pallas_call(
        _pool_kernel,
        grid=(n_seq,),
        in_specs=[
            pl.BlockSpec((s_len, d), lambda b: (blk_off + b, 0)),
            pl.BlockSpec((1, 6, d), lambda b: (mod_off(b), 0, 0)),
            pl.BlockSpec((1, d), lambda b: (0, 0)),
            pl.BlockSpec(w_grp.shape, lambda b: (0, 0, 0)),
            pl.BlockSpec((1, d), lambda b: (0, 0)),
        ],
        out_specs=pl.BlockSpec((s_len, d), lambda b: (blk_off + b, 0)),
        out_shape=jax.ShapeDtypeStruct((t_all, d), F32),
        input_output_aliases={0: 0},
        compiler_params=_params(),
        name="pool_mixer",
    )(y, mod, gain.reshape(1, d), w_grp.astype(BF16), scale.reshape(1, d))


def _sorted_rows(n_exp):
    worst = TM * TOP_K + n_exp * (UNIT - 1)
    return -(-worst // LANES) * LANES


def _route_kernel(y_ref, mod_ref, gain_ref, rw_ref, rb_ref, xs_ref, route_ref, cnt_ref):
    y = y_ref[...]
    tm = y.shape[0]
    n_rows = xs_ref.shape[0]
    h = _norm_mod(y, gain_ref[...], mod_ref[0, 3:4, :], mod_ref[0, 4:5, :])
    h_hi = h.astype(BF16)
    h_lo = (h - h_hi.astype(F32)).astype(BF16)
    hi_terms = _dot(h_hi, rw_ref[...])
    logits = hi_terms[:, :LANES] + hi_terms[:, LANES:] + _dot(h_lo, rw_ref[:, :LANES]) + rb_ref[...]
    lane = lax.broadcasted_iota(jnp.int32, logits.shape, 1).astype(F32)
    work = logits
    vals, ids = [], []
    for _ in range(TOP_K):
        m = jnp.max(work, axis=-1, keepdims=True)
        idx = jnp.min(jnp.where(work == m, lane, float(LANES)), axis=-1, keepdims=True)
        vals.append(m)
        ids.append(idx)
        work = jnp.where(lane == idx, -jnp.inf, work)
    exps = [jnp.exp(v - vals[0]) for v in vals]
    denom = functools.reduce(jnp.add, exps)
    onehot = functools.reduce(jnp.add, [(lane == idx).astype(F32) for idx in ids])

    counts = jnp.sum(onehot, axis=0, keepdims=True)
    seg = jnp.ceil(counts * (1.0 / UNIT)) * UNIT
    e_row = lax.broadcasted_iota(jnp.int32, (LANES, LANES), 0)
    e_col = lax.broadcasted_iota(jnp.int32, (LANES, LANES), 1)
    earlier = (e_row < e_col).astype(BF16)
    seg_off = _dot(jnp.broadcast_to(seg, (SUBLANES, LANES)).astype(BF16), earlier)[0:1, :]
    row = lax.broadcasted_iota(jnp.int32, (tm, tm), 0)
    col = lax.broadcasted_iota(jnp.int32, (tm, tm), 1)
    before = (col < row).astype(BF16)
    pos_mat = _dot(before, onehot.astype(BF16)) + seg_off

    route = jnp.zeros(logits.shape, F32)
    for k in range(TOP_K):
        pos_k = jnp.sum(jnp.where(lane == ids[k], pos_mat, 0.0), axis=-1, keepdims=True)
        route = jnp.where(lane == k, exps[k] / denom, route)
        route = jnp.where(lane == TOP_K + k, ids[k], route)
        route = jnp.where(lane == 2 * TOP_K + k, pos_k, route)
    route_ref[...] = route
    cnt_ref[...] = jnp.broadcast_to(counts, cnt_ref.shape)

    route_t = jnp.transpose(route)
    out_row = lax.broadcasted_iota(jnp.int32, (n_rows, tm), 0).astype(F32)
    perm = functools.reduce(jnp.add, [(out_row == route_t[2 * TOP_K + k:2 * TOP_K + k + 1, :]).astype(F32)
                                      for k in range(TOP_K)])
    xs_ref[...] = _dot(perm.astype(BF16), h_hi)


def _route(y, mod, gain, router_w, router_b, mod_row):
    t_all, d = y.shape
    n_exp = router_w.shape[1]
    n_tiles = t_all // TM
    n_rows = _sorted_rows(n_exp)
    rw = jnp.pad(router_w, ((0, 0), (0, LANES - n_exp)))
    rw_hi = rw.astype(BF16)
    rw = jnp.concatenate([rw_hi, (rw - rw_hi.astype(F32)).astype(BF16)], axis=1)
    rb = jnp.pad(router_b, (0, LANES - n_exp), constant_values=-1e30).reshape(1, LANES)
    return pl.pallas_call(
        _route_kernel,
        grid=(n_tiles,),
        in_specs=[
            pl.BlockSpec((TM, d), lambda i: (i, 0)),
            pl.BlockSpec((1, 6, d), lambda i: (mod_row(i), 0, 0)),
            pl.BlockSpec((1, d), lambda i: (0, 0)),
            pl.BlockSpec((d, 2 * LANES), lambda i: (0, 0)),
            pl.BlockSpec((1, LANES), lambda i: (0, 0)),
        ],
        out_specs=[
            pl.BlockSpec((n_rows, d), lambda i: (i, 0)),
            pl.BlockSpec((TM, LANES), lambda i: (i, 0)),
            pl.BlockSpec((SUBLANES, LANES), lambda i: (i, 0)),
        ],
        out_shape=[
            jax.ShapeDtypeStruct((n_tiles * n_rows, d), F32),
            jax.ShapeDtypeStruct((t_all, LANES), F32),
            jax.ShapeDtypeStruct((n_tiles * SUBLANES, LANES), F32),
        ],
        compiler_params=_params(),
        name="moe_route",
    )(y, mod, gain.reshape(1, d), rw, rb)


def _plan_kernel(n_tiles, n_exp, n_rows, c_ref, srcx_ref, srcy_ref, be_ref, nxt_ref, nu_ref):
    nb = srcx_ref.shape[0]
    seg = jnp.ceil(c_ref[...] * (1.0 / UNIT)) * UNIT
    seg_b = seg.astype(BF16)
    sq_r = lax.broadcasted_iota(jnp.int32, (LANES, LANES), 0)
    sq_c = lax.broadcasted_iota(jnp.int32, (LANES, LANES), 1)
    upto = (sq_r <= sq_c).astype(BF16)
    seg_end_t = _dot(seg_b, upto)
    seg_off_t = seg_end_t - seg
    seg_end_e = _dot((sq_c <= sq_r).astype(BF16), seg_b)
    seg_off_e = seg_end_e - seg
    rows_e = seg_end_e[n_tiles - 1:n_tiles, :]
    nblk = jnp.ceil(rows_e * (1.0 / MOE_ROWS))
    end_blk = _dot(jnp.broadcast_to(nblk, (SUBLANES, LANES)).astype(BF16), upto)[0:1, :]
    start_blk = end_blk - nblk
    n_used = jnp.max(end_blk, axis=-1, keepdims=True)

    lane1 = lax.broadcasted_iota(jnp.int32, (1, LANES), 1).astype(F32)
    lane = lax.broadcasted_iota(jnp.int32, (nb, LANES), 1).astype(F32)
    blk = lax.broadcasted_iota(jnp.int32, (nb, LANES), 0).astype(F32)
    is_exp = lane1 < n_exp
    has_rows = (nblk > 0) & is_exp
    be = jnp.minimum(jnp.sum(((end_blk <= blk) & is_exp).astype(F32), axis=-1, keepdims=True), n_exp - 1.0)
    last_e = jnp.max(jnp.where(has_rows, lane1, 0.0), axis=-1, keepdims=True)
    used = blk[:, 0:1] < n_used
    be = jnp.where(used, be, last_e)
    nxt = jnp.min(jnp.where((lane > be) & has_rows, lane, float(LANES)), axis=-1, keepdims=True)
    nxt = jnp.where(nxt == LANES, be, nxt)
    onehot = (lane == be).astype(F32)

    def column_of_block(x):
        hi = jnp.floor(x * (1.0 / 256.0))
        lo = x - hi * 256.0
        oh = onehot.astype(BF16)
        return _dot_nt(oh, hi.astype(BF16)) * 256.0 + _dot_nt(oh, lo.astype(BF16))

    ends_b = column_of_block(seg_end_e)
    shift_b = column_of_block(seg_off_t) - column_of_block(seg_off_e)
    start_b = jnp.sum(onehot * start_blk, axis=-1, keepdims=True)
    rows_b = jnp.sum(onehot * rows_e, axis=-1, keepdims=True)
    r = (blk - start_b) * MOE_ROWS + lane * UNIT
    tile_of = jnp.zeros((nb, LANES), F32)
    for t in range(n_tiles):
        tile_of = tile_of + (ends_b[:, t:t + 1] <= r).astype(F32)
    tile_c = jnp.minimum(tile_of, n_tiles - 1.0)
    src = jnp.zeros((nb, LANES), F32)
    for t in range(n_tiles):
        src = src + jnp.where(tile_c == t, shift_b[:, t:t + 1] + float(t * n_rows), 0.0)
    valid = (r < rows_b) & used & (lane < MOE_ROWS // UNIT)
    srcx_ref[...] = jnp.where(valid, src + r, 0.0).astype(jnp.int32)

    nt = srcy_ref.shape[0]
    t_end = seg_end_t[:nt]
    back = start_blk * MOE_ROWS + seg_off_e[:nt] - seg_off_t[:nt]
    rt = lax.broadcasted_iota(jnp.int32, srcy_ref.shape, 1).astype(F32) * UNIT
    exp_of = jnp.zeros(srcy_ref.shape, F32)
    for e in range(n_exp):
        exp_of = exp_of + (t_end[:, e:e + 1] <= rt).astype(F32)
    exp_c = jnp.minimum(exp_of, n_exp - 1.0)
    src = jnp.zeros(srcy_ref.shape, F32)
    for e in range(n_exp):
        src = src + jnp.where(exp_c == e, back[:, e:e + 1], 0.0)
    srcy_ref[...] = jnp.where(rt < t_end[:, n_exp - 1:n_exp], src + rt, 0.0).astype(jnp.int32)

    be_ref[...] = jnp.broadcast_to(be, be_ref.shape).astype(jnp.int32)
    nxt_ref[...] = jnp.broadcast_to(nxt, nxt_ref.shape).astype(jnp.int32)
    nu_ref[...] = jnp.broadcast_to(n_used, nu_ref.shape).astype(jnp.int32)


def _plan(counts, n_tiles, n_exp, n_rows, n_blocks):
    assert n_tiles <= LANES and n_exp <= LANES and n_tiles % SUBLANES == 0
    blk_units = MOE_ROWS // UNIT
    tile_units = n_rows // UNIT
    nb = -(-n_blocks // SUBLANES) * SUBLANES
    y_lanes = -(-tile_units // LANES) * LANES
    srcx, srcy, be, nxt, nu = pl.pallas_call(
        functools.partial(_plan_kernel, n_tiles, n_exp, n_rows),
        out_shape=[
            jax.ShapeDtypeStruct((nb, LANES), jnp.int32),
            jax.ShapeDtypeStruct((n_tiles, y_lanes), jnp.int32),
            jax.ShapeDtypeStruct((nb, LANES), jnp.int32),
            jax.ShapeDtypeStruct((nb, LANES), jnp.int32),
            jax.ShapeDtypeStruct((SUBLANES, LANES), jnp.int32),
        ],
        compiler_params=pltpu.CompilerParams(vmem_limit_bytes=VMEM_LIMIT_BYTES),
        name="moe_plan",
    )(counts)
    return (srcx[:n_blocks, :blk_units].reshape(-1), srcy[:, :tile_units].reshape(-1),
            be[:n_blocks, 0], nxt[:n_blocks, 0], nu[0, :1])


def _issue_units(src_ref, base, table_ref, dst, sem, n_queues):
    for u in range(dst.shape[0] // UNIT):
        start = pl.multiple_of(table_ref[base + u], UNIT)
        pltpu.make_async_copy(src_ref.at[pl.ds(start, UNIT)], dst.at[pl.ds(u * UNIT, UNIT)],
                              sem).start(priority=u % n_queues)


def _wait_units(buf, sem):
    pltpu.make_async_copy(buf, buf, sem).wait()


def _expert_kernel(layer, be_ref, nu_ref, src_ref, nxt_ref, xs_ref, wgu_hbm, bgu_ref, wdn_hbm, bdn_ref,
                   o_ref, wgu_st, wdn_st, wgu_bf, wdn_bf, x0, x1, gsem, wsem):
    i = pl.program_id(0)
    n_used = nu_ref[0]
    n_blocks = pl.num_programs(0)
    units = x0.shape[0] // UNIT
    d_exp = wdn_bf.shape[0]

    def weight_copies(e):
        return (pltpu.make_async_copy(wgu_hbm.at[layer, e], wgu_st, wsem.at[0]),
                pltpu.make_async_copy(wdn_hbm.at[layer, e], wdn_st, wsem.at[1]))

    def run_block(xcur, xoth, p):
        _wait_units(xcur, gsem.at[p])
        _issue_units(xs_ref, jnp.minimum(i + 1, n_blocks - 1) * units, src_ref, xoth, gsem.at[1 - p], 1)
        x = xcur[...].astype(BF16)
        gu = _dot(x, wgu_bf[...]) + bgu_ref[0, 0]
        gate = jnp.minimum(gu[:, :d_exp], SWIGLU_LIMIT)
        up = jnp.clip(gu[:, d_exp:], -SWIGLU_LIMIT, SWIGLU_LIMIT)
        hid = (up + 1.0) * gate * jax.nn.sigmoid(SWIGLU_ALPHA * gate)
        o_ref[...] = _dot(hid.astype(BF16), wdn_bf[...]) + bdn_ref[0, 0]

        @pl.when(i == n_used - 1)
        def _():
            _wait_units(xoth, gsem.at[1 - p])

    @pl.when(i < n_used)
    def _():
        @pl.when(i == 0)
        def _():
            for cp in weight_copies(be_ref[0]):
                cp.start(priority=1)
            _issue_units(xs_ref, 0, src_ref, x0, gsem.at[0], 1)

        new_expert = jnp.logical_or(i == 0, be_ref[i] != be_ref[jnp.maximum(i - 1, 0)])

        @pl.when(new_expert)
        def _():
            for cp in weight_copies(be_ref[i]):
                cp.wait()
            wgu_bf[...] = wgu_st[...].astype(BF16)
            wdn_bf[...] = wdn_st[...].astype(BF16)

            @pl.when(nxt_ref[i] != be_ref[i])
            def _():
                for cp in weight_copies(nxt_ref[i]):
                    cp.start(priority=1)

        @pl.when(i % 2 == 0)
        def _():
            run_block(x0, x1, 0)

        @pl.when(i % 2 == 1)
        def _():
            run_block(x1, x0, 1)


def _experts(layer, xs, src_units, block_e, next_e, n_used, w_gu, b_gu, w_dn, b_dn, n_blocks):
    _, d = xs.shape
    depth, n_exp, _, two_f = w_gu.shape
    d_exp = two_f // 2

    def out_block(i, be, nu, src, nxt):
        return (jnp.minimum(i, nu[0] - 1), 0)

    def bias_block(i, be, nu, src, nxt):
        return (layer, be[i], 0, 0)

    grid_spec = pltpu.PrefetchScalarGridSpec(
        num_scalar_prefetch=4,
        grid=(n_blocks,),
        in_specs=[
            pl.BlockSpec(memory_space=pl.ANY),
            pl.BlockSpec(memory_space=pl.ANY),
            pl.BlockSpec((1, 1, 1, two_f), bias_block),
            pl.BlockSpec(memory_space=pl.ANY),
            pl.BlockSpec((1, 1, 1, d), bias_block),
        ],
        out_specs=pl.BlockSpec((MOE_ROWS, d), out_block),
        scratch_shapes=[
            pltpu.VMEM((d, two_f), F32),
            pltpu.VMEM((d_exp, d), F32),
            pltpu.VMEM((d, two_f), BF16),
            pltpu.VMEM((d_exp, d), BF16),
            pltpu.VMEM((MOE_ROWS, d), F32),
            pltpu.VMEM((MOE_ROWS, d), F32),
            pltpu.SemaphoreType.DMA((2,)),
            pltpu.SemaphoreType.DMA((2,)),
        ],
    )
    return pl.pallas_call(
        functools.partial(_expert_kernel, layer),
        grid_spec=grid_spec,
        out_shape=jax.ShapeDtypeStruct((n_blocks * MOE_ROWS, d), F32),
        compiler_params=_params(),
        name="moe_experts",
    )(block_e, n_used, src_units, next_e, xs, w_gu, b_gu.reshape(depth, n_exp, 1, two_f),
      w_dn, b_dn.reshape(depth, n_exp, 1, d))


def _combine_kernel(src_ref, y_ref, mod_ref, route_ref, ys_ref, o_ref, t0, t1, gsem):
    i = pl.program_id(0)
    n_tiles = pl.num_programs(0)
    n_rows = t0.shape[0]
    units = n_rows // UNIT

    def run_tile(tcur, toth, p):
        _wait_units(tcur, gsem.at[p])
        _issue_units(ys_ref, jnp.minimum(i + 1, n_tiles - 1) * units, src_ref, toth, gsem.at[1 - p], 2)
        route = route_ref[...]
        tm = route.shape[0]
        col = lax.broadcasted_iota(jnp.int32, (tm, n_rows), 1).astype(F32)
        weights = functools.reduce(jnp.add, [
            jnp.where(col == route[:, 2 * TOP_K + k:2 * TOP_K + k + 1], route[:, k:k + 1], 0.0)
            for k in range(TOP_K)])
        acc = _dot(weights.astype(BF16), tcur[...].astype(BF16))
        o_ref[...] = y_ref[...] + mod_ref[0, 5:6, :] * acc

        @pl.when(i == n_tiles - 1)
        def _():
            _wait_units(toth, gsem.at[1 - p])

    @pl.when(i == 0)
    def _():
        _issue_units(ys_ref, 0, src_ref, t0, gsem.at[0], 2)

    @pl.when(i % 2 == 0)
    def _():
        run_tile(t0, t1, 0)

    @pl.when(i % 2 == 1)
    def _():
        run_tile(t1, t0, 1)


def _combine(y, mod, route, ys, src_units, n_rows, mod_row):
    t_all, d = y.shape
    grid_spec = pltpu.PrefetchScalarGridSpec(
        num_scalar_prefetch=1,
        grid=(t_all // TM,),
        in_specs=[
            pl.BlockSpec((TM, d), lambda i, src: (i, 0)),
            pl.BlockSpec((1, 6, d), lambda i, src: (mod_row(i), 0, 0)),
            pl.BlockSpec((TM, LANES), lambda i, src: (i, 0)),
            pl.BlockSpec(memory_space=pl.ANY),
        ],
        out_specs=pl.BlockSpec((TM, d), lambda i, src: (i, 0)),
        scratch_shapes=[
            pltpu.VMEM((n_rows, d), F32),
            pltpu.VMEM((n_rows, d), F32),
            pltpu.SemaphoreType.DMA((2,)),
        ],
    )
    return pl.pallas_call(
        _combine_kernel,
        grid_spec=grid_spec,
        out_shape=jax.ShapeDtypeStruct((t_all, d), F32),
        compiler_params=_params(),
        name="moe_combine",
    )(src_units, y, mod, route, ys)


def _moe(layer, y, mod, gain, router_w, router_b, w_gu, b_gu, w_dn, b_dn, mod_row):
    t_all, d = y.shape
    n_exp = router_w.shape[1]
    n_tiles = t_all // TM
    n_rows = _sorted_rows(n_exp)
    blk_units = MOE_ROWS // UNIT
    tile_units = n_rows // UNIT
    n_blocks = -(-(t_all * TOP_K + n_tiles * n_exp * (UNIT - 1)) // MOE_ROWS) + n_exp

    xs, route, cnt = _route(y, mod, gain, router_w, router_b, mod_row)

    counts = jnp.pad(cnt.reshape(n_tiles, SUBLANES, LANES)[:, 0, :], ((0, LANES - n_tiles), (0, 0)))
    src_x, src_y, block_e, next_e, n_used = _plan(counts, n_tiles, n_exp, n_rows, n_blocks)

    ys = _experts(layer, xs, src_x, block_e, next_e, n_used, w_gu, b_gu, w_dn, b_dn, n_blocks)
    return _combine(y, mod, route, ys, src_y, n_rows, mod_row)


def kernel(x_prompt, x_sample, cache_k, cache_v, c, c_ctx, w_mod, b_mod, norm_mix, norm_ffn, gm_w_in, gm_norm_v, gm_w_s, gm_b_s, gm_w_out, at_w_qkv, at_q_norm, at_k_norm, at_w_o, pool_w_grp, pool_scale, router_w, router_b, w_gate_up, b_gate_up, w_down, b_down):
    n_ctx, ctx_len, d = x_prompt.shape
    n_lat, lat_len, _ = x_sample.shape
    depth = w_mod.shape[0]
    tp = n_ctx * ctx_len
    assert tp % lat_len == 0 and ctx_len % TM == 0 and lat_len % TM == 0 and ctx_len == TM
    assert 1 + n_lat <= SUBLANES

    y = jnp.concatenate([x_prompt.reshape(tp, d), x_sample.reshape(n_lat * lat_len, d)], axis=0)

    cvecs = jnp.zeros((SUBLANES, d), F32).at[0].set(c_ctx).at[1:1 + n_lat].set(c)
    mod_all = jnp.transpose(_ada_params(cvecs, w_mod, b_mod), (0, 2, 1, 3))

    def mod_row(i):
        return jnp.where(i * TM < tp, 0, 1 + (i * TM - tp) // lat_len)

    half = HEAD_DIM // 4
    inv = ROPE_THETA ** (-jnp.arange(half, dtype=F32) / half)
    pos = jnp.arange(lat_len)
    ang_r = (pos // GRID_W).astype(F32)[:, None] * inv[None, :]
    ang_c = (pos % GRID_W).astype(F32)[:, None] * inv[None, :]
    cos_t = jnp.concatenate([jnp.ones((TM, HEAD_DIM), F32),
                             jnp.concatenate([jnp.cos(ang_r)] * 2 + [jnp.cos(ang_c)] * 2, axis=1)], axis=0)
    sin_t = jnp.concatenate([jnp.zeros((TM, HEAD_DIM), F32),
                             jnp.concatenate([-jnp.sin(ang_r), jnp.sin(ang_r),
                                              -jnp.sin(ang_c), jnp.sin(ang_c)], axis=1)], axis=0)

    def rope_row(i):
        return jnp.where(i * TM < tp, 0, 1 + ((i * TM - tp) % lat_len) // TM)

    new_k, new_v = [], []
    for l in range(depth):
        kind, j = l % 3, l // 3
        mod = mod_all[l]
        if kind == 0:
            y = _gmlp(y, mod, norm_mix[l], gm_w_in[j], gm_norm_v[j], gm_w_s[j], gm_b_s[j], gm_w_out[j], mod_row)
        elif kind == 1:
            q, k, v = _qkv(y, mod, norm_mix[l], at_w_qkv[j], at_q_norm[j], at_k_norm[j],
                           cos_t, sin_t, mod_row, rope_row)
            new_k.append(k[:tp].reshape(n_ctx, ctx_len, N_KV_HEADS, HEAD_DIM))
            new_v.append(v[:tp].reshape(n_ctx, ctx_len, N_KV_HEADS, HEAD_DIM))
            past = cache_k.shape[2]
            ck = cache_k[:, j].reshape(n_lat, past, N_KV_HEADS * HEAD_DIM)
            cv = cache_v[:, j].reshape(n_lat, past, N_KV_HEADS * HEAD_DIM)
            y = _attention(y, mod, q, k, v, ck, cv, at_w_o[j], n_ctx, ctx_len, n_lat, lat_len)
        else:
            y = _pool(y, mod, norm_mix[l], pool_w_grp[j], pool_scale[j], n_ctx, ctx_len, 0, lambda b: 0)
            y = _pool(y, mod, norm_mix[l], pool_w_grp[j], pool_scale[j], n_lat, lat_len, tp, lambda b: 1 + b)
        y = _moe(l, y, mod, norm_ffn[l], router_w[l], router_b[l], w_gate_up, b_gate_up,
                 w_down, b_down, mod_row)

    return (y[:tp].reshape(n_ctx, ctx_len, d), y[tp:].reshape(n_lat, lat_len, d),
            jnp.stack(new_k, axis=1), jnp.stack(new_v, axis=1))
```

```python
import functools

import jax
import jax.numpy as jnp
from jax import lax
from jax.experimental import pallas as pl
from jax.experimental.pallas import tpu as pltpu

F32 = jnp.float32
BF16 = jnp.bfloat16
HIGHEST = lax.Precision.HIGHEST

LANES = 128
SUBLANES = 8
VMEM_LIMIT_BYTES = 56 * 1024 * 1024

EPS = 1e-6
GRID_W = 64
CHUNK = 128
HEAD_DIM = 128
N_KV_HEADS = 2
ROPE_THETA = 10000.0
POOL_WINDOWS = (2, 4, 8, 16)
TOP_K = 4
SWIGLU_ALPHA = 1.702
SWIGLU_LIMIT = 7.0

TM = 256
MOE_ROWS = 512
UNIT = SUBLANES


def _params(n_axes=1):
    return pltpu.CompilerParams(dimension_semantics=("arbitrary",) * n_axes,
                                vmem_limit_bytes=VMEM_LIMIT_BYTES)


def _rms(x, gain):
    return x * lax.rsqrt(jnp.mean(x * x, axis=-1, keepdims=True) + EPS) * gain


def _norm_mod(y, gain, shift, scale):
    return _rms(y, gain) * (1.0 + scale) + shift


def _gelu_tanh(x):
    return 0.5 * x * (1.0 + jnp.tanh(0.7978845608028654 * (x + 0.044715 * (x * x * x))))


def _dot(a, b):
    return jnp.dot(a, b, preferred_element_type=F32)


def _dot_nt(a, b):
    return lax.dot_general(a, b, (((1,), (1,)), ((), ())), preferred_element_type=F32)


def _ada_kernel(cv_ref, w_ref, b_ref, o_ref):
    cv = cv_ref[...]
    s = cv * jax.nn.sigmoid(cv)
    o_ref[0, 0] = jnp.dot(s, w_ref[0], precision=HIGHEST, preferred_element_type=F32) + b_ref[0, 0]


def _ada_params(cvecs, w_mod, b_mod):
    depth, d, _ = w_mod.shape
    rows = cvecs.shape[0]
    return pl.pallas_call(
        _ada_kernel,
        grid=(depth, 6),
        in_specs=[
            pl.BlockSpec((rows, d), lambda l, j: (0, 0)),
            pl.BlockSpec((1, d, d), lambda l, j: (l, 0, j)),
            pl.BlockSpec((1, 1, 1, d), lambda l, j: (l, j, 0, 0)),
        ],
        out_specs=pl.BlockSpec((1, 1, rows, d), lambda l, j: (l, j, 0, 0)),
        out_shape=jax.ShapeDtypeStruct((depth, 6, rows, d), F32),
        compiler_params=_params(2),
        name="ada_params",
    )(cvecs, w_mod, b_mod.reshape(depth, 6, 1, d))


def _gmlp_kernel(y_ref, mod_ref, gain_ref, win_ref, nv_ref, ws_ref, bexp_ref, wout_ref, o_ref):
    y = y_ref[...]
    inner = nv_ref.shape[-1]
    groups = ws_ref.shape[0]
    h = _norm_mod(y, gain_ref[...], mod_ref[0, 0:1, :], mod_ref[0, 1:2, :])
    z = _gelu_tanh(_dot(h.astype(BF16), win_ref[...]))
    u = z[:, :inner]
    v = _rms(z[:, inner:], nv_ref[...]).astype(BF16)
    bexp = bexp_ref[...]
    chunks = []
    for c in range(y.shape[0] // CHUNK):
        cols = []
        for g in range(groups):
            vg = v[c * CHUNK:(c + 1) * CHUNK, g * LANES:(g + 1) * LANES]
            cols.append(_dot(ws_ref[g], vg))
        chunks.append(jnp.concatenate(cols, axis=1) + bexp)
    mixed = jnp.concatenate(chunks, axis=0)
    o = _dot((u * mixed).astype(BF16), wout_ref[...])
    o_ref[...] = y + mod_ref[0, 2:3, :] * o


def _gmlp(y, mod, gain, w_in, norm_v, w_s, b_s, w_out, mod_row):
    t_all, d = y.shape
    inner = norm_v.shape[-1]
    groups = w_s.shape[0]
    bexp = jnp.repeat(b_s.T, inner // groups, axis=1)
    return pl.pallas_call(
        _gmlp_kernel,
        grid=(t_all // TM,),
        in_specs=[
            pl.BlockSpec((TM, d), lambda i: (i, 0)),
            pl.BlockSpec((1, 6, d), lambda i: (mod_row(i), 0, 0)),
            pl.BlockSpec((1, d), lambda i: (0, 0)),
            pl.BlockSpec((d, 2 * inner), lambda i: (0, 0)),
            pl.BlockSpec((1, inner), lambda i: (0, 0)),
            pl.BlockSpec((groups, CHUNK, CHUNK), lambda i: (0, 0, 0)),
            pl.BlockSpec((CHUNK, inner), lambda i: (0, 0)),
            pl.BlockSpec((inner, d), lambda i: (0, 0)),
        ],
        out_specs=pl.BlockSpec((TM, d), lambda i: (i, 0)),
        out_shape=jax.ShapeDtypeStruct((t_all, d), F32),
        compiler_params=_params(),
        name="gmlp",
    )(y, mod, gain.reshape(1, d), w_in.astype(BF16), norm_v.reshape(1, inner),
      w_s.astype(BF16), bexp, w_out.astype(BF16))


def _qkv_kernel(y_ref, mod_ref, gain_ref, w_ref, qn_ref, kn_ref, cos_ref, sin_ref,
                q_ref, k_ref, v_ref):
    y = y_ref[...]
    n_q = q_ref.shape[-1] // HEAD_DIM
    n_kv = k_ref.shape[-1] // HEAD_DIM
    h = _norm_mod(y, gain_ref[...], mod_ref[0, 0:1, :], mod_ref[0, 1:2, :])
    qkv = _dot(h.astype(BF16), w_ref[...])
    cos = cos_ref[...]
    sin = sin_ref[...]
    lane = lax.broadcasted_iota(jnp.int32, cos.shape, 1)
    low_half = (lane % (HEAD_DIM // 2)) < (HEAD_DIM // 4)

    def head(idx, norm):
        x = _rms(qkv[:, idx * HEAD_DIM:(idx + 1) * HEAD_DIM], norm)
        partner = jnp.where(low_half,
                            pltpu.roll(x, HEAD_DIM - HEAD_DIM // 4, 1),
                            pltpu.roll(x, HEAD_DIM // 4, 1))
        return x * cos + partner * sin

    qn = qn_ref[...]
    kn = kn_ref[...]
    q_ref[...] = jnp.concatenate([head(i, qn) for i in range(n_q)], axis=1).astype(q_ref.dtype)
    k_ref[...] = jnp.concatenate([head(n_q + i, kn) for i in range(n_kv)], axis=1)
    v_ref[...] = qkv[:, (n_q + n_kv) * HEAD_DIM:]


def _qkv(y, mod, gain, w_qkv, q_norm, k_norm, cos_t, sin_t, mod_row, rope_row):
    t_all, d = y.shape
    n_kv_cols = N_KV_HEADS * HEAD_DIM
    return pl.pallas_call(
        _qkv_kernel,
        grid=(t_all // TM,),
        in_specs=[
            pl.BlockSpec((TM, d), lambda i: (i, 0)),
            pl.BlockSpec((1, 6, d), lambda i: (mod_row(i), 0, 0)),
            pl.BlockSpec((1, d), lambda i: (0, 0)),
            pl.BlockSpec(w_qkv.shape, lambda i: (0, 0)),
            pl.BlockSpec((1, HEAD_DIM), lambda i: (0, 0)),
            pl.BlockSpec((1, HEAD_DIM), lambda i: (0, 0)),
            pl.BlockSpec((TM, HEAD_DIM), lambda i: (rope_row(i), 0)),
            pl.BlockSpec((TM, HEAD_DIM), lambda i: (rope_row(i), 0)),
        ],
        out_specs=[
            pl.BlockSpec((TM, d), lambda i: (i, 0)),
            pl.BlockSpec((TM, n_kv_cols), lambda i: (i, 0)),
            pl.BlockSpec((TM, n_kv_cols), lambda i: (i, 0)),
        ],
        out_shape=[
            jax.ShapeDtypeStruct((t_all, d), BF16),
            jax.ShapeDtypeStruct((t_all, n_kv_cols), F32),
            jax.ShapeDtypeStruct((t_all, n_kv_cols), F32),
        ],
        compiler_params=_params(),
        name="qkv_project",
    )(y, mod, gain.reshape(1, d), w_qkv.astype(BF16), q_norm.reshape(1, HEAD_DIM),
      k_norm.reshape(1, HEAD_DIM), cos_t, sin_t)


def _attend(q, key_sets):
    n_heads = q.shape[1] // HEAD_DIM
    rep = n_heads // N_KV_HEADS
    scale = HEAD_DIM ** -0.5
    outs = []
    for hd in range(n_heads):
        g = hd // rep
        qh = q[:, hd * HEAD_DIM:(hd + 1) * HEAD_DIM]
        cols = slice(g * HEAD_DIM, (g + 1) * HEAD_DIM)
        scores = [_dot_nt(qh, k[:, cols]) * scale for k, _ in key_sets]
        m = functools.reduce(jnp.maximum, [jnp.max(s, axis=-1, keepdims=True) for s in scores])
        probs = [jnp.exp(s - m) for s in scores]
        denom = functools.reduce(jnp.add, [jnp.sum(p, axis=-1, keepdims=True) for p in probs])
        o = functools.reduce(jnp.add, [_dot(p.astype(BF16), v[:, cols])
                                       for p, (_, v) in zip(probs, key_sets)])
        outs.append(o / denom)
    return jnp.concatenate(outs, axis=1)


def _attn_ctx_kernel(y_ref, mod_ref, q_ref, k_ref, v_ref, wo_ref, o_ref):
    att = _attend(q_ref[...], [(k_ref[...].astype(BF16), v_ref[...].astype(BF16))])
    o_ref[...] = y_ref[...] + mod_ref[0, 2:3, :] * _dot(att.astype(BF16), wo_ref[...])


def _attn_lat_kernel(y_ref, mod_ref, q_ref, k_ref, v_ref, ck_ref, cv_ref, wo_ref, o_ref):
    sets = [(ck_ref[0].astype(BF16), cv_ref[0].astype(BF16)),
            (k_ref[...].astype(BF16), v_ref[...].astype(BF16))]
    att = _attend(q_ref[...], sets)
    o_ref[...] = y_ref[...] + mod_ref[0, 2:3, :] * _dot(att.astype(BF16), wo_ref[...])


def _attention(y, mod, q, k, v, cache_k, cache_v, w_o, n_ctx, ctx_len, n_lat, lat_len):
    t_all, d = y.shape
    kvc = k.shape[1]
    tp = n_ctx * ctx_len
    wo = w_o.astype(BF16)
    y = pl.pallas_call(
        _attn_ctx_kernel,
        grid=(n_ctx,),
        in_specs=[
            pl.BlockSpec((ctx_len, d), lambda b: (b, 0)),
            pl.BlockSpec((1, 6, d), lambda b: (0, 0, 0)),
            pl.BlockSpec((ctx_len, d), lambda b: (b, 0)),
            pl.BlockSpec((ctx_len, kvc), lambda b: (b, 0)),
            pl.BlockSpec((ctx_len, kvc), lambda b: (b, 0)),
            pl.BlockSpec((d, d), lambda b: (0, 0)),
        ],
        out_specs=pl.BlockSpec((ctx_len, d), lambda b: (b, 0)),
        out_shape=jax.ShapeDtypeStruct((t_all, d), F32),
        input_output_aliases={0: 0},
        compiler_params=_params(),
        name="attn_context",
    )(y, mod, q, k, v, wo)
    tq = TM
    qb = lat_len // tq
    past = cache_k.shape[1]
    y = pl.pallas_call(
        _attn_lat_kernel,
        grid=(n_lat, qb),
        in_specs=[
            pl.BlockSpec((tq, d), lambda b, i: (tp // tq + b * qb + i, 0)),
            pl.BlockSpec((1, 6, d), lambda b, i: (1 + b, 0, 0)),
            pl.BlockSpec((tq, d), lambda b, i: (tp // tq + b * qb + i, 0)),
            pl.BlockSpec((lat_len, kvc), lambda b, i: (tp // lat_len + b, 0)),
            pl.BlockSpec((lat_len, kvc), lambda b, i: (tp // lat_len + b, 0)),
            pl.BlockSpec((1, past, kvc), lambda b, i: (b, 0, 0)),
            pl.BlockSpec((1, past, kvc), lambda b, i: (b, 0, 0)),
            pl.BlockSpec((d, d), lambda b, i: (0, 0)),
        ],
        out_specs=pl.BlockSpec((tq, d), lambda b, i: (tp // tq + b * qb + i, 0)),
        out_shape=jax.ShapeDtypeStruct((t_all, d), F32),
        input_output_aliases={0: 0},
        compiler_params=_params(2),
        name="attn_latent",
    )(y, mod, q, k, v, cache_k, cache_v, wo)
    return y


def _pool_kernel(y_ref, mod_ref, gain_ref, wg_ref, scale_ref, o_ref):
    y = y_ref[...]
    s_len, d = y.shape
    gd = d // len(POOL_WINDOWS)
    h = _norm_mod(y, gain_ref[...], mod_ref[0, 0:1, :], mod_ref[0, 1:2, :])
    t_idx = lax.broadcasted_iota(jnp.int32, (s_len, s_len), 0)
    s_idx = lax.broadcasted_iota(jnp.int32, (s_len, s_len), 1)
    t_col = lax.broadcasted_iota(jnp.int32, (s_len, 1), 0)
    parts = []
    for g, w in enumerate(POOL_WINDOWS):
        lo = t_idx - w // 2
        window = ((s_idx >= lo) & (s_idx < lo + w)).astype(BF16)
        count = (jnp.minimum(t_col - w // 2 + w, s_len) - jnp.maximum(t_col - w // 2, 0)).astype(F32)
        hg = h[:, g * gd:(g + 1) * gd]
        h_hi = hg.astype(BF16)
        h_lo = (hg - h_hi.astype(F32)).astype(BF16)
        mean = (_dot(window, h_hi) + _dot(window, h_lo)) / count
        parts.append(_dot((mean - hg).astype(BF16), wg_ref[g]))
    mixed = jnp.concatenate(parts, axis=1) * scale_ref[...]
    o_ref[...] = y + mod_ref[0, 2:3, :] * mixed


def _pool(y, mod, gain, w_grp, scale, n_seq, s_len, row_off, mod_off):
    t_all, d = y.shape
    blk_off = row_off // s_len
    return pl.pallas_call(
        _pool_kernel,
        grid=(n_seq,),
        in_specs=[
            pl.BlockSpec((s_len, d), lambda b: (blk_off + b, 0)),
            pl.BlockSpec((1, 6, d), lambda b: (mod_off(b), 0, 0)),
            pl.BlockSpec((1, d), lambda b: (0, 0)),
            pl.BlockSpec(w_grp.shape, lambda b: (0, 0, 0)),
            pl.BlockSpec((1, d), lambda b: (0, 0)),
        ],
        out_specs=pl.BlockSpec((s_len, d), lambda b: (blk_off + b, 0)),
        out_shape=jax.ShapeDtypeStruct((t_all, d), F32),
        input_output_aliases={0: 0},
        compiler_params=_params(),
        name="pool_mixer",
    )(y, mod, gain.reshape(1, d), w_grp.astype(BF16), scale.reshape(1, d))


def _sorted_rows(n_exp):
    worst = TM * TOP_K + n_exp * (UNIT - 1) + UNIT
    return -(-worst // LANES) * LANES


def _route_kernel(y_ref, mod_ref, gain_ref, rw_ref, rb_ref, xs_ref, route_ref, cnt_ref):
    y = y_ref[...]
    tm = y.shape[0]
    n_rows = xs_ref.shape[0]
    h = _norm_mod(y, gain_ref[...], mod_ref[0, 3:4, :], mod_ref[0, 4:5, :])
    h_hi = h.astype(BF16)
    h_lo = (h - h_hi.astype(F32)).astype(BF16)
    hi_terms = _dot(h_hi, rw_ref[...])
    logits = hi_terms[:, :LANES] + hi_terms[:, LANES:] + _dot(h_lo, rw_ref[:, :LANES]) + rb_ref[...]
    lane = lax.broadcasted_iota(jnp.int32, logits.shape, 1).astype(F32)
    work = logits
    vals, ids = [], []
    for _ in range(TOP_K):
        m = jnp.max(work, axis=-1, keepdims=True)
        idx = jnp.min(jnp.where(work == m, lane, float(LANES)), axis=-1, keepdims=True)
        vals.append(m)
        ids.append(idx)
        work = jnp.where(lane == idx, -jnp.inf, work)
    exps = [jnp.exp(v - vals[0]) for v in vals]
    denom = functools.reduce(jnp.add, exps)
    onehot = functools.reduce(jnp.add, [(lane == idx).astype(F32) for idx in ids])

    counts = jnp.sum(onehot, axis=0, keepdims=True)
    seg = jnp.ceil(counts * (1.0 / UNIT)) * UNIT
    e_row = lax.broadcasted_iota(jnp.int32, (LANES, LANES), 0)
    e_col = lax.broadcasted_iota(jnp.int32, (LANES, LANES), 1)
    earlier = (e_row < e_col).astype(BF16)
    seg_off = _dot(jnp.broadcast_to(seg, (SUBLANES, LANES)).astype(BF16), earlier)[0:1, :]
    row = lax.broadcasted_iota(jnp.int32, (tm, tm), 0)
    col = lax.broadcasted_iota(jnp.int32, (tm, tm), 1)
    before = (col < row).astype(BF16)
    pos_mat = _dot(before, onehot.astype(BF16)) + seg_off

    route = jnp.zeros(logits.shape, F32)
    for k in range(TOP_K):
        pos_k = jnp.sum(jnp.where(lane == ids[k], pos_mat, 0.0), axis=-1, keepdims=True)
        route = jnp.where(lane == k, exps[k] / denom, route)
        route = jnp.where(lane == TOP_K + k, ids[k], route)
        route = jnp.where(lane == 2 * TOP_K + k, pos_k, route)
    route_ref[...] = route
    cnt_ref[...] = jnp.broadcast_to(counts, cnt_ref.shape)

    route_t = jnp.transpose(route)
    out_row = lax.broadcasted_iota(jnp.int32, (n_rows, tm), 0).astype(F32)
    perm = functools.reduce(jnp.add, [(out_row == route_t[2 * TOP_K + k:2 * TOP_K + k + 1, :]).astype(F32)
                                      for k in range(TOP_K)])
    xs_ref[...] = _dot(perm.astype(BF16), h_hi)


def _route(y, mod, gain, router_w, router_b, mod_row):
    t_all, d = y.shape
    n_exp = router_w.shape[1]
    n_tiles = t_all // TM
    n_rows = _sorted_rows(n_exp)
    rw = jnp.pad(router_w, ((0, 0), (0, LANES - n_exp)))
    rw_hi = rw.astype(BF16)
    rw = jnp.concatenate([rw_hi, (rw - rw_hi.astype(F32)).astype(BF16)], axis=1)
    rb = jnp.pad(router_b, (0, LANES - n_exp), constant_values=-1e30).reshape(1, LANES)
    return pl.pallas_call(
        _route_kernel,
        grid=(n_tiles,),
        in_specs=[
            pl.BlockSpec((TM, d), lambda i: (i, 0)),
            pl.BlockSpec((1, 6, d), lambda i: (mod_row(i), 0, 0)),
            pl.BlockSpec((1, d), lambda i: (0, 0)),
            pl.BlockSpec((d, 2 * LANES), lambda i: (0, 0)),
            pl.BlockSpec((1, LANES), lambda i: (0, 0)),
        ],
        out_specs=[
            pl.BlockSpec((n_rows, d), lambda i: (i, 0)),
            pl.BlockSpec((TM, LANES), lambda i: (i, 0)),
            pl.BlockSpec((SUBLANES, LANES), lambda i: (i, 0)),
        ],
        out_shape=[
            jax.ShapeDtypeStruct((n_tiles * n_rows + 2 * MOE_ROWS, d), F32),
            jax.ShapeDtypeStruct((t_all, LANES), F32),
            jax.ShapeDtypeStruct((n_tiles * SUBLANES, LANES), F32),
        ],
        compiler_params=_params(),
        name="moe_route",
    )(y, mod, gain.reshape(1, d), rw, rb)


def _plan_kernel(n_tiles, n_exp, n_rows, c_ref, srcx_ref, dstx_ref, be_ref, nxt_ref, nu_ref):
    nb = srcx_ref.shape[0]
    seg = jnp.ceil(c_ref[...] * (1.0 / UNIT)) * UNIT
    seg_b = seg.astype(BF16)
    sq_r = lax.broadcasted_iota(jnp.int32, (LANES, LANES), 0)
    sq_c = lax.broadcasted_iota(jnp.int32, (LANES, LANES), 1)
    upto = (sq_r <= sq_c).astype(BF16)
    seg_end_t = _dot(seg_b, upto)
    seg_off_t = seg_end_t - seg
    seg_end_e = _dot((sq_c <= sq_r).astype(BF16), seg_b)
    seg_off_e = seg_end_e - seg
    rows_e = seg_end_e[n_tiles - 1:n_tiles, :]
    nblk = jnp.ceil(rows_e * (1.0 / MOE_ROWS))
    end_blk = _dot(jnp.broadcast_to(nblk, (SUBLANES, LANES)).astype(BF16), upto)[0:1, :]
    start_blk = end_blk - nblk
    n_used = jnp.max(end_blk, axis=-1, keepdims=True)

    lane1 = lax.broadcasted_iota(jnp.int32, (1, LANES), 1).astype(F32)
    lane = lax.broadcasted_iota(jnp.int32, (nb, LANES), 1).astype(F32)
    blk = lax.broadcasted_iota(jnp.int32, (nb, LANES), 0).astype(F32)
    is_exp = lane1 < n_exp
    has_rows = (nblk > 0) & is_exp
    be = jnp.minimum(jnp.sum(((end_blk <= blk) & is_exp).astype(F32), axis=-1, keepdims=True), n_exp - 1.0)
    last_e = jnp.max(jnp.where(has_rows, lane1, 0.0), axis=-1, keepdims=True)
    used = blk[:, 0:1] < n_used
    be = jnp.where(used, be, last_e)
    nxt = jnp.min(jnp.where((lane > be) & has_rows, lane, float(LANES)), axis=-1, keepdims=True)
    nxt = jnp.where(nxt == LANES, be, nxt)
    onehot = (lane == be).astype(F32)

    def column_of_block(x):
        hi = jnp.floor(x * (1.0 / 256.0))
        lo = x - hi * 256.0
        oh = onehot.astype(BF16)
        return _dot_nt(oh, hi.astype(BF16)) * 256.0 + _dot_nt(oh, lo.astype(BF16))

    ends_b = column_of_block(seg_end_e)
    shift_b = column_of_block(seg_off_t) - column_of_block(seg_off_e)
    start_b = jnp.sum(onehot * start_blk, axis=-1, keepdims=True)
    rows_b = jnp.sum(onehot * rows_e, axis=-1, keepdims=True)
    r = (blk - start_b) * MOE_ROWS + lane * UNIT
    tile_of = jnp.zeros((nb, LANES), F32)
    for t in range(n_tiles):
        tile_of = tile_of + (ends_b[:, t:t + 1] <= r).astype(F32)
    tile_c = jnp.minimum(tile_of, n_tiles - 1.0)
    src = jnp.zeros((nb, LANES), F32)
    for t in range(n_tiles):
        src = src + jnp.where(tile_c == t, shift_b[:, t:t + 1] + float(t * n_rows), 0.0)
    valid = (r < rows_b) & used & (lane < MOE_ROWS // UNIT)
    srcx_ref[...] = jnp.where(valid, src + r, float(n_rows - UNIT)).astype(jnp.int32)
    spare = float(n_tiles * n_rows) + (blk - 2.0 * jnp.floor(blk * 0.5)) * MOE_ROWS + lane * UNIT
    dstx_ref[...] = jnp.where(valid, src + r, spare).astype(jnp.int32)

    be_ref[...] = jnp.broadcast_to(be, be_ref.shape).astype(jnp.int32)
    nxt_ref[...] = jnp.broadcast_to(nxt, nxt_ref.shape).astype(jnp.int32)
    nu_ref[...] = jnp.broadcast_to(n_used, nu_ref.shape).astype(jnp.int32)


def _plan(counts, n_tiles, n_exp, n_rows, n_blocks):
    assert n_tiles <= LANES and n_exp <= LANES and n_tiles % SUBLANES == 0
    blk_units = MOE_ROWS // UNIT
    assert blk_units <= LANES
    nb = -(-n_blocks // SUBLANES) * SUBLANES
    srcx, dstx, be, nxt, nu = pl.pallas_call(
        functools.partial(_plan_kernel, n_tiles, n_exp, n_rows),
        out_shape=[
            jax.ShapeDtypeStruct((nb, LANES), jnp.int32),
            jax.ShapeDtypeStruct((nb, LANES), jnp.int32),
            jax.ShapeDtypeStruct((nb, LANES), jnp.int32),
            jax.ShapeDtypeStruct((nb, LANES), jnp.int32),
            jax.ShapeDtypeStruct((SUBLANES, LANES), jnp.int32),
        ],
        compiler_params=pltpu.CompilerParams(vmem_limit_bytes=VMEM_LIMIT_BYTES),
        name="moe_plan",
    )(counts)
    before = n_tiles * n_rows + MOE_ROWS + UNIT * jnp.arange(blk_units, dtype=jnp.int32)
    dst = jnp.concatenate([before, dstx[:n_blocks, :blk_units].reshape(-1)])
    return (srcx[:n_blocks, :blk_units].reshape(-1), dst, be[:n_blocks, 0], nxt[:n_blocks, 0], nu[0, :1])


def _gather_units(src_ref, base, table_ref, dst, sem):
    for u in range(dst.shape[0] // UNIT):
        start = pl.multiple_of(table_ref[base + u], UNIT)
        pltpu.make_async_copy(src_ref.at[pl.ds(start, UNIT)], dst.at[pl.ds(u * UNIT, UNIT)], sem).start()


def _scatter_units(src, base, table_ref, dst_ref, sem):
    for u in range(src.shape[0] // UNIT):
        start = pl.multiple_of(table_ref[base + u], UNIT)
        pltpu.make_async_copy(src.at[pl.ds(u * UNIT, UNIT)], dst_ref.at[pl.ds(start, UNIT)], sem).start()


def _wait_units(buf, sem):
    pltpu.make_async_copy(buf, buf, sem).wait()


def _expert_kernel(layer, be_ref, nu_ref, src_ref, dst_ref, nxt_ref, xs_ref, wgu_hbm, bgu_ref, wdn_hbm,
                   bdn_ref, ys_ref, wgu_st, wdn_st, wgu_bf, wdn_bf, x0, x1, y0, y1, gsem, ssem, wsem):
    i = pl.program_id(0)
    n_used = nu_ref[0]
    n_blocks = pl.num_programs(0)
    units = x0.shape[0] // UNIT
    d_exp = wdn_bf.shape[0]

    def weight_copies(e):
        return (pltpu.make_async_copy(wgu_hbm.at[layer, e], wgu_st, wsem.at[0]),
                pltpu.make_async_copy(wdn_hbm.at[layer, e], wdn_st, wsem.at[1]))

    def run_block(xcur, xoth, ycur, yoth, p):
        _wait_units(xcur, gsem.at[p])

        @pl.when(i >= 1)
        def _():
            _wait_units(ycur, ssem.at[p])

        _gather_units(xs_ref, jnp.minimum(i + 1, n_blocks - 1) * units, src_ref, xoth, gsem.at[1 - p])
        _scatter_units(yoth, i * units, dst_ref, ys_ref, ssem.at[1 - p])
        x = xcur[...].astype(BF16)
        gu = _dot(x, wgu_bf[...]) + bgu_ref[0, 0]
        gate = jnp.minimum(gu[:, :d_exp], SWIGLU_LIMIT)
        up = jnp.clip(gu[:, d_exp:], -SWIGLU_LIMIT, SWIGLU_LIMIT)
        hid = (up + 1.0) * gate * jax.nn.sigmoid(SWIGLU_ALPHA * gate)
        ycur[...] = _dot(hid.astype(BF16), wdn_bf[...]) + bdn_ref[0, 0]

        @pl.when(i == n_used - 1)
        def _():
            _wait_units(xoth, gsem.at[1 - p])
            _scatter_units(ycur, (i + 1) * units, dst_ref, ys_ref, ssem.at[p])
            _wait_units(yoth, ssem.at[1 - p])
            _wait_units(ycur, ssem.at[p])

    @pl.when(i < n_used)
    def _():
        @pl.when(i == 0)
        def _():
            for cp in weight_copies(be_ref[0]):
                cp.start(priority=1)
            _gather_units(xs_ref, 0, src_ref, x0, gsem.at[0])
            y1[...] = jnp.zeros_like(y1)

        new_expert = jnp.logical_or(i == 0, be_ref[i] != be_ref[jnp.maximum(i - 1, 0)])

        @pl.when(new_expert)
        def _():
            for cp in weight_copies(be_ref[i]):
                cp.wait()
            wgu_bf[...] = wgu_st[...].astype(BF16)
            wdn_bf[...] = wdn_st[...].astype(BF16)

            @pl.when(nxt_ref[i] != be_ref[i])
            def _():
                for cp in weight_copies(nxt_ref[i]):
                    cp.start(priority=1)

        @pl.when(i % 2 == 0)
        def _():
            run_block(x0, x1, y0, y1, 0)

        @pl.when(i % 2 == 1)
        def _():
            run_block(x1, x0, y1, y0, 1)


def _experts(layer, xs, src_units, dst_units, block_e, next_e, n_used, w_gu, b_gu, w_dn, b_dn, n_blocks):
    d = w_dn.shape[-1]
    depth, n_exp, _, two_f = w_gu.shape
    d_exp = two_f // 2
    n_prefetch = 5

    def bias_block(i, be, nu, src, dst, nxt):
        return (layer, be[i], 0, 0)

    grid_spec = pltpu.PrefetchScalarGridSpec(
        num_scalar_prefetch=n_prefetch,
        grid=(n_blocks,),
        in_specs=[
            pl.BlockSpec(memory_space=pl.ANY),
            pl.BlockSpec(memory_space=pl.ANY),
            pl.BlockSpec((1, 1, 1, two_f), bias_block),
            pl.BlockSpec(memory_space=pl.ANY),
            pl.BlockSpec((1, 1, 1, d), bias_block),
        ],
        out_specs=pl.BlockSpec(memory_space=pl.ANY),
        scratch_shapes=[
            pltpu.VMEM((d, two_f), F32),
            pltpu.VMEM((d_exp, d), F32),
            pltpu.VMEM((d, two_f), BF16),
            pltpu.VMEM((d_exp, d), BF16),
            pltpu.VMEM((MOE_ROWS, d), F32),
            pltpu.VMEM((MOE_ROWS, d), F32),
            pltpu.VMEM((MOE_ROWS, d), F32),
            pltpu.VMEM((MOE_ROWS, d), F32),
            pltpu.SemaphoreType.DMA((2,)),
            pltpu.SemaphoreType.DMA((2,)),
            pltpu.SemaphoreType.DMA((2,)),
        ],
    )
    return pl.pallas_call(
        functools.partial(_expert_kernel, layer),
        grid_spec=grid_spec,
        out_shape=jax.ShapeDtypeStruct(xs.shape, xs.dtype),
        input_output_aliases={n_prefetch: 0},
        compiler_params=_params(),
        name="moe_experts",
    )(block_e, n_used, src_units, dst_units, next_e, xs, w_gu, b_gu.reshape(depth, n_exp, 1, two_f),
      w_dn, b_dn.reshape(depth, n_exp, 1, d))


def _combine_kernel(y_ref, mod_ref, route_ref, ys_ref, o_ref):
    route = route_ref[...]
    tm = route.shape[0]
    n_rows = ys_ref.shape[0]
    col = lax.broadcasted_iota(jnp.int32, (tm, n_rows), 1).astype(F32)
    weights = functools.reduce(jnp.add, [
        jnp.where(col == route[:, 2 * TOP_K + k:2 * TOP_K + k + 1], route[:, k:k + 1], 0.0)
        for k in range(TOP_K)])
    acc = _dot(weights.astype(BF16), ys_ref[...].astype(BF16))
    o_ref[...] = y_ref[...] + mod_ref[0, 5:6, :] * acc


def _combine(y, mod, route, ys, n_rows, mod_row):
    t_all, d = y.shape
    return pl.pallas_call(
        _combine_kernel,
        grid=(t_all // TM,),
        in_specs=[
            pl.BlockSpec((TM, d), lambda i: (i, 0)),
            pl.BlockSpec((1, 6, d), lambda i: (mod_row(i), 0, 0)),
            pl.BlockSpec((TM, LANES), lambda i: (i, 0)),
            pl.BlockSpec((n_rows, d), lambda i: (i, 0)),
        ],
        out_specs=pl.BlockSpec((TM, d), lambda i: (i, 0)),
        out_shape=jax.ShapeDtypeStruct((t_all, d), F32),
        compiler_params=_params(),
        name="moe_combine",
    )(y, mod, route, ys)


def _moe(layer, y, mod, gain, router_w, router_b, w_gu, b_gu, w_dn, b_dn, mod_row):
    t_all, d = y.shape
    n_exp = router_w.shape[1]
    n_tiles = t_all // TM
    n_rows = _sorted_rows(n_exp)
    n_blocks = -(-(t_all * TOP_K + n_tiles * n_exp * (UNIT - 1)) // MOE_ROWS) + n_exp

    xs, route, cnt = _route(y, mod, gain, router_w, router_b, mod_row)
    counts = jnp.pad(cnt.reshape(n_tiles, SUBLANES, LANES)[:, 0, :], ((0, LANES - n_tiles), (0, 0)))
    src_x, dst_x, block_e, next_e, n_used = _plan(counts, n_tiles, n_exp, n_rows, n_blocks)
    ys = _experts(layer, xs, src_x, dst_x, block_e, next_e, n_used, w_gu, b_gu, w_dn, b_dn, n_blocks)
    return _combine(y, mod, route, ys, n_rows, mod_row)


def kernel(x_prompt, x_sample, cache_k, cache_v, c, c_ctx, w_mod, b_mod, norm_mix, norm_ffn, gm_w_in, gm_norm_v, gm_w_s, gm_b_s, gm_w_out, at_w_qkv, at_q_norm, at_k_norm, at_w_o, pool_w_grp, pool_scale, router_w, router_b, w_gate_up, b_gate_up, w_down, b_down):
    n_ctx, ctx_len, d = x_prompt.shape
    n_lat, lat_len, _ = x_sample.shape
    depth = w_mod.shape[0]
    tp = n_ctx * ctx_len
    assert tp % lat_len == 0 and ctx_len % TM == 0 and lat_len % TM == 0 and ctx_len == TM
    assert 1 + n_lat <= SUBLANES

    y = jnp.concatenate([x_prompt.reshape(tp, d), x_sample.reshape(n_lat * lat_len, d)], axis=0)

    cvecs = jnp.zeros((SUBLANES, d), F32).at[0].set(c_ctx).at[1:1 + n_lat].set(c)
    mod_all = jnp.transpose(_ada_params(cvecs, w_mod, b_mod), (0, 2, 1, 3))

    def mod_row(i):
        return jnp.where(i * TM < tp, 0, 1 + (i * TM - tp) // lat_len)

    half = HEAD_DIM // 4
    inv = ROPE_THETA ** (-jnp.arange(half, dtype=F32) / half)
    pos = jnp.arange(lat_len)
    ang_r = (pos // GRID_W).astype(F32)[:, None] * inv[None, :]
    ang_c = (pos % GRID_W).astype(F32)[:, None] * inv[None, :]
    cos_t = jnp.concatenate([jnp.ones((TM, HEAD_DIM), F32),
                             jnp.concatenate([jnp.cos(ang_r)] * 2 + [jnp.cos(ang_c)] * 2, axis=1)], axis=0)
    sin_t = jnp.concatenate([jnp.zeros((TM, HEAD_DIM), F32),
                             jnp.concatenate([-jnp.sin(ang_r), jnp.sin(ang_r),
                                              -jnp.sin(ang_c), jnp.sin(ang_c)], axis=1)], axis=0)

    def rope_row(i):
        return jnp.where(i * TM < tp, 0, 1 + ((i * TM - tp) % lat_len) // TM)

    new_k, new_v = [], []
    for l in range(depth):
        kind, j = l % 3, l // 3
        mod = mod_all[l]
        if kind == 0:
            y = _gmlp(y, mod, norm_mix[l], gm_w_in[j], gm_norm_v[j], gm_w_s[j], gm_b_s[j], gm_w_out[j], mod_row)
        elif kind == 1:
            q, k, v = _qkv(y, mod, norm_mix[l], at_w_qkv[j], at_q_norm[j], at_k_norm[j],
                           cos_t, sin_t, mod_row, rope_row)
            new_k.append(k[:tp].reshape(n_ctx, ctx_len, N_KV_HEADS, HEAD_DIM))
            new_v.append(v[:tp].reshape(n_ctx, ctx_len, N_KV_HEADS, HEAD_DIM))
            past = cache_k.shape[2]
            ck = cache_k[:, j].reshape(n_lat, past, N_KV_HEADS * HEAD_DIM)
            cv = cache_v[:, j].reshape(n_lat, past, N_KV_HEADS * HEAD_DIM)
            y = _attention(y, mod, q, k, v, ck, cv, at_w_o[j], n_ctx, ctx_len, n_lat, lat_len)
        else:
            y = _pool(y, mod, norm_mix[l], pool_w_grp[j], pool_scale[j], n_ctx, ctx_len, 0, lambda b: 0)
            y = _pool(y, mod, norm_mix[l], pool_w_grp[j], pool_scale[j], n_lat, lat_len, tp, lambda b: 1 + b)
        y = _moe(l, y, mod, norm_ffn[l], router_w[l], router_b[l], w_gate_up, b_gate_up,
                 w_down, b_down, mod_row)

    return (y[:tp].reshape(n_ctx, ctx_len, d), y[tp:].reshape(n_lat, lat_len, d),
            jnp.stack(new_k, axis=1), jnp.stack(new_v, axis=1))
```

```python
import functools

import jax
import jax.numpy as jnp
from jax import lax
from jax.experimental import pallas as pl
from jax.experimental.pallas import tpu as pltpu

F32 = jnp.float32
BF16 = jnp.bfloat16
HIGHEST = lax.Precision.HIGHEST

LANES = 128
SUBLANES = 8
VMEM_LIMIT_BYTES = 56 * 1024 * 1024

EPS = 1e-6
GRID_W = 64
CHUNK = 128
HEAD_DIM = 128
N_KV_HEADS = 2
ROPE_THETA = 10000.0
POOL_WINDOWS = (2, 4, 8, 16)
TOP_K = 4
SWIGLU_ALPHA = 1.702
SWIGLU_LIMIT = 7.0

TM = 256
MOE_ROWS = 512
MOE_GRANULE = 128
UNIT = SUBLANES


def _params(n_axes=1):
    return pltpu.CompilerParams(dimension_semantics=("arbitrary",) * n_axes,
                                vmem_limit_bytes=VMEM_LIMIT_BYTES)


def _rms(x, gain):
    return x * lax.rsqrt(jnp.mean(x * x, axis=-1, keepdims=True) + EPS) * gain


def _norm_mod(y, gain, shift, scale):
    return _rms(y, gain) * (1.0 + scale) + shift


def _gelu_tanh(x):
    return 0.5 * x * (1.0 + jnp.tanh(0.7978845608028654 * (x + 0.044715 * (x * x * x))))


def _dot(a, b):
    return jnp.dot(a, b, preferred_element_type=F32)


def _dot_nt(a, b):
    return lax.dot_general(a, b, (((1,), (1,)), ((), ())), preferred_element_type=F32)


def _ada_kernel(cv_ref, w_ref, b_ref, o_ref):
    cv = cv_ref[...]
    s = cv * jax.nn.sigmoid(cv)
    o_ref[0, 0] = jnp.dot(s, w_ref[0], precision=HIGHEST, preferred_element_type=F32) + b_ref[0, 0]


def _ada_params(cvecs, w_mod, b_mod):
    depth, d, _ = w_mod.shape
    rows = cvecs.shape[0]
    return pl.pallas_call(
        _ada_kernel,
        grid=(depth, 6),
        in_specs=[
            pl.BlockSpec((rows, d), lambda l, j: (0, 0)),
            pl.BlockSpec((1, d, d), lambda l, j: (l, 0, j)),
            pl.BlockSpec((1, 1, 1, d), lambda l, j: (l, j, 0, 0)),
        ],
        out_specs=pl.BlockSpec((1, 1, rows, d), lambda l, j: (l, j, 0, 0)),
        out_shape=jax.ShapeDtypeStruct((depth, 6, rows, d), F32),
        compiler_params=_params(2),
        name="ada_params",
    )(cvecs, w_mod, b_mod.reshape(depth, 6, 1, d))


def _gmlp_kernel(y_ref, mod_ref, gain_ref, win_ref, nv_ref, ws_ref, bexp_ref, wout_ref, o_ref):
    y = y_ref[...]
    inner = nv_ref.shape[-1]
    groups = ws_ref.shape[0]
    h = _norm_mod(y, gain_ref[...], mod_ref[0, 0:1, :], mod_ref[0, 1:2, :])
    z = _gelu_tanh(_dot(h.astype(BF16), win_ref[...]))
    u = z[:, :inner]
    v = _rms(z[:, inner:], nv_ref[...]).astype(BF16)
    bexp = bexp_ref[...]
    chunks = []
    for c in range(y.shape[0] // CHUNK):
        cols = []
        for g in range(groups):
            vg = v[c * CHUNK:(c + 1) * CHUNK, g * LANES:(g + 1) * LANES]
            cols.append(_dot(ws_ref[g], vg))
        chunks.append(jnp.concatenate(cols, axis=1) + bexp)
    mixed = jnp.concatenate(chunks, axis=0)
    o = _dot((u * mixed).astype(BF16), wout_ref[...])
    o_ref[...] = y + mod_ref[0, 2:3, :] * o


def _gmlp(y, mod, gain, w_in, norm_v, w_s, b_s, w_out, mod_row):
    t_all, d = y.shape
    inner = norm_v.shape[-1]
    groups = w_s.shape[0]
    bexp = jnp.repeat(b_s.T, inner // groups, axis=1)
    return pl.pallas_call(
        _gmlp_kernel,
        grid=(t_all // TM,),
        in_specs=[
            pl.BlockSpec((TM, d), lambda i: (i, 0)),
            pl.BlockSpec((1, 6, d), lambda i: (mod_row(i), 0, 0)),
            pl.BlockSpec((1, d), lambda i: (0, 0)),
            pl.BlockSpec((d, 2 * inner), lambda i: (0, 0)),
            pl.BlockSpec((1, inner), lambda i: (0, 0)),
            pl.BlockSpec((groups, CHUNK, CHUNK), lambda i: (0, 0, 0)),
            pl.BlockSpec((CHUNK, inner), lambda i: (0, 0)),
            pl.BlockSpec((inner, d), lambda i: (0, 0)),
        ],
        out_specs=pl.BlockSpec((TM, d), lambda i: (i, 0)),
        out_shape=jax.ShapeDtypeStruct((t_all, d), F32),
        compiler_params=_params(),
        name="gmlp",
    )(y, mod, gain.reshape(1, d), w_in.astype(BF16), norm_v.reshape(1, inner),
      w_s.astype(BF16), bexp, w_out.astype(BF16))


def _qkv_kernel(y_ref, mod_ref, gain_ref, w_ref, qn_ref, kn_ref, cos_ref, sin_ref,
                q_ref, k_ref, v_ref):
    y = y_ref[...]
    n_q = q_ref.shape[-1] // HEAD_DIM
    n_kv = k_ref.shape[-1] // HEAD_DIM
    h = _norm_mod(y, gain_ref[...], mod_ref[0, 0:1, :], mod_ref[0, 1:2, :])
    qkv = _dot(h.astype(BF16), w_ref[...])
    cos = cos_ref[...]
    sin = sin_ref[...]
    lane = lax.broadcasted_iota(jnp.int32, cos.shape, 1)
    low_half = (lane % (HEAD_DIM // 2)) < (HEAD_DIM // 4)

    def head(idx, norm):
        x = _rms(qkv[:, idx * HEAD_DIM:(idx + 1) * HEAD_DIM], norm)
        partner = jnp.where(low_half,
                            pltpu.roll(x, HEAD_DIM - HEAD_DIM // 4, 1),
                            pltpu.roll(x, HEAD_DIM // 4, 1))
        return x * cos + partner * sin

    qn = qn_ref[...]
    kn = kn_ref[...]
    q_ref[...] = jnp.concatenate([head(i, qn) for i in range(n_q)], axis=1).astype(q_ref.dtype)
    k_ref[...] = jnp.concatenate([head(n_q + i, kn) for i in range(n_kv)], axis=1)
    v_ref[...] = qkv[:, (n_q + n_kv) * HEAD_DIM:]


def _qkv(y, mod, gain, w_qkv, q_norm, k_norm, cos_t, sin_t, mod_row, rope_row):
    t_all, d = y.shape
    n_kv_cols = N_KV_HEADS * HEAD_DIM
    return pl.pallas_call(
        _qkv_kernel,
        grid=(t_all // TM,),
        in_specs=[
            pl.BlockSpec((TM, d), lambda i: (i, 0)),
            pl.BlockSpec((1, 6, d), lambda i: (mod_row(i), 0, 0)),
            pl.BlockSpec((1, d), lambda i: (0, 0)),
            pl.BlockSpec(w_qkv.shape, lambda i: (0, 0)),
            pl.BlockSpec((1, HEAD_DIM), lambda i: (0, 0)),
            pl.BlockSpec((1, HEAD_DIM), lambda i: (0, 0)),
            pl.BlockSpec((TM, HEAD_DIM), lambda i: (rope_row(i), 0)),
            pl.BlockSpec((TM, HEAD_DIM), lambda i: (rope_row(i), 0)),
        ],
        out_specs=[
            pl.BlockSpec((TM, d), lambda i: (i, 0)),
            pl.BlockSpec((TM, n_kv_cols), lambda i: (i, 0)),
            pl.BlockSpec((TM, n_kv_cols), lambda i: (i, 0)),
        ],
        out_shape=[
            jax.ShapeDtypeStruct((t_all, d), BF16),
            jax.ShapeDtypeStruct((t_all, n_kv_cols), F32),
            jax.ShapeDtypeStruct((t_all, n_kv_cols), F32),
        ],
        compiler_params=_params(),
        name="qkv_project",
    )(y, mod, gain.reshape(1, d), w_qkv.astype(BF16), q_norm.reshape(1, HEAD_DIM),
      k_norm.reshape(1, HEAD_DIM), cos_t, sin_t)


def _attend(q, key_sets):
    n_heads = q.shape[1] // HEAD_DIM
    rep = n_heads // N_KV_HEADS
    scale = HEAD_DIM ** -0.5
    outs = []
    for hd in range(n_heads):
        g = hd // rep
        qh = q[:, hd * HEAD_DIM:(hd + 1) * HEAD_DIM]
        cols = slice(g * HEAD_DIM, (g + 1) * HEAD_DIM)
        scores = [_dot_nt(qh, k[:, cols]) * scale for k, _ in key_sets]
        m = functools.reduce(jnp.maximum, [jnp.max(s, axis=-1, keepdims=True) for s in scores])
        probs = [jnp.exp(s - m) for s in scores]
        denom = functools.reduce(jnp.add, [jnp.sum(p, axis=-1, keepdims=True) for p in probs])
        o = functools.reduce(jnp.add, [_dot(p.astype(BF16), v[:, cols])
                                       for p, (_, v) in zip(probs, key_sets)])
        outs.append(o / denom)
    return jnp.concatenate(outs, axis=1)


def _attn_ctx_kernel(y_ref, mod_ref, q_ref, k_ref, v_ref, wo_ref, o_ref):
    att = _attend(q_ref[...], [(k_ref[...].astype(BF16), v_ref[...].astype(BF16))])
    o_ref[...] = y_ref[...] + mod_ref[0, 2:3, :] * _dot(att.astype(BF16), wo_ref[...])


def _attn_lat_kernel(y_ref, mod_ref, q_ref, k_ref, v_ref, ck_ref, cv_ref, wo_ref, o_ref):
    sets = [(ck_ref[0].astype(BF16), cv_ref[0].astype(BF16)),
            (k_ref[...].astype(BF16), v_ref[...].astype(BF16))]
    att = _attend(q_ref[...], sets)
    o_ref[...] = y_ref[...] + mod_ref[0, 2:3, :] * _dot(att.astype(BF16), wo_ref[...])


def _attention(y, mod, q, k, v, cache_k, cache_v, w_o, n_ctx, ctx_len, n_lat, lat_len):
    t_all, d = y.shape
    kvc = k.shape[1]
    tp = n_ctx * ctx_len
    wo = w_o.astype(BF16)
    y = pl.pallas_call(
        _attn_ctx_kernel,
        grid=(n_ctx,),
        in_specs=[
            pl.BlockSpec((ctx_len, d), lambda b: (b, 0)),
            pl.BlockSpec((1, 6, d), lambda b: (0, 0, 0)),
            pl.BlockSpec((ctx_len, d), lambda b: (b, 0)),
            pl.BlockSpec((ctx_len, kvc), lambda b: (b, 0)),
            pl.BlockSpec((ctx_len, kvc), lambda b: (b, 0)),
            pl.BlockSpec((d, d), lambda b: (0, 0)),
        ],
        out_specs=pl.BlockSpec((ctx_len, d), lambda b: (b, 0)),
        out_shape=jax.ShapeDtypeStruct((t_all, d), F32),
        input_output_aliases={0: 0},
        compiler_params=_params(),
        name="attn_context",
    )(y, mod, q, k, v, wo)
    tq = TM
    qb = lat_len // tq
    past = cache_k.shape[1]
    y = pl.pallas_call(
        _attn_lat_kernel,
        grid=(n_lat, qb),
        in_specs=[
            pl.BlockSpec((tq, d), lambda b, i: (tp // tq + b * qb + i, 0)),
            pl.BlockSpec((1, 6, d), lambda b, i: (1 + b, 0, 0)),
            pl.BlockSpec((tq, d), lambda b, i: (tp // tq + b * qb + i, 0)),
            pl.BlockSpec((lat_len, kvc), lambda b, i: (tp // lat_len + b, 0)),
            pl.BlockSpec((lat_len, kvc), lambda b, i: (tp // lat_len + b, 0)),
            pl.BlockSpec((1, past, kvc), lambda b, i: (b, 0, 0)),
            pl.BlockSpec((1, past, kvc), lambda b, i: (b, 0, 0)),
            pl.BlockSpec((d, d), lambda b, i: (0, 0)),
        ],
        out_specs=pl.BlockSpec((tq, d), lambda b, i: (tp // tq + b * qb + i, 0)),
        out_shape=jax.ShapeDtypeStruct((t_all, d), F32),
        input_output_aliases={0: 0},
        compiler_params=_params(2),
        name="attn_latent",
    )(y, mod, q, k, v, cache_k, cache_v, wo)
    return y


def _pool_kernel(y_ref, mod_ref, gain_ref, wg_ref, scale_ref, o_ref):
    y = y_ref[...]
    s_len, d = y.shape
    gd = d // len(POOL_WINDOWS)
    h = _norm_mod(y, gain_ref[...], mod_ref[0, 0:1, :], mod_ref[0, 1:2, :])
    t_idx = lax.broadcasted_iota(jnp.int32, (s_len, s_len), 0)
    s_idx = lax.broadcasted_iota(jnp.int32, (s_len, s_len), 1)
    t_col = lax.broadcasted_iota(jnp.int32, (s_len, 1), 0)
    parts = []
    for g, w in enumerate(POOL_WINDOWS):
        lo = t_idx - w // 2
        window = ((s_idx >= lo) & (s_idx < lo + w)).astype(BF16)
        count = (jnp.minimum(t_col - w // 2 + w, s_len) - jnp.maximum(t_col - w // 2, 0)).astype(F32)
        hg = h[:, g * gd:(g + 1) * gd]
        h_hi = hg.astype(BF16)
        h_lo = (hg - h_hi.astype(F32)).astype(BF16)
        mean = (_dot(window, h_hi) + _dot(window, h_lo)) / count
        parts.append(_dot((mean - hg).astype(BF16), wg_ref[g]))
    mixed = jnp.concatenate(parts, axis=1) * scale_ref[...]
    o_ref[...] = y + mod_ref[0, 2:3, :] * mixed


def _pool(y, mod, gain, w_grp, scale, n_seq, s_len, row_off, mod_off):
    t_all, d = y.shape
    blk_off = row_off // s_len
    return pl.pallas_call(
        _pool_kernel,
        grid=(n_seq,),
        in_specs=[
            pl.BlockSpec((s_len, d), lambda b: (blk_off + b, 0)),
            pl.BlockSpec((1, 6, d), lambda b: (mod_off(b), 0, 0)),
            pl.BlockSpec((1, d), lambda b: (0, 0)),
            pl.BlockSpec(w_grp.shape, lambda b: (0, 0, 0)),
            pl.BlockSpec((1, d), lambda b: (0, 0)),
        ],
        out_specs=pl.BlockSpec((s_len, d), lambda b: (blk_off + b, 0)),
        out_shape=jax.ShapeDtypeStruct((t_all, d), F32),
        input_output_aliases={0: 0},
        compiler_params=_params(),
        name="pool_mixer",
    )(y, mod, gain.reshape(1, d), w_grp.astype(BF16), scale.reshape(1, d))


def _sorted_rows(n_exp):
    worst = TM * TOP_K + n_exp * (UNIT - 1) + UNIT
    return -(-worst // LANES) * LANES


def _route_kernel(y_ref, mod_ref, gain_ref, rw_ref, rb_ref, xs_ref, route_ref, cnt_ref):
    y = y_ref[...]
    tm = y.shape[0]
    n_rows = xs_ref.shape[0]
    h = _norm_mod(y, gain_ref[...], mod_ref[0, 3:4, :], mod_ref[0, 4:5, :])
    h_hi = h.astype(BF16)
    h_lo = (h - h_hi.astype(F32)).astype(BF16)
    hi_terms = _dot(h_hi, rw_ref[...])
    logits = hi_terms[:, :LANES] + hi_terms[:, LANES:] + _dot(h_lo, rw_ref[:, :LANES]) + rb_ref[...]
    lane = lax.broadcasted_iota(jnp.int32, logits.shape, 1).astype(F32)
    work = logits
    vals, ids = [], []
    for _ in range(TOP_K):
        m = jnp.max(work, axis=-1, keepdims=True)
        idx = jnp.min(jnp.where(work == m, lane, float(LANES)), axis=-1, keepdims=True)
        vals.append(m)
        ids.append(idx)
        work = jnp.where(lane == idx, -jnp.inf, work)
    exps = [jnp.exp(v - vals[0]) for v in vals]
    denom = functools.reduce(jnp.add, exps)
    onehot = functools.reduce(jnp.add, [(lane == idx).astype(F32) for idx in ids])

    counts = jnp.sum(onehot, axis=0, keepdims=True)
    seg = jnp.ceil(counts * (1.0 / UNIT)) * UNIT
    e_row = lax.broadcasted_iota(jnp.int32, (LANES, LANES), 0)
    e_col = lax.broadcasted_iota(jnp.int32, (LANES, LANES), 1)
    earlier = (e_row < e_col).astype(BF16)
    seg_off = _dot(jnp.broadcast_to(seg, (SUBLANES, LANES)).astype(BF16), earlier)[0:1, :]
    row = lax.broadcasted_iota(jnp.int32, (tm, tm), 0)
    col = lax.broadcasted_iota(jnp.int32, (tm, tm), 1)
    before = (col < row).astype(BF16)
    pos_mat = _dot(before, onehot.astype(BF16)) + seg_off

    route = jnp.zeros(logits.shape, F32)
    for k in range(TOP_K):
        pos_k = jnp.sum(jnp.where(lane == ids[k], pos_mat, 0.0), axis=-1, keepdims=True)
        route = jnp.where(lane == k, exps[k] / denom, route)
        route = jnp.where(lane == TOP_K + k, ids[k], route)
        route = jnp.where(lane == 2 * TOP_K + k, pos_k, route)
    route_ref[...] = route
    cnt_ref[...] = jnp.broadcast_to(counts, cnt_ref.shape)

    route_t = jnp.transpose(route)
    out_row = lax.broadcasted_iota(jnp.int32, (n_rows, tm), 0).astype(F32)
    perm = functools.reduce(jnp.add, [(out_row == route_t[2 * TOP_K + k:2 * TOP_K + k + 1, :]).astype(F32)
                                      for k in range(TOP_K)])
    xs_ref[...] = _dot(perm.astype(BF16), h_hi)


def _route(y, mod, gain, router_w, router_b, mod_row):
    t_all, d = y.shape
    n_exp = router_w.shape[1]
    n_tiles = t_all // TM
    n_rows = _sorted_rows(n_exp)
    rw = jnp.pad(router_w, ((0, 0), (0, LANES - n_exp)))
    rw_hi = rw.astype(BF16)
    rw = jnp.concatenate([rw_hi, (rw - rw_hi.astype(F32)).astype(BF16)], axis=1)
    rb = jnp.pad(router_b, (0, LANES - n_exp), constant_values=-1e30).reshape(1, LANES)
    return pl.pallas_call(
        _route_kernel,
        grid=(n_tiles,),
        in_specs=[
            pl.BlockSpec((TM, d), lambda i: (i, 0)),
            pl.BlockSpec((1, 6, d), lambda i: (mod_row(i), 0, 0)),
            pl.BlockSpec((1, d), lambda i: (0, 0)),
            pl.BlockSpec((d, 2 * LANES), lambda i: (0, 0)),
            pl.BlockSpec((1, LANES), lambda i: (0, 0)),
        ],
        out_specs=[
            pl.BlockSpec((n_rows, d), lambda i: (i, 0)),
            pl.BlockSpec((TM, LANES), lambda i: (i, 0)),
            pl.BlockSpec((SUBLANES, LANES), lambda i: (i, 0)),
        ],
        out_shape=[
            jax.ShapeDtypeStruct((n_tiles * n_rows + 2 * MOE_ROWS, d), F32),
            jax.ShapeDtypeStruct((t_all, LANES), F32),
            jax.ShapeDtypeStruct((n_tiles * SUBLANES, LANES), F32),
        ],
        compiler_params=_params(),
        name="moe_route",
    )(y, mod, gain.reshape(1, d), rw, rb)


def _plan_kernel(n_tiles, n_exp, n_rows, c_ref, srcx_ref, dstx_ref, be_ref, nxt_ref, nu_ref):
    nb = srcx_ref.shape[0]
    seg = jnp.ceil(c_ref[...] * (1.0 / UNIT)) * UNIT
    seg_b = seg.astype(BF16)
    sq_r = lax.broadcasted_iota(jnp.int32, (LANES, LANES), 0)
    sq_c = lax.broadcasted_iota(jnp.int32, (LANES, LANES), 1)
    upto = (sq_r <= sq_c).astype(BF16)
    seg_end_t = _dot(seg_b, upto)
    seg_off_t = seg_end_t - seg
    seg_end_e = _dot((sq_c <= sq_r).astype(BF16), seg_b)
    seg_off_e = seg_end_e - seg
    rows_e = seg_end_e[n_tiles - 1:n_tiles, :]
    nblk = jnp.ceil(rows_e * (1.0 / MOE_ROWS))
    end_blk = _dot(jnp.broadcast_to(nblk, (SUBLANES, LANES)).astype(BF16), upto)[0:1, :]
    start_blk = end_blk - nblk
    n_used = jnp.max(end_blk, axis=-1, keepdims=True)

    lane1 = lax.broadcasted_iota(jnp.int32, (1, LANES), 1).astype(F32)
    lane = lax.broadcasted_iota(jnp.int32, (nb, LANES), 1).astype(F32)
    blk = lax.broadcasted_iota(jnp.int32, (nb, LANES), 0).astype(F32)
    is_exp = lane1 < n_exp
    has_rows = (nblk > 0) & is_exp
    be = jnp.minimum(jnp.sum(((end_blk <= blk) & is_exp).astype(F32), axis=-1, keepdims=True), n_exp - 1.0)
    last_e = jnp.max(jnp.where(has_rows, lane1, 0.0), axis=-1, keepdims=True)
    used = blk[:, 0:1] < n_used
    be = jnp.where(used, be, last_e)
    nxt = jnp.min(jnp.where((lane > be) & has_rows, lane, float(LANES)), axis=-1, keepdims=True)
    nxt = jnp.where(nxt == LANES, be, nxt)
    onehot = (lane == be).astype(F32)

    def column_of_block(x):
        hi = jnp.floor(x * (1.0 / 256.0))
        lo = x - hi * 256.0
        oh = onehot.astype(BF16)
        return _dot_nt(oh, hi.astype(BF16)) * 256.0 + _dot_nt(oh, lo.astype(BF16))

    ends_b = column_of_block(seg_end_e)
    shift_b = column_of_block(seg_off_t) - column_of_block(seg_off_e)
    start_b = jnp.sum(onehot * start_blk, axis=-1, keepdims=True)
    rows_b = jnp.sum(onehot * rows_e, axis=-1, keepdims=True)
    r = (blk - start_b) * MOE_ROWS + lane * UNIT
    tile_of = jnp.zeros((nb, LANES), F32)
    for t in range(n_tiles):
        tile_of = tile_of + (ends_b[:, t:t + 1] <= r).astype(F32)
    tile_c = jnp.minimum(tile_of, n_tiles - 1.0)
    src = jnp.zeros((nb, LANES), F32)
    for t in range(n_tiles):
        src = src + jnp.where(tile_c == t, shift_b[:, t:t + 1] + float(t * n_rows), 0.0)
    valid = (r < rows_b) & used & (lane < MOE_ROWS // UNIT)
    srcx_ref[...] = jnp.where(valid, src + r, float(n_rows - UNIT)).astype(jnp.int32)
    spare = float(n_tiles * n_rows) + (blk - 2.0 * jnp.floor(blk * 0.5)) * MOE_ROWS + lane * UNIT
    dstx_ref[...] = jnp.where(valid, src + r, spare).astype(jnp.int32)

    rows_here = jnp.clip(rows_b - (blk[:, 0:1] - start_b) * MOE_ROWS, 0.0, float(MOE_ROWS))
    be_ref[...] = jnp.where(lane == 1, rows_here, be).astype(jnp.int32)
    nxt_ref[...] = jnp.broadcast_to(nxt, nxt_ref.shape).astype(jnp.int32)
    nu_ref[...] = jnp.broadcast_to(n_used, nu_ref.shape).astype(jnp.int32)


def _plan(counts, n_tiles, n_exp, n_rows, n_blocks):
    assert n_tiles <= LANES and n_exp <= LANES and n_tiles % SUBLANES == 0
    blk_units = MOE_ROWS // UNIT
    assert blk_units <= LANES
    nb = -(-n_blocks // SUBLANES) * SUBLANES
    srcx, dstx, be, nxt, nu = pl.pallas_call(
        functools.partial(_plan_kernel, n_tiles, n_exp, n_rows),
        out_shape=[
            jax.ShapeDtypeStruct((nb, LANES), jnp.int32),
            jax.ShapeDtypeStruct((nb, LANES), jnp.int32),
            jax.ShapeDtypeStruct((nb, LANES), jnp.int32),
            jax.ShapeDtypeStruct((nb, LANES), jnp.int32),
            jax.ShapeDtypeStruct((SUBLANES, LANES), jnp.int32),
        ],
        compiler_params=pltpu.CompilerParams(vmem_limit_bytes=VMEM_LIMIT_BYTES),
        name="moe_plan",
    )(counts)
    before = n_tiles * n_rows + MOE_ROWS + UNIT * jnp.arange(blk_units, dtype=jnp.int32)
    dst = jnp.concatenate([before, dstx[:n_blocks, :blk_units].reshape(-1)])
    return (srcx[:n_blocks, :blk_units].reshape(-1), dst, be[:n_blocks, 0], be[:n_blocks, 1],
            nxt[:n_blocks, 0], nu[0, :1])


def _gather_units(src_ref, base, table_ref, dst, sem):
    for u in range(dst.shape[0] // UNIT):
        start = pl.multiple_of(table_ref[base + u], UNIT)
        pltpu.make_async_copy(src_ref.at[pl.ds(start, UNIT)], dst.at[pl.ds(u * UNIT, UNIT)], sem).start()


def _scatter_units(src, base, table_ref, dst_ref, sem):
    for u in range(src.shape[0] // UNIT):
        start = pl.multiple_of(table_ref[base + u], UNIT)
        pltpu.make_async_copy(src.at[pl.ds(u * UNIT, UNIT)], dst_ref.at[pl.ds(start, UNIT)], sem).start()


def _wait_units(buf, sem):
    pltpu.make_async_copy(buf, buf, sem).wait()


def _expert_kernel(layer, be_ref, rows_ref, nu_ref, src_ref, dst_ref, nxt_ref, xs_ref, wgu_hbm, bgu_ref, wdn_hbm,
                   bdn_ref, ys_ref, wgu_st, wdn_st, wgu_bf, wdn_bf, x0, x1, y0, y1, gsem, ssem, wsem):
    i = pl.program_id(0)
    n_used = nu_ref[0]
    n_blocks = pl.num_programs(0)
    units = x0.shape[0] // UNIT
    d_exp = wdn_bf.shape[0]

    def weight_copies(e):
        return (pltpu.make_async_copy(wgu_hbm.at[layer, e], wgu_st, wsem.at[0]),
                pltpu.make_async_copy(wdn_hbm.at[layer, e], wdn_st, wsem.at[1]))

    def run_block(xcur, xoth, ycur, yoth, p, m):
        _wait_units(xcur, gsem.at[p])

        @pl.when(i >= 1)
        def _():
            _wait_units(ycur, ssem.at[p])

        _gather_units(xs_ref, jnp.minimum(i + 1, n_blocks - 1) * units, src_ref, xoth, gsem.at[1 - p])
        _scatter_units(yoth, i * units, dst_ref, ys_ref, ssem.at[1 - p])
        x = xcur[0:m, :].astype(BF16)
        gu = _dot(x, wgu_bf[...]) + bgu_ref[0, 0]
        gate = jnp.minimum(gu[:, :d_exp], SWIGLU_LIMIT)
        up = jnp.clip(gu[:, d_exp:], -SWIGLU_LIMIT, SWIGLU_LIMIT)
        hid = (up + 1.0) * gate * jax.nn.sigmoid(SWIGLU_ALPHA * gate)
        ycur[0:m, :] = _dot(hid.astype(BF16), wdn_bf[...]) + bdn_ref[0, 0]

        @pl.when(i == n_used - 1)
        def _():
            _wait_units(xoth, gsem.at[1 - p])
            _scatter_units(ycur, (i + 1) * units, dst_ref, ys_ref, ssem.at[p])
            _wait_units(yoth, ssem.at[1 - p])
            _wait_units(ycur, ssem.at[p])

    @pl.when(i < n_used)
    def _():
        @pl.when(i == 0)
        def _():
            for cp in weight_copies(be_ref[0]):
                cp.start(priority=1)
            _gather_units(xs_ref, 0, src_ref, x0, gsem.at[0])
            y0[...] = jnp.zeros_like(y0)
            y1[...] = jnp.zeros_like(y1)

        new_expert = jnp.logical_or(i == 0, be_ref[i] != be_ref[jnp.maximum(i - 1, 0)])

        @pl.when(new_expert)
        def _():
            for cp in weight_copies(be_ref[i]):
                cp.wait()
            wgu_bf[...] = wgu_st[...].astype(BF16)
            wdn_bf[...] = wdn_st[...].astype(BF16)

            @pl.when(nxt_ref[i] != be_ref[i])
            def _():
                for cp in weight_copies(nxt_ref[i]):
                    cp.start(priority=1)

        n_gran = x0.shape[0] // MOE_GRANULE
        granules = jnp.clip((rows_ref[i] + MOE_GRANULE - 1) // MOE_GRANULE, 1, n_gran)
        for g in range(1, n_gran + 1):
            @pl.when(jnp.logical_and(i % 2 == 0, granules == g))
            def _():
                run_block(x0, x1, y0, y1, 0, g * MOE_GRANULE)

            @pl.when(jnp.logical_and(i % 2 == 1, granules == g))
            def _():
                run_block(x1, x0, y1, y0, 1, g * MOE_GRANULE)


def _experts(layer, xs, src_units, dst_units, block_e, block_rows, next_e, n_used, w_gu, b_gu, w_dn, b_dn,
             n_blocks):
    d = w_dn.shape[-1]
    depth, n_exp, _, two_f = w_gu.shape
    d_exp = two_f // 2
    n_prefetch = 6

    def bias_block(i, be, rows, nu, src, dst, nxt):
        return (layer, be[i], 0, 0)

    grid_spec = pltpu.PrefetchScalarGridSpec(
        num_scalar_prefetch=n_prefetch,
        grid=(n_blocks,),
        in_specs=[
            pl.BlockSpec(memory_space=pl.ANY),
            pl.BlockSpec(memory_space=pl.ANY),
            pl.BlockSpec((1, 1, 1, two_f), bias_block),
            pl.BlockSpec(memory_space=pl.ANY),
            pl.BlockSpec((1, 1, 1, d), bias_block),
        ],
        out_specs=pl.BlockSpec(memory_space=pl.ANY),
        scratch_shapes=[
            pltpu.VMEM((d, two_f), F32),
            pltpu.VMEM((d_exp, d), F32),
            pltpu.VMEM((d, two_f), BF16),
            pltpu.VMEM((d_exp, d), BF16),
            pltpu.VMEM((MOE_ROWS, d), F32),
            pltpu.VMEM((MOE_ROWS, d), F32),
            pltpu.VMEM((MOE_ROWS, d), F32),
            pltpu.VMEM((MOE_ROWS, d), F32),
            pltpu.SemaphoreType.DMA((2,)),
            pltpu.SemaphoreType.DMA((2,)),
            pltpu.SemaphoreType.DMA((2,)),
        ],
    )
    return pl.pallas_call(
        functools.partial(_expert_kernel, layer),
        grid_spec=grid_spec,
        out_shape=jax.ShapeDtypeStruct(xs.shape, xs.dtype),
        input_output_aliases={n_prefetch: 0},
        compiler_params=_params(),
        name="moe_experts",
    )(block_e, block_rows, n_used, src_units, dst_units, next_e, xs, w_gu, b_gu.reshape(depth, n_exp, 1, two_f),
      w_dn, b_dn.reshape(depth, n_exp, 1, d))


def _combine_kernel(y_ref, mod_ref, route_ref, ys_ref, o_ref):
    route = route_ref[...]
    tm = route.shape[0]
    n_rows = ys_ref.shape[0]
    col = lax.broadcasted_iota(jnp.int32, (tm, n_rows), 1).astype(F32)
    weights = functools.reduce(jnp.add, [
        jnp.where(col == route[:, 2 * TOP_K + k:2 * TOP_K + k + 1], route[:, k:k + 1], 0.0)
        for k in range(TOP_K)])
    acc = _dot(weights.astype(BF16), ys_ref[...].astype(BF16))
    o_ref[...] = y_ref[...] + mod_ref[0, 5:6, :] * acc


def _combine(y, mod, route, ys, n_rows, mod_row):
    t_all, d = y.shape
    return pl.pallas_call(
        _combine_kernel,
        grid=(t_all // TM,),
        in_specs=[
            pl.BlockSpec((TM, d), lambda i: (i, 0)),
            pl.BlockSpec((1, 6, d), lambda i: (mod_row(i), 0, 0)),
            pl.BlockSpec((TM, LANES), lambda i: (i, 0)),
            pl.BlockSpec((n_rows, d), lambda i: (i, 0)),
        ],
        out_specs=pl.BlockSpec((TM, d), lambda i: (i, 0)),
        out_shape=jax.ShapeDtypeStruct((t_all, d), F32),
        compiler_params=_params(),
        name="moe_combine",
    )(y, mod, route, ys)


def _moe(layer, y, mod, gain, router_w, router_b, w_gu, b_gu, w_dn, b_dn, mod_row):
    t_all, d = y.shape
    n_exp = router_w.shape[1]
    n_tiles = t_all // TM
    n_rows = _sorted_rows(n_exp)
    n_blocks = -(-(t_all * TOP_K + n_tiles * n_exp * (UNIT - 1)) // MOE_ROWS) + n_exp

    xs, route, cnt = _route(y, mod, gain, router_w, router_b, mod_row)
    counts = jnp.pad(cnt.reshape(n_tiles, SUBLANES, LANES)[:, 0, :], ((0, LANES - n_tiles), (0, 0)))
    src_x, dst_x, block_e, block_rows, next_e, n_used = _plan(counts, n_tiles, n_exp, n_rows, n_blocks)
    ys = _experts(layer, xs, src_x, dst_x, block_e, block_rows, next_e, n_used, w_gu, b_gu, w_dn, b_dn, n_blocks)
    return _combine(y, mod, route, ys, n_rows, mod_row)


def kernel(x_prompt, x_sample, cache_k, cache_v, c, c_ctx, w_mod, b_mod, norm_mix, norm_ffn, gm_w_in, gm_norm_v, gm_w_s, gm_b_s, gm_w_out, at_w_qkv, at_q_norm, at_k_norm, at_w_o, pool_w_grp, pool_scale, router_w, router_b, w_gate_up, b_gate_up, w_down, b_down):
    n_ctx, ctx_len, d = x_prompt.shape
    n_lat, lat_len, _ = x_sample.shape
    depth = w_mod.shape[0]
    tp = n_ctx * ctx_len
    assert tp % lat_len == 0 and ctx_len % TM == 0 and lat_len % TM == 0 and ctx_len == TM
    assert 1 + n_lat <= SUBLANES

    y = jnp.concatenate([x_prompt.reshape(tp, d), x_sample.reshape(n_lat * lat_len, d)], axis=0)

    cvecs = jnp.zeros((SUBLANES, d), F32).at[0].set(c_ctx).at[1:1 + n_lat].set(c)
    mod_all = jnp.transpose(_ada_params(cvecs, w_mod, b_mod), (0, 2, 1, 3))

    def mod_row(i):
        return jnp.where(i * TM < tp, 0, 1 + (i * TM - tp) // lat_len)

    half = HEAD_DIM // 4
    inv = ROPE_THETA ** (-jnp.arange(half, dtype=F32) / half)
    pos = jnp.arange(lat_len)
    ang_r = (pos // GRID_W).astype(F32)[:, None] * inv[None, :]
    ang_c = (pos % GRID_W).astype(F32)[:, None] * inv[None, :]
    cos_t = jnp.concatenate([jnp.ones((TM, HEAD_DIM), F32),
                             jnp.concatenate([jnp.cos(ang_r)] * 2 + [jnp.cos(ang_c)] * 2, axis=1)], axis=0)
    sin_t = jnp.concatenate([jnp.zeros((TM, HEAD_DIM), F32),
                             jnp.concatenate([-jnp.sin(ang_r), jnp.sin(ang_r),
                                              -jnp.sin(ang_c), jnp.sin(ang_c)], axis=1)], axis=0)

    def rope_row(i):
        return jnp.where(i * TM < tp, 0, 1 + ((i * TM - tp) % lat_len) // TM)

    new_k, new_v = [], []
    for l in range(depth):
        kind, j = l % 3, l // 3
        mod = mod_all[l]
        if kind == 0:
            y = _gmlp(y, mod, norm_mix[l], gm_w_in[j], gm_norm_v[j], gm_w_s[j], gm_b_s[j], gm_w_out[j], mod_row)
        elif kind == 1:
            q, k, v = _qkv(y, mod, norm_mix[l], at_w_qkv[j], at_q_norm[j], at_k_norm[j],
                           cos_t, sin_t, mod_row, rope_row)
            new_k.append(k[:tp].reshape(n_ctx, ctx_len, N_KV_HEADS, HEAD_DIM))
            new_v.append(v[:tp].reshape(n_ctx, ctx_len, N_KV_HEADS, HEAD_DIM))
            past = cache_k.shape[2]
            ck = cache_k[:, j].reshape(n_lat, past, N_KV_HEADS * HEAD_DIM)
            cv = cache_v[:, j].reshape(n_lat, past, N_KV_HEADS * HEAD_DIM)
            y = _attention(y, mod, q, k, v, ck, cv, at_w_o[j], n_ctx, ctx_len, n_lat, lat_len)
        else:
            y = _pool(y, mod, norm_mix[l], pool_w_grp[j], pool_scale[j], n_ctx, ctx_len, 0, lambda b: 0)
            y = _pool(y, mod, norm_mix[l], pool_w_grp[j], pool_scale[j], n_lat, lat_len, tp, lambda b: 1 + b)
        y = _moe(l, y, mod, norm_ffn[l], router_w[l], router_b[l], w_gate_up, b_gate_up,
                 w_down, b_down, mod_row)

    return (y[:tp].reshape(n_ctx, ctx_len, d), y[tp:].reshape(n_lat, lat_len, d),
            jnp.stack(new_k, axis=1), jnp.stack(new_v, axis=1))
```

```python
import functools

import jax
import jax.numpy as jnp
from jax import lax
from jax.experimental import pallas as pl
from jax.experimental.pallas import tpu as pltpu

F32 = jnp.float32
BF16 = jnp.bfloat16

LANES = 128
SUBLANES = 8
VMEM_LIMIT_BYTES = 56 * 1024 * 1024

EPS = 1e-6
GRID_W = 64
CHUNK = 128
HEAD_DIM = 128
N_KV_HEADS = 2
ROPE_THETA = 10000.0
POOL_WINDOWS = (2, 4, 8, 16)
TOP_K = 4
SWIGLU_ALPHA = 1.702
SWIGLU_LIMIT = 7.0

TM = 256
MOE_ROWS = 512
MOE_GRANULE = 128
UNIT = SUBLANES


def _params(n_axes=1):
    return pltpu.CompilerParams(dimension_semantics=("arbitrary",) * n_axes,
                                vmem_limit_bytes=VMEM_LIMIT_BYTES)


def _rms(x, gain):
    return x * lax.rsqrt(jnp.mean(x * x, axis=-1, keepdims=True) + EPS) * gain


def _norm_mod(y, gain, shift, scale):
    return _rms(y, gain) * (1.0 + scale) + shift


def _gelu_tanh(x):
    return 0.5 * x * (1.0 + jnp.tanh(0.7978845608028654 * (x + 0.044715 * (x * x * x))))


def _dot(a, b):
    return jnp.dot(a, b, preferred_element_type=F32)


def _dot_nt(a, b):
    return lax.dot_general(a, b, (((1,), (1,)), ((), ())), preferred_element_type=F32)


def _ada_kernel(cv_ref, w_ref, b_ref, o_ref):
    cv = cv_ref[...]
    s = cv * jax.nn.sigmoid(cv)
    w = w_ref[0]
    s_hi = s.astype(BF16)
    s_lo = (s - s_hi.astype(F32)).astype(BF16)
    w_hi = w.astype(BF16)
    w_lo = (w - w_hi.astype(F32)).astype(BF16)
    o_ref[0, 0] = _dot(s_hi, w_hi) + _dot(s_lo, w_hi) + _dot(s_hi, w_lo) + b_ref[0, 0]


def _ada_params(cvecs, w_mod, b_mod):
    depth, d, _ = w_mod.shape
    rows = cvecs.shape[0]
    return pl.pallas_call(
        _ada_kernel,
        grid=(depth, 6),
        in_specs=[
            pl.BlockSpec((rows, d), lambda l, j: (0, 0)),
            pl.BlockSpec((1, d, d), lambda l, j: (l, 0, j)),
            pl.BlockSpec((1, 1, 1, d), lambda l, j: (l, j, 0, 0)),
        ],
        out_specs=pl.BlockSpec((1, 1, rows, d), lambda l, j: (l, j, 0, 0)),
        out_shape=jax.ShapeDtypeStruct((depth, 6, rows, d), F32),
        compiler_params=_params(2),
        name="ada_params",
    )(cvecs, w_mod, b_mod.reshape(depth, 6, 1, d))


def _gmlp_kernel(n_first, *refs):
    if n_first is None:
        y = refs[0][...]
        refs = refs[1:]
    else:
        y = jnp.where(pl.program_id(0) < n_first, refs[0][...], refs[1][...])
        refs = refs[2:]
    mod_ref, gain_ref, win_ref, nv_ref, ws_ref, bexp_ref, wout_ref, o_ref = refs
    inner = nv_ref.shape[-1]
    groups = ws_ref.shape[0]
    h = _norm_mod(y, gain_ref[...], mod_ref[0, 0:1, :], mod_ref[0, 1:2, :])
    z = _gelu_tanh(_dot(h.astype(BF16), win_ref[...]))
    u = z[:, :inner]
    v = _rms(z[:, inner:], nv_ref[...]).astype(BF16)
    bexp = bexp_ref[...]
    chunks = []
    for c in range(y.shape[0] // CHUNK):
        cols = []
        for g in range(groups):
            vg = v[c * CHUNK:(c + 1) * CHUNK, g * LANES:(g + 1) * LANES]
            cols.append(_dot(ws_ref[g], vg))
        chunks.append(jnp.concatenate(cols, axis=1) + bexp)
    mixed = jnp.concatenate(chunks, axis=0)
    o = _dot((u * mixed).astype(BF16), wout_ref[...])
    o_ref[...] = y + mod_ref[0, 2:3, :] * o


def _gmlp(y, mod, gain, w_in, norm_v, w_s, b_s, w_out, mod_row):
    if isinstance(y, tuple):
        n_first = y[0].shape[0] // TM
        t_all, d = y[0].shape[0] + y[1].shape[0], y[0].shape[1]
        y_specs = [pl.BlockSpec((TM, d), lambda i: (jnp.minimum(i, n_first - 1), 0)),
                   pl.BlockSpec((TM, d), lambda i: (jnp.maximum(i - n_first, 0), 0))]
    else:
        n_first, y = None, (y,)
        t_all, d = y[0].shape
        y_specs = [pl.BlockSpec((TM, d), lambda i: (i, 0))]
    inner = norm_v.shape[-1]
    groups = w_s.shape[0]
    bexp = jnp.repeat(b_s.T, inner // groups, axis=1)
    return pl.pallas_call(
        functools.partial(_gmlp_kernel, n_first),
        grid=(t_all // TM,),
        in_specs=y_specs + [
            pl.BlockSpec((1, 6, d), lambda i: (mod_row(i), 0, 0)),
            pl.BlockSpec((1, d), lambda i: (0, 0)),
            pl.BlockSpec((d, 2 * inner), lambda i: (0, 0)),
            pl.BlockSpec((1, inner), lambda i: (0, 0)),
            pl.BlockSpec((groups, CHUNK, CHUNK), lambda i: (0, 0, 0)),
            pl.BlockSpec((CHUNK, inner), lambda i: (0, 0)),
            pl.BlockSpec((inner, d), lambda i: (0, 0)),
        ],
        out_specs=pl.BlockSpec((TM, d), lambda i: (i, 0)),
        out_shape=jax.ShapeDtypeStruct((t_all, d), F32),
        compiler_params=_params(),
        name="gmlp",
    )(*y, mod, gain.reshape(1, d), w_in.astype(BF16), norm_v.reshape(1, inner),
      w_s.astype(BF16), bexp, w_out.astype(BF16))


def _qkv_kernel(y_ref, mod_ref, gain_ref, w_ref, qn_ref, kn_ref, cos_ref, sin_ref,
                q_ref, k_ref, v_ref):
    y = y_ref[...]
    n_q = q_ref.shape[-1] // HEAD_DIM
    n_kv = k_ref.shape[-1] // HEAD_DIM
    h = _norm_mod(y, gain_ref[...], mod_ref[0, 0:1, :], mod_ref[0, 1:2, :])
    qkv = _dot(h.astype(BF16), w_ref[...])
    cos = cos_ref[...]
    sin = sin_ref[...]
    lane = lax.broadcasted_iota(jnp.int32, cos.shape, 1)
    low_half = (lane % (HEAD_DIM // 2)) < (HEAD_DIM // 4)

    def head(idx, norm):
        x = _rms(qkv[:, idx * HEAD_DIM:(idx + 1) * HEAD_DIM], norm)
        partner = jnp.where(low_half,
                            pltpu.roll(x, HEAD_DIM - HEAD_DIM // 4, 1),
                            pltpu.roll(x, HEAD_DIM // 4, 1))
        return x * cos + partner * sin

    qn = qn_ref[...]
    kn = kn_ref[...]
    q_ref[...] = jnp.concatenate([head(i, qn) for i in range(n_q)], axis=1).astype(q_ref.dtype)
    k_ref[...] = jnp.concatenate([head(n_q + i, kn) for i in range(n_kv)], axis=1)
    v_ref[...] = qkv[:, (n_q + n_kv) * HEAD_DIM:]


def _qkv(y, mod, gain, w_qkv, q_norm, k_norm, cos_t, sin_t, mod_row, rope_row):
    t_all, d = y.shape
    n_kv_cols = N_KV_HEADS * HEAD_DIM
    return pl.pallas_call(
        _qkv_kernel,
        grid=(t_all // TM,),
        in_specs=[
            pl.BlockSpec((TM, d), lambda i: (i, 0)),
            pl.BlockSpec((1, 6, d), lambda i: (mod_row(i), 0, 0)),
            pl.BlockSpec((1, d), lambda i: (0, 0)),
            pl.BlockSpec(w_qkv.shape, lambda i: (0, 0)),
            pl.BlockSpec((1, HEAD_DIM), lambda i: (0, 0)),
            pl.BlockSpec((1, HEAD_DIM), lambda i: (0, 0)),
            pl.BlockSpec((TM, HEAD_DIM), lambda i: (rope_row(i), 0)),
            pl.BlockSpec((TM, HEAD_DIM), lambda i: (rope_row(i), 0)),
        ],
        out_specs=[
            pl.BlockSpec((TM, d), lambda i: (i, 0)),
            pl.BlockSpec((TM, n_kv_cols), lambda i: (i, 0)),
            pl.BlockSpec((TM, n_kv_cols), lambda i: (i, 0)),
        ],
        out_shape=[
            jax.ShapeDtypeStruct((t_all, d), BF16),
            jax.ShapeDtypeStruct((t_all, n_kv_cols), F32),
            jax.ShapeDtypeStruct((t_all, n_kv_cols), F32),
        ],
        compiler_params=_params(),
        name="qkv_project",
    )(y, mod, gain.reshape(1, d), w_qkv.astype(BF16), q_norm.reshape(1, HEAD_DIM),
      k_norm.reshape(1, HEAD_DIM), cos_t, sin_t)


def _attend(q, key_sets):
    n_heads = q.shape[1] // HEAD_DIM
    rep = n_heads // N_KV_HEADS
    scale = HEAD_DIM ** -0.5
    outs = []
    for hd in range(n_heads):
        g = hd // rep
        qh = q[:, hd * HEAD_DIM:(hd + 1) * HEAD_DIM]
        cols = slice(g * HEAD_DIM, (g + 1) * HEAD_DIM)
        scores = [_dot_nt(qh, k[:, cols]) * scale for k, _ in key_sets]
        m = functools.reduce(jnp.maximum, [jnp.max(s, axis=-1, keepdims=True) for s in scores])
        probs = [jnp.exp(s - m) for s in scores]
        denom = functools.reduce(jnp.add, [jnp.sum(p, axis=-1, keepdims=True) for p in probs])
        o = functools.reduce(jnp.add, [_dot(p.astype(BF16), v[:, cols])
                                       for p, (_, v) in zip(probs, key_sets)])
        outs.append(o / denom)
    return jnp.concatenate(outs, axis=1)


def _attn_ctx_kernel(y_ref, mod_ref, q_ref, k_ref, v_ref, wo_ref, o_ref):
    att = _attend(q_ref[...], [(k_ref[...].astype(BF16), v_ref[...].astype(BF16))])
    o_ref[...] = y_ref[...] + mod_ref[0, 2:3, :] * _dot(att.astype(BF16), wo_ref[...])


def _attn_lat_kernel(y_ref, mod_ref, q_ref, k_ref, v_ref, ck_ref, cv_ref, wo_ref, o_ref):
    sets = [(ck_ref[0].astype(BF16), cv_ref[0].astype(BF16)),
            (k_ref[...].astype(BF16), v_ref[...].astype(BF16))]
    att = _attend(q_ref[...], sets)
    o_ref[...] = y_ref[...] + mod_ref[0, 2:3, :] * _dot(att.astype(BF16), wo_ref[...])


def _attention(y, mod, q, k, v, cache_k, cache_v, w_o, n_ctx, ctx_len, n_lat, lat_len):
    t_all, d = y.shape
    kvc = k.shape[1]
    tp = n_ctx * ctx_len
    wo = w_o.astype(BF16)
    y = pl.pallas_call(
        _attn_ctx_kernel,
        grid=(n_ctx,),
        in_specs=[
            pl.BlockSpec((ctx_len, d), lambda b: (b, 0)),
            pl.BlockSpec((1, 6, d), lambda b: (0, 0, 0)),
            pl.BlockSpec((ctx_len, d), lambda b: (b, 0)),
            pl.BlockSpec((ctx_len, kvc), lambda b: (b, 0)),
            pl.BlockSpec((ctx_len, kvc), lambda b: (b, 0)),
            pl.BlockSpec((d, d), lambda b: (0, 0)),
        ],
        out_specs=pl.BlockSpec((ctx_len, d), lambda b: (b, 0)),
        out_shape=jax.ShapeDtypeStruct((t_all, d), F32),
        input_output_aliases={0: 0},
        compiler_params=_params(),
        name="attn_context",
    )(y, mod, q, k, v, wo)
    tq = TM
    qb = lat_len // tq
    past = cache_k.shape[1]
    y = pl.pallas_call(
        _attn_lat_kernel,
        grid=(n_lat, qb),
        in_specs=[
            pl.BlockSpec((tq, d), lambda b, i: (tp // tq + b * qb + i, 0)),
            pl.BlockSpec((1, 6, d), lambda b, i: (1 + b, 0, 0)),
            pl.BlockSpec((tq, d), lambda b, i: (tp // tq + b * qb + i, 0)),
            pl.BlockSpec((lat_len, kvc), lambda b, i: (tp // lat_len + b, 0)),
            pl.BlockSpec((lat_len, kvc), lambda b, i: (tp // lat_len + b, 0)),
            pl.BlockSpec((1, past, kvc), lambda b, i: (b, 0, 0)),
            pl.BlockSpec((1, past, kvc), lambda b, i: (b, 0, 0)),
            pl.BlockSpec((d, d), lambda b, i: (0, 0)),
        ],
        out_specs=pl.BlockSpec((tq, d), lambda b, i: (tp // tq + b * qb + i, 0)),
        out_shape=jax.ShapeDtypeStruct((t_all, d), F32),
        input_output_aliases={0: 0},
        compiler_params=_params(2),
        name="attn_latent",
    )(y, mod, q, k, v, cache_k, cache_v, wo)
    return y


def _pool_kernel(y_ref, mod_ref, gain_ref, wg_ref, scale_ref, o_ref):
    y = y_ref[...]
    s_len, d = y.shape
    gd = d // len(POOL_WINDOWS)
    h = _norm_mod(y, gain_ref[...], mod_ref[0, 0:1, :], mod_ref[0, 1:2, :])
    t_idx = lax.broadcasted_iota(jnp.int32, (s_len, s_len), 0)
    s_idx = lax.broadcasted_iota(jnp.int32, (s_len, s_len), 1)
    t_col = lax.broadcasted_iota(jnp.int32, (s_len, 1), 0)
    parts = []
    for g, w in enumerate(POOL_WINDOWS):
        lo = t_idx - w // 2
        window = ((s_idx >= lo) & (s_idx < lo + w)).astype(BF16)
        count = (jnp.minimum(t_col - w // 2 + w, s_len) - jnp.maximum(t_col - w // 2, 0)).astype(F32)
        hg = h[:, g * gd:(g + 1) * gd]
        h_hi = hg.astype(BF16)
        h_lo = (hg - h_hi.astype(F32)).astype(BF16)
        mean = (_dot(window, h_hi) + _dot(window, h_lo)) / count
        parts.append(_dot((mean - hg).astype(BF16), wg_ref[g]))
    mixed = jnp.concatenate(parts, axis=1) * scale_ref[...]
    o_ref[...] = y + mod_ref[0, 2:3, :] * mixed


def _pool(y, mod, gain, w_grp, scale, n_seq, s_len, row_off, mod_off):
    t_all, d = y.shape
    blk_off = row_off // s_len
    return pl.pallas_call(
        _pool_kernel,
        grid=(n_seq,),
        in_specs=[
            pl.BlockSpec((s_len, d), lambda b: (blk_off + b, 0)),
            pl.BlockSpec((1, 6, d), lambda b: (mod_off(b), 0, 0)),
            pl.BlockSpec((1, d), lambda b: (0, 0)),
            pl.BlockSpec(w_grp.shape, lambda b: (0, 0, 0)),
            pl.BlockSpec((1, d), lambda b: (0, 0)),
        ],
        out_specs=pl.BlockSpec((s_len, d), lambda b: (blk_off + b, 0)),
        out_shape=jax.ShapeDtypeStruct((t_all, d), F32),
        input_output_aliases={0: 0},
        compiler_params=_params(),
        name="pool_mixer",
    )(y, mod, gain.reshape(1, d), w_grp.astype(BF16), scale.reshape(1, d))


def _sorted_rows(n_exp):
    worst = TM * TOP_K + n_exp * (UNIT - 1) + UNIT
    return -(-worst // LANES) * LANES


def _route_kernel(y_ref, mod_ref, gain_ref, rw_ref, rb_ref, xs_ref, route_ref, cnt_ref):
    y = y_ref[...]
    tm = y.shape[0]
    n_rows = xs_ref.shape[0]
    h = _norm_mod(y, gain_ref[...], mod_ref[0, 3:4, :], mod_ref[0, 4:5, :])
    h_hi = h.astype(BF16)
    h_lo = (h - h_hi.astype(F32)).astype(BF16)
    hi_terms = _dot(h_hi, rw_ref[...])
    logits = hi_terms[:, :LANES] + hi_terms[:, LANES:] + _dot(h_lo, rw_ref[:, :LANES]) + rb_ref[...]
    lane = lax.broadcasted_iota(jnp.int32, logits.shape, 1).astype(F32)
    work = logits
    vals, ids = [], []
    for _ in range(TOP_K):
        m = jnp.max(work, axis=-1, keepdims=True)
        idx = jnp.min(jnp.where(work == m, lane, float(LANES)), axis=-1, keepdims=True)
        vals.append(m)
        ids.append(idx)
        work = jnp.where(lane == idx, -jnp.inf, work)
    exps = [jnp.exp(v - vals[0]) for v in vals]
    denom = functools.reduce(jnp.add, exps)
    onehot = functools.reduce(jnp.add, [(lane == idx).astype(F32) for idx in ids])

    counts = jnp.sum(onehot, axis=0, keepdims=True)
    seg = jnp.ceil(counts * (1.0 / UNIT)) * UNIT
    e_row = lax.broadcasted_iota(jnp.int32, (LANES, LANES), 0)
    e_col = lax.broadcasted_iota(jnp.int32, (LANES, LANES), 1)
    earlier = (e_row < e_col).astype(BF16)
    seg_off = _dot(jnp.broadcast_to(seg, (SUBLANES, LANES)).astype(BF16), earlier)[0:1, :]
    row = lax.broadcasted_iota(jnp.int32, (tm, tm), 0)
    col = lax.broadcasted_iota(jnp.int32, (tm, tm), 1)
    before = (col < row).astype(BF16)
    pos_mat = _dot(before, onehot.astype(BF16)) + seg_off

    route = jnp.zeros(logits.shape, F32)
    for k in range(TOP_K):
        pos_k = jnp.sum(jnp.where(lane == ids[k], pos_mat, 0.0), axis=-1, keepdims=True)
        route = jnp.where(lane == k, exps[k] / denom, route)
        route = jnp.where(lane == TOP_K + k, ids[k], route)
        route = jnp.where(lane == 2 * TOP_K + k, pos_k, route)
    route_ref[...] = route
    cnt_ref[...] = jnp.broadcast_to(counts, cnt_ref.shape)

    route_t = jnp.transpose(route)
    out_row = lax.broadcasted_iota(jnp.int32, (n_rows, tm), 0).astype(F32)
    perm = functools.reduce(jnp.add, [(out_row == route_t[2 * TOP_K + k:2 * TOP_K + k + 1, :]).astype(F32)
                                      for k in range(TOP_K)])
    xs_ref[...] = _dot(perm.astype(BF16), h_hi)


def _route(y, mod, gain, router_w, router_b, mod_row):
    t_all, d = y.shape
    n_exp = router_w.shape[1]
    n_tiles = t_all // TM
    n_rows = _sorted_rows(n_exp)
    rw = jnp.pad(router_w, ((0, 0), (0, LANES - n_exp)))
    rw_hi = rw.astype(BF16)
    rw = jnp.concatenate([rw_hi, (rw - rw_hi.astype(F32)).astype(BF16)], axis=1)
    rb = jnp.pad(router_b, (0, LANES - n_exp), constant_values=-1e30).reshape(1, LANES)
    return pl.pallas_call(
        _route_kernel,
        grid=(n_tiles,),
        in_specs=[
            pl.BlockSpec((TM, d), lambda i: (i, 0)),
            pl.BlockSpec((1, 6, d), lambda i: (mod_row(i), 0, 0)),
            pl.BlockSpec((1, d), lambda i: (0, 0)),
            pl.BlockSpec((d, 2 * LANES), lambda i: (0, 0)),
            pl.BlockSpec((1, LANES), lambda i: (0, 0)),
        ],
        out_specs=[
            pl.BlockSpec((n_rows, d), lambda i: (i, 0)),
            pl.BlockSpec((TM, LANES), lambda i: (i, 0)),
            pl.BlockSpec((SUBLANES, LANES), lambda i: (i, 0)),
        ],
        out_shape=[
            jax.ShapeDtypeStruct((n_tiles * n_rows + 2 * MOE_ROWS, d), F32),
            jax.ShapeDtypeStruct((t_all, LANES), F32),
            jax.ShapeDtypeStruct((n_tiles * SUBLANES, LANES), F32),
        ],
        compiler_params=_params(),
        name="moe_route",
    )(y, mod, gain.reshape(1, d), rw, rb)


def _plan_kernel(n_tiles, n_exp, n_rows, c_ref, srcx_ref, dstx_ref, be_ref, nxt_ref, nu_ref):
    nb = srcx_ref.shape[0]
    seg = jnp.ceil(c_ref[...] * (1.0 / UNIT)) * UNIT
    seg_b = seg.astype(BF16)
    sq_r = lax.broadcasted_iota(jnp.int32, (LANES, LANES), 0)
    sq_c = lax.broadcasted_iota(jnp.int32, (LANES, LANES), 1)
    upto = (sq_r <= sq_c).astype(BF16)
    seg_end_t = _dot(seg_b, upto)
    seg_off_t = seg_end_t - seg
    seg_end_e = _dot((sq_c <= sq_r).astype(BF16), seg_b)
    seg_off_e = seg_end_e - seg
    rows_e = seg_end_e[n_tiles - 1:n_tiles, :]
    nblk = jnp.ceil(rows_e * (1.0 / MOE_ROWS))
    end_blk = _dot(jnp.broadcast_to(nblk, (SUBLANES, LANES)).astype(BF16), upto)[0:1, :]
    start_blk = end_blk - nblk
    n_used = jnp.max(end_blk, axis=-1, keepdims=True)

    lane1 = lax.broadcasted_iota(jnp.int32, (1, LANES), 1).astype(F32)
    lane = lax.broadcasted_iota(jnp.int32, (nb, LANES), 1).astype(F32)
    blk = lax.broadcasted_iota(jnp.int32, (nb, LANES), 0).astype(F32)
    is_exp = lane1 < n_exp
    has_rows = (nblk > 0) & is_exp
    be = jnp.minimum(jnp.sum(((end_blk <= blk) & is_exp).astype(F32), axis=-1, keepdims=True), n_exp - 1.0)
    last_e = jnp.max(jnp.where(has_rows, lane1, 0.0), axis=-1, keepdims=True)
    used = blk[:, 0:1] < n_used
    be = jnp.where(used, be, last_e)
    nxt = jnp.min(jnp.where((lane > be) & has_rows, lane, float(LANES)), axis=-1, keepdims=True)
    nxt = jnp.where(nxt == LANES, be, nxt)
    onehot = (lane == be).astype(F32)

    def column_of_block(x):
        hi = jnp.floor(x * (1.0 / 256.0))
        lo = x - hi * 256.0
        oh = onehot.astype(BF16)
        return _dot_nt(oh, hi.astype(BF16)) * 256.0 + _dot_nt(oh, lo.astype(BF16))

    ends_b = column_of_block(seg_end_e)
    shift_b = column_of_block(seg_off_t) - column_of_block(seg_off_e)
    start_b = jnp.sum(onehot * start_blk, axis=-1, keepdims=True)
    rows_b = jnp.sum(onehot * rows_e, axis=-1, keepdims=True)
    r = (blk - start_b) * MOE_ROWS + lane * UNIT
    tile_of = jnp.zeros((nb, LANES), F32)
    for t in range(n_tiles):
        tile_of = tile_of + (ends_b[:, t:t + 1] <= r).astype(F32)
    tile_c = jnp.minimum(tile_of, n_tiles - 1.0)
    src = jnp.zeros((nb, LANES), F32)
    for t in range(n_tiles):
        src = src + jnp.where(tile_c == t, shift_b[:, t:t + 1] + float(t * n_rows), 0.0)
    valid = (r < rows_b) & used & (lane < MOE_ROWS // UNIT)
    srcx_ref[...] = jnp.where(valid, src + r, float(n_rows - UNIT)).astype(jnp.int32)
    spare = float(n_tiles * n_rows) + (blk - 2.0 * jnp.floor(blk * 0.5)) * MOE_ROWS + lane * UNIT
    dstx_ref[...] = jnp.where(valid, src + r, spare).astype(jnp.int32)

    rows_here = jnp.clip(rows_b - (blk[:, 0:1] - start_b) * MOE_ROWS, 0.0, float(MOE_ROWS))
    be_ref[...] = jnp.where(lane == 1, rows_here, be).astype(jnp.int32)
    nxt_ref[...] = jnp.broadcast_to(nxt, nxt_ref.shape).astype(jnp.int32)
    nu_ref[...] = jnp.broadcast_to(n_used, nu_ref.shape).astype(jnp.int32)


def _plan(counts, n_tiles, n_exp, n_rows, n_blocks):
    assert n_tiles <= LANES and n_exp <= LANES and n_tiles % SUBLANES == 0
    blk_units = MOE_ROWS // UNIT
    assert blk_units <= LANES
    nb = -(-n_blocks // SUBLANES) * SUBLANES
    srcx, dstx, be, nxt, nu = pl.pallas_call(
        functools.partial(_plan_kernel, n_tiles, n_exp, n_rows),
        out_shape=[
            jax.ShapeDtypeStruct((nb, LANES), jnp.int32),
            jax.ShapeDtypeStruct((nb, LANES), jnp.int32),
            jax.ShapeDtypeStruct((nb, LANES), jnp.int32),
            jax.ShapeDtypeStruct((nb, LANES), jnp.int32),
            jax.ShapeDtypeStruct((SUBLANES, LANES), jnp.int32),
        ],
        compiler_params=pltpu.CompilerParams(vmem_limit_bytes=VMEM_LIMIT_BYTES),
        name="moe_plan",
    )(counts)
    before = n_tiles * n_rows + MOE_ROWS + UNIT * jnp.arange(blk_units, dtype=jnp.int32)
    dst = jnp.concatenate([before, dstx[:n_blocks, :blk_units].reshape(-1)])
    return (srcx[:n_blocks, :blk_units].reshape(-1), dst, be[:n_blocks, 0], be[:n_blocks, 1],
            nxt[:n_blocks, 0], nu[0, :1])


def _gather_units(src_ref, base, table_ref, dst, sem):
    for u in range(dst.shape[0] // UNIT):
        start = pl.multiple_of(table_ref[base + u], UNIT)
        pltpu.make_async_copy(src_ref.at[pl.ds(start, UNIT)], dst.at[pl.ds(u * UNIT, UNIT)], sem).start()


def _scatter_units(src, base, table_ref, dst_ref, sem):
    for u in range(src.shape[0] // UNIT):
        start = pl.multiple_of(table_ref[base + u], UNIT)
        pltpu.make_async_copy(src.at[pl.ds(u * UNIT, UNIT)], dst_ref.at[pl.ds(start, UNIT)], sem).start()


def _wait_units(buf, sem):
    pltpu.make_async_copy(buf, buf, sem).wait()


def _expert_kernel(layer, be_ref, rows_ref, nu_ref, src_ref, dst_ref, nxt_ref, xs_ref, wgu_hbm, bgu_ref, wdn_hbm,
                   bdn_ref, ys_ref, wgu_st, wdn_st, wgu_bf, wdn_bf, x0, x1, y0, y1, gsem, ssem, wsem):
    i = pl.program_id(0)
    n_used = nu_ref[0]
    n_blocks = pl.num_programs(0)
    units = x0.shape[0] // UNIT
    d_exp = wdn_bf.shape[0]

    def weight_copies(e):
        return (pltpu.make_async_copy(wgu_hbm.at[layer, e], wgu_st, wsem.at[0]),
                pltpu.make_async_copy(wdn_hbm.at[layer, e], wdn_st, wsem.at[1]))

    def run_block(xcur, xoth, ycur, yoth, p, m):
        _wait_units(xcur, gsem.at[p])

        @pl.when(i >= 1)
        def _():
            _wait_units(ycur, ssem.at[p])

        _gather_units(xs_ref, jnp.minimum(i + 1, n_blocks - 1) * units, src_ref, xoth, gsem.at[1 - p])
        _scatter_units(yoth, i * units, dst_ref, ys_ref, ssem.at[1 - p])
        x = xcur[0:m, :].astype(BF16)
        gu = _dot(x, wgu_bf[...]) + bgu_ref[0, 0]
        gate = jnp.minimum(gu[:, :d_exp], SWIGLU_LIMIT)
        up = jnp.clip(gu[:, d_exp:], -SWIGLU_LIMIT, SWIGLU_LIMIT)
        hid = (up + 1.0) * gate * jax.nn.sigmoid(SWIGLU_ALPHA * gate)
        ycur[0:m, :] = _dot(hid.astype(BF16), wdn_bf[...]) + bdn_ref[0, 0]

        @pl.when(i == n_used - 1)
        def _():
            _wait_units(xoth, gsem.at[1 - p])
            _scatter_units(ycur, (i + 1) * units, dst_ref, ys_ref, ssem.at[p])
            _wait_units(yoth, ssem.at[1 - p])
            _wait_units(ycur, ssem.at[p])

    @pl.when(i < n_used)
    def _():
        @pl.when(i == 0)
        def _():
            for cp in weight_copies(be_ref[0]):
                cp.start(priority=1)
            _gather_units(xs_ref, 0, src_ref, x0, gsem.at[0])
            y0[...] = jnp.zeros_like(y0)
            y1[...] = jnp.zeros_like(y1)

        new_expert = jnp.logical_or(i == 0, be_ref[i] != be_ref[jnp.maximum(i - 1, 0)])

        @pl.when(new_expert)
        def _():
            for cp in weight_copies(be_ref[i]):
                cp.wait()
            wgu_bf[...] = wgu_st[...].astype(BF16)
            wdn_bf[...] = wdn_st[...].astype(BF16)

            @pl.when(nxt_ref[i] != be_ref[i])
            def _():
                for cp in weight_copies(nxt_ref[i]):
                    cp.start(priority=1)

        n_gran = x0.shape[0] // MOE_GRANULE
        granules = jnp.clip((rows_ref[i] + MOE_GRANULE - 1) // MOE_GRANULE, 1, n_gran)
        for g in range(1, n_gran + 1):
            @pl.when(jnp.logical_and(i % 2 == 0, granules == g))
            def _():
                run_block(x0, x1, y0, y1, 0, g * MOE_GRANULE)

            @pl.when(jnp.logical_and(i % 2 == 1, granules == g))
            def _():
                run_block(x1, x0, y1, y0, 1, g * MOE_GRANULE)


def _experts(layer, xs, src_units, dst_units, block_e, block_rows, next_e, n_used, w_gu, b_gu, w_dn, b_dn,
             n_blocks):
    d = w_dn.shape[-1]
    depth, n_exp, _, two_f = w_gu.shape
    d_exp = two_f // 2
    n_prefetch = 6

    def bias_block(i, be, rows, nu, src, dst, nxt):
        return (layer, be[i], 0, 0)

    grid_spec = pltpu.PrefetchScalarGridSpec(
        num_scalar_prefetch=n_prefetch,
        grid=(n_blocks,),
        in_specs=[
            pl.BlockSpec(memory_space=pl.ANY),
            pl.BlockSpec(memory_space=pl.ANY),
            pl.BlockSpec((1, 1, 1, two_f), bias_block),
            pl.BlockSpec(memory_space=pl.ANY),
            pl.BlockSpec((1, 1, 1, d), bias_block),
        ],
        out_specs=pl.BlockSpec(memory_space=pl.ANY),
        scratch_shapes=[
            pltpu.VMEM((d, two_f), F32),
            pltpu.VMEM((d_exp, d), F32),
            pltpu.VMEM((d, two_f), BF16),
            pltpu.VMEM((d_exp, d), BF16),
            pltpu.VMEM((MOE_ROWS, d), F32),
            pltpu.VMEM((MOE_ROWS, d), F32),
            pltpu.VMEM((MOE_ROWS, d), F32),
            pltpu.VMEM((MOE_ROWS, d), F32),
            pltpu.SemaphoreType.DMA((2,)),
            pltpu.SemaphoreType.DMA((2,)),
            pltpu.SemaphoreType.DMA((2,)),
        ],
    )
    return pl.pallas_call(
        functools.partial(_expert_kernel, layer),
        grid_spec=grid_spec,
        out_shape=jax.ShapeDtypeStruct(xs.shape, xs.dtype),
        input_output_aliases={n_prefetch: 0},
        compiler_params=_params(),
        name="moe_experts",
    )(block_e, block_rows, n_used, src_units, dst_units, next_e, xs, w_gu, b_gu.reshape(depth, n_exp, 1, two_f),
      w_dn, b_dn.reshape(depth, n_exp, 1, d))


def _combine_kernel(n_first, y_ref, mod_ref, route_ref, ys_ref, *o_refs):
    route = route_ref[...]
    tm = route.shape[0]
    n_rows = ys_ref.shape[0]
    col = lax.broadcasted_iota(jnp.int32, (tm, n_rows), 1).astype(F32)
    weights = functools.reduce(jnp.add, [
        jnp.where(col == route[:, 2 * TOP_K + k:2 * TOP_K + k + 1], route[:, k:k + 1], 0.0)
        for k in range(TOP_K)])
    acc = _dot(weights.astype(BF16), ys_ref[...].astype(BF16))
    out = y_ref[...] + mod_ref[0, 5:6, :] * acc
    if n_first is None:
        o_refs[0][...] = out
    else:
        first = pl.program_id(0) < n_first

        @pl.when(first)
        def _():
            o_refs[0][...] = out

        @pl.when(jnp.logical_not(first))
        def _():
            o_refs[1][...] = out


def _combine(y, mod, route, ys, n_rows, mod_row, split_rows=None):
    t_all, d = y.shape
    if split_rows is None:
        n_first = None
        out_specs = pl.BlockSpec((TM, d), lambda i: (i, 0))
        out_shape = jax.ShapeDtypeStruct((t_all, d), F32)
    else:
        n_first = split_rows // TM
        out_specs = [pl.BlockSpec((TM, d), lambda i: (jnp.minimum(i, n_first - 1), 0)),
                     pl.BlockSpec((TM, d), lambda i: (jnp.maximum(i - n_first, 0), 0))]
        out_shape = [jax.ShapeDtypeStruct((split_rows, d), F32),
                     jax.ShapeDtypeStruct((t_all - split_rows, d), F32)]
    return pl.pallas_call(
        functools.partial(_combine_kernel, n_first),
        grid=(t_all // TM,),
        in_specs=[
            pl.BlockSpec((TM, d), lambda i: (i, 0)),
            pl.BlockSpec((1, 6, d), lambda i: (mod_row(i), 0, 0)),
            pl.BlockSpec((TM, LANES), lambda i: (i, 0)),
            pl.BlockSpec((n_rows, d), lambda i: (i, 0)),
        ],
        out_specs=out_specs,
        out_shape=out_shape,
        compiler_params=_params(),
        name="moe_combine",
    )(y, mod, route, ys)


def _moe(layer, y, mod, gain, router_w, router_b, w_gu, b_gu, w_dn, b_dn, mod_row, split_rows=None):
    t_all, d = y.shape
    n_exp = router_w.shape[1]
    n_tiles = t_all // TM
    n_rows = _sorted_rows(n_exp)
    n_blocks = -(-(t_all * TOP_K + n_tiles * n_exp * (UNIT - 1)) // MOE_ROWS) + n_exp

    xs, route, cnt = _route(y, mod, gain, router_w, router_b, mod_row)
    counts = jnp.pad(cnt.reshape(n_tiles, SUBLANES, LANES)[:, 0, :], ((0, LANES - n_tiles), (0, 0)))
    src_x, dst_x, block_e, block_rows, next_e, n_used = _plan(counts, n_tiles, n_exp, n_rows, n_blocks)
    ys = _experts(layer, xs, src_x, dst_x, block_e, block_rows, next_e, n_used, w_gu, b_gu, w_dn, b_dn, n_blocks)
    return _combine(y, mod, route, ys, n_rows, mod_row, split_rows)


def kernel(x_prompt, x_sample, cache_k, cache_v, c, c_ctx, w_mod, b_mod, norm_mix, norm_ffn, gm_w_in, gm_norm_v, gm_w_s, gm_b_s, gm_w_out, at_w_qkv, at_q_norm, at_k_norm, at_w_o, pool_w_grp, pool_scale, router_w, router_b, w_gate_up, b_gate_up, w_down, b_down):
    n_ctx, ctx_len, d = x_prompt.shape
    n_lat, lat_len, _ = x_sample.shape
    depth = w_mod.shape[0]
    tp = n_ctx * ctx_len
    assert tp % lat_len == 0 and ctx_len % TM == 0 and lat_len % TM == 0 and ctx_len == TM
    assert 1 + n_lat <= SUBLANES

    assert depth >= 1
    y = (x_prompt.reshape(tp, d), x_sample.reshape(n_lat * lat_len, d))

    cvecs = jnp.zeros((SUBLANES, d), F32).at[0].set(c_ctx).at[1:1 + n_lat].set(c)
    mod_all = jnp.transpose(_ada_params(cvecs, w_mod, b_mod), (0, 2, 1, 3))

    def mod_row(i):
        return jnp.where(i * TM < tp, 0, 1 + (i * TM - tp) // lat_len)

    half = HEAD_DIM // 4
    inv = ROPE_THETA ** (-jnp.arange(half, dtype=F32) / half)
    pos = jnp.arange(lat_len)
    ang_r = (pos // GRID_W).astype(F32)[:, None] * inv[None, :]
    ang_c = (pos % GRID_W).astype(F32)[:, None] * inv[None, :]
    cos_t = jnp.concatenate([jnp.ones((TM, HEAD_DIM), F32),
                             jnp.concatenate([jnp.cos(ang_r)] * 2 + [jnp.cos(ang_c)] * 2, axis=1)], axis=0)
    sin_t = jnp.concatenate([jnp.zeros((TM, HEAD_DIM), F32),
                             jnp.concatenate([-jnp.sin(ang_r), jnp.sin(ang_r),
                                              -jnp.sin(ang_c), jnp.sin(ang_c)], axis=1)], axis=0)

    def rope_row(i):
        return jnp.where(i * TM < tp, 0, 1 + ((i * TM - tp) % lat_len) // TM)

    new_k, new_v = [], []
    for l in range(depth):
        kind, j = l % 3, l // 3
        mod = mod_all[l]
        if kind == 0:
            y = _gmlp(y, mod, norm_mix[l], gm_w_in[j], gm_norm_v[j], gm_w_s[j], gm_b_s[j], gm_w_out[j], mod_row)
        elif kind == 1:
            q, k, v = _qkv(y, mod, norm_mix[l], at_w_qkv[j], at_q_norm[j], at_k_norm[j],
                           cos_t, sin_t, mod_row, rope_row)
            new_k.append(k[:tp].reshape(n_ctx, ctx_len, N_KV_HEADS, HEAD_DIM))
            new_v.append(v[:tp].reshape(n_ctx, ctx_len, N_KV_HEADS, HEAD_DIM))
            past = cache_k.shape[2]
            ck = cache_k[:, j].reshape(n_lat, past, N_KV_HEADS * HEAD_DIM)
            cv = cache_v[:, j].reshape(n_lat, past, N_KV_HEADS * HEAD_DIM)
            y = _attention(y, mod, q, k, v, ck, cv, at_w_o[j], n_ctx, ctx_len, n_lat, lat_len)
        else:
            y = _pool(y, mod, norm_mix[l], pool_w_grp[j], pool_scale[j], n_ctx, ctx_len, 0, lambda b: 0)
            y = _pool(y, mod, norm_mix[l], pool_w_grp[j], pool_scale[j], n_lat, lat_len, tp, lambda b: 1 + b)
        y = _moe(l, y, mod, norm_ffn[l], router_w[l], router_b[l], w_gate_up, b_gate_up,
                 w_down, b_down, mod_row, split_rows=tp if l == depth - 1 else None)

    y_ctx, y_lat = y
    return (y_ctx.reshape(n_ctx, ctx_len, d), y_lat.reshape(n_lat, lat_len, d),
            jnp.stack(new_k, axis=1), jnp.stack(new_v, axis=1))
```

```python
import functools

import jax
import jax.numpy as jnp
from jax import lax
from jax.experimental import pallas as pl
from jax.experimental.pallas import tpu as pltpu

F32 = jnp.float32
BF16 = jnp.bfloat16

LANES = 128
SUBLANES = 8
VMEM_LIMIT_BYTES = 56 * 1024 * 1024

EPS = 1e-6
GRID_W = 64
CHUNK = 128
HEAD_DIM = 128
N_KV_HEADS = 2
ROPE_THETA = 10000.0
POOL_WINDOWS = (2, 4, 8, 16)
TOP_K = 4
SWIGLU_ALPHA = 1.702
SWIGLU_LIMIT = 7.0

TM = 256
MOE_ROWS = 512
MOE_GRANULE = 128
ROUTE_TILES = 2
UNIT = SUBLANES


def _params(n_axes=1):
    return pltpu.CompilerParams(dimension_semantics=("arbitrary",) * n_axes,
                                vmem_limit_bytes=VMEM_LIMIT_BYTES)


def _rms(x, gain):
    return x * lax.rsqrt(jnp.mean(x * x, axis=-1, keepdims=True) + EPS) * gain


def _norm_mod(y, gain, shift, scale):
    return _rms(y, gain) * (1.0 + scale) + shift


def _gelu_tanh(x):
    return 0.5 * x * (1.0 + jnp.tanh(0.7978845608028654 * (x + 0.044715 * (x * x * x))))


def _dot(a, b):
    return jnp.dot(a, b, preferred_element_type=F32)


def _dot_nt(a, b):
    return lax.dot_general(a, b, (((1,), (1,)), ((), ())), preferred_element_type=F32)


def _ada_kernel(cv_ref, w_ref, b_ref, o_ref):
    cv = cv_ref[...]
    s = cv * jax.nn.sigmoid(cv)
    w = w_ref[0]
    s_hi = s.astype(BF16)
    s_lo = (s - s_hi.astype(F32)).astype(BF16)
    w_hi = w.astype(BF16)
    w_lo = (w - w_hi.astype(F32)).astype(BF16)
    o_ref[0, 0] = _dot(s_hi, w_hi) + _dot(s_lo, w_hi) + _dot(s_hi, w_lo) + b_ref[0, 0]


def _ada_params(cvecs, w_mod, b_mod):
    depth, d, _ = w_mod.shape
    rows = cvecs.shape[0]
    return pl.pallas_call(
        _ada_kernel,
        grid=(depth, 6),
        in_specs=[
            pl.BlockSpec((rows, d), lambda l, j: (0, 0)),
            pl.BlockSpec((1, d, d), lambda l, j: (l, 0, j)),
            pl.BlockSpec((1, 1, 1, d), lambda l, j: (l, j, 0, 0)),
        ],
        out_specs=pl.BlockSpec((1, 1, rows, d), lambda l, j: (l, j, 0, 0)),
        out_shape=jax.ShapeDtypeStruct((depth, 6, rows, d), F32),
        compiler_params=_params(2),
        name="ada_params",
    )(cvecs, w_mod, b_mod.reshape(depth, 6, 1, d))


def _gmlp_kernel(n_first, *refs):
    if n_first is None:
        y = refs[0][...]
        refs = refs[1:]
    else:
        y = jnp.where(pl.program_id(0) < n_first, refs[0][...], refs[1][...])
        refs = refs[2:]
    mod_ref, gain_ref, win_ref, nv_ref, ws_ref, bexp_ref, wout_ref, o_ref = refs
    inner = nv_ref.shape[-1]
    groups = ws_ref.shape[0]
    h = _norm_mod(y, gain_ref[...], mod_ref[0, 0:1, :], mod_ref[0, 1:2, :])
    z = _gelu_tanh(_dot(h.astype(BF16), win_ref[...]))
    u = z[:, :inner]
    v = _rms(z[:, inner:], nv_ref[...]).astype(BF16)
    bexp = bexp_ref[...]
    chunks = []
    for c in range(y.shape[0] // CHUNK):
        cols = []
        for g in range(groups):
            vg = v[c * CHUNK:(c + 1) * CHUNK, g * LANES:(g + 1) * LANES]
            cols.append(_dot(ws_ref[g], vg))
        chunks.append(jnp.concatenate(cols, axis=1) + bexp)
    mixed = jnp.concatenate(chunks, axis=0)
    o = _dot((u * mixed).astype(BF16), wout_ref[...])
    o_ref[...] = y + mod_ref[0, 2:3, :] * o


def _gmlp(y, mod, gain, w_in, norm_v, w_s, b_s, w_out, mod_row):
    if isinstance(y, tuple):
        n_first = y[0].shape[0] // TM
        t_all, d = y[0].shape[0] + y[1].shape[0], y[0].shape[1]
        y_specs = [pl.BlockSpec((TM, d), lambda i: (jnp.minimum(i, n_first - 1), 0)),
                   pl.BlockSpec((TM, d), lambda i: (jnp.maximum(i - n_first, 0), 0))]
    else:
        n_first, y = None, (y,)
        t_all, d = y[0].shape
        y_specs = [pl.BlockSpec((TM, d), lambda i: (i, 0))]
    inner = norm_v.shape[-1]
    groups = w_s.shape[0]
    bexp = jnp.repeat(b_s.T, inner // groups, axis=1)
    return pl.pallas_call(
        functools.partial(_gmlp_kernel, n_first),
        grid=(t_all // TM,),
        in_specs=y_specs + [
            pl.BlockSpec((1, 6, d), lambda i: (mod_row(i), 0, 0)),
            pl.BlockSpec((1, d), lambda i: (0, 0)),
            pl.BlockSpec((d, 2 * inner), lambda i: (0, 0)),
            pl.BlockSpec((1, inner), lambda i: (0, 0)),
            pl.BlockSpec((groups, CHUNK, CHUNK), lambda i: (0, 0, 0)),
            pl.BlockSpec((CHUNK, inner), lambda i: (0, 0)),
            pl.BlockSpec((inner, d), lambda i: (0, 0)),
        ],
        out_specs=pl.BlockSpec((TM, d), lambda i: (i, 0)),
        out_shape=jax.ShapeDtypeStruct((t_all, d), F32),
        compiler_params=_params(),
        name="gmlp",
    )(*y, mod, gain.reshape(1, d), w_in.astype(BF16), norm_v.reshape(1, inner),
      w_s.astype(BF16), bexp, w_out.astype(BF16))


def _qkv_kernel(y_ref, mod_ref, gain_ref, w_ref, qn_ref, kn_ref, cos_ref, sin_ref,
                q_ref, k_ref, v_ref):
    y = y_ref[...]
    n_q = q_ref.shape[-1] // HEAD_DIM
    n_kv = k_ref.shape[-1] // HEAD_DIM
    h = _norm_mod(y, gain_ref[...], mod_ref[0, 0:1, :], mod_ref[0, 1:2, :])
    qkv = _dot(h.astype(BF16), w_ref[...])
    cos = cos_ref[...]
    sin = sin_ref[...]
    lane = lax.broadcasted_iota(jnp.int32, cos.shape, 1)
    low_half = (lane % (HEAD_DIM // 2)) < (HEAD_DIM // 4)

    def head(idx, norm):
        x = _rms(qkv[:, idx * HEAD_DIM:(idx + 1) * HEAD_DIM], norm)
        partner = jnp.where(low_half,
                            pltpu.roll(x, HEAD_DIM - HEAD_DIM // 4, 1),
                            pltpu.roll(x, HEAD_DIM // 4, 1))
        return x * cos + partner * sin

    qn = qn_ref[...]
    kn = kn_ref[...]
    q_ref[...] = jnp.concatenate([head(i, qn) for i in range(n_q)], axis=1).astype(q_ref.dtype)
    k_ref[...] = jnp.concatenate([head(n_q + i, kn) for i in range(n_kv)], axis=1)
    v_ref[...] = qkv[:, (n_q + n_kv) * HEAD_DIM:]


def _qkv(y, mod, gain, w_qkv, q_norm, k_norm, cos_t, sin_t, mod_row, rope_row):
    t_all, d = y.shape
    n_kv_cols = N_KV_HEADS * HEAD_DIM
    return pl.pallas_call(
        _qkv_kernel,
        grid=(t_all // TM,),
        in_specs=[
            pl.BlockSpec((TM, d), lambda i: (i, 0)),
            pl.BlockSpec((1, 6, d), lambda i: (mod_row(i), 0, 0)),
            pl.BlockSpec((1, d), lambda i: (0, 0)),
            pl.BlockSpec(w_qkv.shape, lambda i: (0, 0)),
            pl.BlockSpec((1, HEAD_DIM), lambda i: (0, 0)),
            pl.BlockSpec((1, HEAD_DIM), lambda i: (0, 0)),
            pl.BlockSpec((TM, HEAD_DIM), lambda i: (rope_row(i), 0)),
            pl.BlockSpec((TM, HEAD_DIM), lambda i: (rope_row(i), 0)),
        ],
        out_specs=[
            pl.BlockSpec((TM, d), lambda i: (i, 0)),
            pl.BlockSpec((TM, n_kv_cols), lambda i: (i, 0)),
            pl.BlockSpec((TM, n_kv_cols), lambda i: (i, 0)),
        ],
        out_shape=[
            jax.ShapeDtypeStruct((t_all, d), BF16),
            jax.ShapeDtypeStruct((t_all, n_kv_cols), F32),
            jax.ShapeDtypeStruct((t_all, n_kv_cols), F32),
        ],
        compiler_params=_params(),
        name="qkv_project",
    )(y, mod, gain.reshape(1, d), w_qkv.astype(BF16), q_norm.reshape(1, HEAD_DIM),
      k_norm.reshape(1, HEAD_DIM), cos_t, sin_t)


def _attend(q, key_sets):
    n_heads = q.shape[1] // HEAD_DIM
    rep = n_heads // N_KV_HEADS
    scale = HEAD_DIM ** -0.5
    outs = []
    for hd in range(n_heads):
        g = hd // rep
        qh = q[:, hd * HEAD_DIM:(hd + 1) * HEAD_DIM]
        cols = slice(g * HEAD_DIM, (g + 1) * HEAD_DIM)
        scores = [_dot_nt(qh, k[:, cols]) * scale for k, _ in key_sets]
        m = functools.reduce(jnp.maximum, [jnp.max(s, axis=-1, keepdims=True) for s in scores])
        probs = [jnp.exp(s - m) for s in scores]
        denom = functools.reduce(jnp.add, [jnp.sum(p, axis=-1, keepdims=True) for p in probs])
        o = functools.reduce(jnp.add, [_dot(p.astype(BF16), v[:, cols])
                                       for p, (_, v) in zip(probs, key_sets)])
        outs.append(o / denom)
    return jnp.concatenate(outs, axis=1)


def _attn_ctx_kernel(y_ref, mod_ref, q_ref, k_ref, v_ref, wo_ref, o_ref):
    att = _attend(q_ref[...], [(k_ref[...].astype(BF16), v_ref[...].astype(BF16))])
    o_ref[...] = y_ref[...] + mod_ref[0, 2:3, :] * _dot(att.astype(BF16), wo_ref[...])


def _attn_lat_kernel(y_ref, mod_ref, q_ref, k_ref, v_ref, ck_ref, cv_ref, wo_ref, o_ref):
    sets = [(ck_ref[0].astype(BF16), cv_ref[0].astype(BF16)),
            (k_ref[...].astype(BF16), v_ref[...].astype(BF16))]
    att = _attend(q_ref[...], sets)
    o_ref[...] = y_ref[...] + mod_ref[0, 2:3, :] * _dot(att.astype(BF16), wo_ref[...])


def _attention(y, mod, q, k, v, cache_k, cache_v, w_o, n_ctx, ctx_len, n_lat, lat_len):
    t_all, d = y.shape
    kvc = k.shape[1]
    tp = n_ctx * ctx_len
    wo = w_o.astype(BF16)
    y = pl.pallas_call(
        _attn_ctx_kernel,
        grid=(n_ctx,),
        in_specs=[
            pl.BlockSpec((ctx_len, d), lambda b: (b, 0)),
            pl.BlockSpec((1, 6, d), lambda b: (0, 0, 0)),
            pl.BlockSpec((ctx_len, d), lambda b: (b, 0)),
            pl.BlockSpec((ctx_len, kvc), lambda b: (b, 0)),
            pl.BlockSpec((ctx_len, kvc), lambda b: (b, 0)),
            pl.BlockSpec((d, d), lambda b: (0, 0)),
        ],
        out_specs=pl.BlockSpec((ctx_len, d), lambda b: (b, 0)),
        out_shape=jax.ShapeDtypeStruct((t_all, d), F32),
        input_output_aliases={0: 0},
        compiler_params=_params(),
        name="attn_context",
    )(y, mod, q, k, v, wo)
    tq = TM
    qb = lat_len // tq
    past = cache_k.shape[1]
    y = pl.pallas_call(
        _attn_lat_kernel,
        grid=(n_lat, qb),
        in_specs=[
            pl.BlockSpec((tq, d), lambda b, i: (tp // tq + b * qb + i, 0)),
            pl.BlockSpec((1, 6, d), lambda b, i: (1 + b, 0, 0)),
            pl.BlockSpec((tq, d), lambda b, i: (tp // tq + b * qb + i, 0)),
            pl.BlockSpec((lat_len, kvc), lambda b, i: (tp // lat_len + b, 0)),
            pl.BlockSpec((lat_len, kvc), lambda b, i: (tp // lat_len + b, 0)),
            pl.BlockSpec((1, past, kvc), lambda b, i: (b, 0, 0)),
            pl.BlockSpec((1, past, kvc), lambda b, i: (b, 0, 0)),
            pl.BlockSpec((d, d), lambda b, i: (0, 0)),
        ],
        out_specs=pl.BlockSpec((tq, d), lambda b, i: (tp // tq + b * qb + i, 0)),
        out_shape=jax.ShapeDtypeStruct((t_all, d), F32),
        input_output_aliases={0: 0},
        compiler_params=_params(2),
        name="attn_latent",
    )(y, mod, q, k, v, cache_k, cache_v, wo)
    return y


def _pool_kernel(y_ref, mod_ref, gain_ref, wg_ref, scale_ref, o_ref):
    y = y_ref[...]
    s_len, d = y.shape
    gd = d // len(POOL_WINDOWS)
    h = _norm_mod(y, gain_ref[...], mod_ref[0, 0:1, :], mod_ref[0, 1:2, :])
    t_idx = lax.broadcasted_iota(jnp.int32, (s_len, s_len), 0)
    s_idx = lax.broadcasted_iota(jnp.int32, (s_len, s_len), 1)
    t_col = lax.broadcasted_iota(jnp.int32, (s_len, 1), 0)
    parts = []
    for g, w in enumerate(POOL_WINDOWS):
        lo = t_idx - w // 2
        window = ((s_idx >= lo) & (s_idx < lo + w)).astype(BF16)
        count = (jnp.minimum(t_col - w // 2 + w, s_len) - jnp.maximum(t_col - w // 2, 0)).astype(F32)
        hg = h[:, g * gd:(g + 1) * gd]
        h_hi = hg.astype(BF16)
        h_lo = (hg - h_hi.astype(F32)).astype(BF16)
        mean = (_dot(window, h_hi) + _dot(window, h_lo)) / count
        parts.append(_dot((mean - hg).astype(BF16), wg_ref[g]))
    mixed = jnp.concatenate(parts, axis=1) * scale_ref[...]
    o_ref[...] = y + mod_ref[0, 2:3, :] * mixed


def _pool(y, mod, gain, w_grp, scale, n_seq, s_len, row_off, mod_off):
    t_all, d = y.shape
    blk_off = row_off // s_len
    return pl.pallas_call(
        _pool_kernel,
        grid=(n_seq,),
        in_specs=[
            pl.BlockSpec((s_len, d), lambda b: (blk_off + b, 0)),
            pl.BlockSpec((1, 6, d), lambda b: (mod_off(b), 0, 0)),
            pl.BlockSpec((1, d), lambda b: (0, 0)),
            pl.BlockSpec(w_grp.shape, lambda b: (0, 0, 0)),
            pl.BlockSpec((1, d), lambda b: (0, 0)),
        ],
        out_specs=pl.BlockSpec((s_len, d), lambda b: (blk_off + b, 0)),
        out_shape=jax.ShapeDtypeStruct((t_all, d), F32),
        input_output_aliases={0: 0},
        compiler_params=_params(),
        name="pool_mixer",
    )(y, mod, gain.reshape(1, d), w_grp.astype(BF16), scale.reshape(1, d))


def _sorted_rows(n_exp):
    worst = TM * TOP_K + n_exp * (UNIT - 1) + UNIT
    return -(-worst // LANES) * LANES


def _route_tile(y, mod_ref, gain_ref, rw_ref, rb_ref, n_rows):
    tm = y.shape[0]
    h = _norm_mod(y, gain_ref[...], mod_ref[0, 3:4, :], mod_ref[0, 4:5, :])
    h_hi = h.astype(BF16)
    h_lo = (h - h_hi.astype(F32)).astype(BF16)
    hi_terms = _dot(h_hi, rw_ref[...])
    logits = hi_terms[:, :LANES] + hi_terms[:, LANES:] + _dot(h_lo, rw_ref[:, :LANES]) + rb_ref[...]
    lane = lax.broadcasted_iota(jnp.int32, logits.shape, 1).astype(F32)
    work = logits
    vals, ids = [], []
    for _ in range(TOP_K):
        m = jnp.max(work, axis=-1, keepdims=True)
        idx = jnp.min(jnp.where(work == m, lane, float(LANES)), axis=-1, keepdims=True)
        vals.append(m)
        ids.append(idx)
        work = jnp.where(lane == idx, -jnp.inf, work)
    exps = [jnp.exp(v - vals[0]) for v in vals]
    denom = functools.reduce(jnp.add, exps)
    onehot = functools.reduce(jnp.add, [(lane == idx).astype(F32) for idx in ids])

    counts = jnp.sum(onehot, axis=0, keepdims=True)
    seg = jnp.ceil(counts * (1.0 / UNIT)) * UNIT
    e_row = lax.broadcasted_iota(jnp.int32, (LANES, LANES), 0)
    e_col = lax.broadcasted_iota(jnp.int32, (LANES, LANES), 1)
    earlier = (e_row < e_col).astype(BF16)
    seg_off = _dot(jnp.broadcast_to(seg, (SUBLANES, LANES)).astype(BF16), earlier)[0:1, :]
    row = lax.broadcasted_iota(jnp.int32, (tm, tm), 0)
    col = lax.broadcasted_iota(jnp.int32, (tm, tm), 1)
    before = (col < row).astype(BF16)
    pos_mat = _dot(before, onehot.astype(BF16)) + seg_off

    route = jnp.zeros(logits.shape, F32)
    for k in range(TOP_K):
        pos_k = jnp.sum(jnp.where(lane == ids[k], pos_mat, 0.0), axis=-1, keepdims=True)
        route = jnp.where(lane == k, exps[k] / denom, route)
        route = jnp.where(lane == TOP_K + k, ids[k], route)
        route = jnp.where(lane == 2 * TOP_K + k, pos_k, route)

    route_t = jnp.transpose(route)
    out_row = lax.broadcasted_iota(jnp.int32, (n_rows, tm), 0).astype(F32)
    perm = functools.reduce(jnp.add, [(out_row == route_t[2 * TOP_K + k:2 * TOP_K + k + 1, :]).astype(F32)
                                      for k in range(TOP_K)])
    return _dot(perm.astype(BF16), h_hi), route, counts


def _route_kernel(y_ref, mod_ref, gain_ref, rw_ref, rb_ref, xs_ref, route_ref, cnt_ref):
    n_rows = xs_ref.shape[0] // ROUTE_TILES
    for s in range(ROUTE_TILES):
        xs, route, counts = _route_tile(y_ref[s * TM:(s + 1) * TM, :], mod_ref, gain_ref, rw_ref, rb_ref, n_rows)
        xs_ref[s * n_rows:(s + 1) * n_rows, :] = xs
        route_ref[s * TM:(s + 1) * TM, :] = route
        cnt_ref[s * SUBLANES:(s + 1) * SUBLANES, :] = jnp.broadcast_to(counts, (SUBLANES, LANES))


def _route(y, mod, gain, router_w, router_b, mod_row):
    t_all, d = y.shape
    n_exp = router_w.shape[1]
    n_tiles = t_all // TM
    n_rows = _sorted_rows(n_exp)
    rw = jnp.pad(router_w, ((0, 0), (0, LANES - n_exp)))
    rw_hi = rw.astype(BF16)
    rw = jnp.concatenate([rw_hi, (rw - rw_hi.astype(F32)).astype(BF16)], axis=1)
    rb = jnp.pad(router_b, (0, LANES - n_exp), constant_values=-1e30).reshape(1, LANES)
    return pl.pallas_call(
        _route_kernel,
        grid=(n_tiles // ROUTE_TILES,),
        in_specs=[
            pl.BlockSpec((ROUTE_TILES * TM, d), lambda i: (i, 0)),
            pl.BlockSpec((1, 6, d), lambda i: (mod_row(ROUTE_TILES * i), 0, 0)),
            pl.BlockSpec((1, d), lambda i: (0, 0)),
            pl.BlockSpec((d, 2 * LANES), lambda i: (0, 0)),
            pl.BlockSpec((1, LANES), lambda i: (0, 0)),
        ],
        out_specs=[
            pl.BlockSpec((ROUTE_TILES * n_rows, d), lambda i: (i, 0)),
            pl.BlockSpec((ROUTE_TILES * TM, LANES), lambda i: (i, 0)),
            pl.BlockSpec((ROUTE_TILES * SUBLANES, LANES), lambda i: (i, 0)),
        ],
        out_shape=[
            jax.ShapeDtypeStruct((n_tiles * n_rows + 2 * MOE_ROWS, d), F32),
            jax.ShapeDtypeStruct((t_all, LANES), F32),
            jax.ShapeDtypeStruct((n_tiles * SUBLANES, LANES), F32),
        ],
        compiler_params=_params(),
        name="moe_route",
    )(y, mod, gain.reshape(1, d), rw, rb)


def _plan_kernel(n_tiles, n_exp, n_rows, c_ref, srcx_ref, dstx_ref, be_ref, nxt_ref, nu_ref):
    nb = srcx_ref.shape[0]
    seg = jnp.ceil(c_ref[...] * (1.0 / UNIT)) * UNIT
    seg_b = seg.astype(BF16)
    sq_r = lax.broadcasted_iota(jnp.int32, (LANES, LANES), 0)
    sq_c = lax.broadcasted_iota(jnp.int32, (LANES, LANES), 1)
    upto = (sq_r <= sq_c).astype(BF16)
    seg_end_t = _dot(seg_b, upto)
    seg_off_t = seg_end_t - seg
    seg_end_e = _dot((sq_c <= sq_r).astype(BF16), seg_b)
    seg_off_e = seg_end_e - seg
    rows_e = seg_end_e[n_tiles - 1:n_tiles, :]
    nblk = jnp.ceil(rows_e * (1.0 / MOE_ROWS))
    end_blk = _dot(jnp.broadcast_to(nblk, (SUBLANES, LANES)).astype(BF16), upto)[0:1, :]
    start_blk = end_blk - nblk
    n_used = jnp.max(end_blk, axis=-1, keepdims=True)

    lane1 = lax.broadcasted_iota(jnp.int32, (1, LANES), 1).astype(F32)
    lane = lax.broadcasted_iota(jnp.int32, (nb, LANES), 1).astype(F32)
    blk = lax.broadcasted_iota(jnp.int32, (nb, LANES), 0).astype(F32)
    is_exp = lane1 < n_exp
    has_rows = (nblk > 0) & is_exp
    be = jnp.minimum(jnp.sum(((end_blk <= blk) & is_exp).astype(F32), axis=-1, keepdims=True), n_exp - 1.0)
    last_e = jnp.max(jnp.where(has_rows, lane1, 0.0), axis=-1, keepdims=True)
    used = blk[:, 0:1] < n_used
    be = jnp.where(used, be, last_e)
    nxt = jnp.min(jnp.where((lane > be) & has_rows, lane, float(LANES)), axis=-1, keepdims=True)
    nxt = jnp.where(nxt == LANES, be, nxt)
    onehot = (lane == be).astype(F32)

    def column_of_block(x):
        hi = jnp.floor(x * (1.0 / 256.0))
        lo = x - hi * 256.0
        oh = onehot.astype(BF16)
        return _dot_nt(oh, hi.astype(BF16)) * 256.0 + _dot_nt(oh, lo.astype(BF16))

    ends_b = column_of_block(seg_end_e)
    shift_b = column_of_block(seg_off_t) - column_of_block(seg_off_e)
    start_b = jnp.sum(onehot * start_blk, axis=-1, keepdims=True)
    rows_b = jnp.sum(onehot * rows_e, axis=-1, keepdims=True)
    r = (blk - start_b) * MOE_ROWS + lane * UNIT
    tile_of = jnp.zeros((nb, LANES), F32)
    for t in range(n_tiles):
        tile_of = tile_of + (ends_b[:, t:t + 1] <= r).astype(F32)
    tile_c = jnp.minimum(tile_of, n_tiles - 1.0)
    src = jnp.zeros((nb, LANES), F32)
    for t in range(n_tiles):
        src = src + jnp.where(tile_c == t, shift_b[:, t:t + 1] + float(t * n_rows), 0.0)
    valid = (r < rows_b) & used & (lane < MOE_ROWS // UNIT)
    srcx_ref[...] = jnp.where(valid, src + r, float(n_rows - UNIT)).astype(jnp.int32)
    spare = float(n_tiles * n_rows) + (blk - 2.0 * jnp.floor(blk * 0.5)) * MOE_ROWS + lane * UNIT
    dstx_ref[...] = jnp.where(valid, src + r, spare).astype(jnp.int32)

    rows_here = jnp.clip(rows_b - (blk[:, 0:1] - start_b) * MOE_ROWS, 0.0, float(MOE_ROWS))
    be_ref[...] = jnp.where(lane == 1, rows_here, be).astype(jnp.int32)
    nxt_ref[...] = jnp.broadcast_to(nxt, nxt_ref.shape).astype(jnp.int32)
    nu_ref[...] = jnp.broadcast_to(n_used, nu_ref.shape).astype(jnp.int32)


def _plan(counts, n_tiles, n_exp, n_rows, n_blocks):
    assert n_tiles <= LANES and n_exp <= LANES and n_tiles % SUBLANES == 0
    blk_units = MOE_ROWS // UNIT
    assert blk_units <= LANES
    nb = -(-n_blocks // SUBLANES) * SUBLANES
    srcx, dstx, be, nxt, nu = pl.pallas_call(
        functools.partial(_plan_kernel, n_tiles, n_exp, n_rows),
        out_shape=[
            jax.ShapeDtypeStruct((nb, LANES), jnp.int32),
            jax.ShapeDtypeStruct((nb, LANES), jnp.int32),
            jax.ShapeDtypeStruct((nb, LANES), jnp.int32),
            jax.ShapeDtypeStruct((nb, LANES), jnp.int32),
            jax.ShapeDtypeStruct((SUBLANES, LANES), jnp.int32),
        ],
        compiler_params=pltpu.CompilerParams(vmem_limit_bytes=VMEM_LIMIT_BYTES),
        name="moe_plan",
    )(counts)
    before = n_tiles * n_rows + MOE_ROWS + UNIT * jnp.arange(blk_units, dtype=jnp.int32)
    dst = jnp.concatenate([before, dstx[:n_blocks, :blk_units].reshape(-1)])
    return (srcx[:n_blocks, :blk_units].reshape(-1), dst, be[:n_blocks, 0], be[:n_blocks, 1],
            nxt[:n_blocks, 0], nu[0, :1])


def _gather_units(src_ref, base, table_ref, dst, sem):
    for u in range(dst.shape[0] // UNIT):
        start = pl.multiple_of(table_ref[base + u], UNIT)
        pltpu.make_async_copy(src_ref.at[pl.ds(start, UNIT)], dst.at[pl.ds(u * UNIT, UNIT)], sem).start()


def _scatter_units(src, base, table_ref, dst_ref, sem):
    for u in range(src.shape[0] // UNIT):
        start = pl.multiple_of(table_ref[base + u], UNIT)
        pltpu.make_async_copy(src.at[pl.ds(u * UNIT, UNIT)], dst_ref.at[pl.ds(start, UNIT)], sem).start()


def _wait_units(buf, sem):
    pltpu.make_async_copy(buf, buf, sem).wait()


def _expert_kernel(layer, be_ref, rows_ref, nu_ref, src_ref, dst_ref, nxt_ref, xs_ref, wgu_hbm, bgu_ref, wdn_hbm,
                   bdn_ref, ys_ref, wgu_st, wdn_st, wgu_bf, wdn_bf, x0, x1, y0, y1, gsem, ssem, wsem):
    i = pl.program_id(0)
    n_used = nu_ref[0]
    n_blocks = pl.num_programs(0)
    units = x0.shape[0] // UNIT
    d_exp = wdn_bf.shape[0]

    def weight_copies(e):
        return (pltpu.make_async_copy(wgu_hbm.at[layer, e], wgu_st, wsem.at[0]),
                pltpu.make_async_copy(wdn_hbm.at[layer, e], wdn_st, wsem.at[1]))

    def run_block(xcur, xoth, ycur, yoth, p, m):
        _wait_units(xcur, gsem.at[p])

        @pl.when(i >= 1)
        def _():
            _wait_units(ycur, ssem.at[p])

        _gather_units(xs_ref, jnp.minimum(i + 1, n_blocks - 1) * units, src_ref, xoth, gsem.at[1 - p])
        _scatter_units(yoth, i * units, dst_ref, ys_ref, ssem.at[1 - p])
        x = xcur[0:m, :].astype(BF16)
        gu = _dot(x, wgu_bf[...]) + bgu_ref[0, 0]
        gate = jnp.minimum(gu[:, :d_exp], SWIGLU_LIMIT)
        up = jnp.clip(gu[:, d_exp:], -SWIGLU_LIMIT, SWIGLU_LIMIT)
        hid = (up + 1.0) * gate * jax.nn.sigmoid(SWIGLU_ALPHA * gate)
        ycur[0:m, :] = _dot(hid.astype(BF16), wdn_bf[...]) + bdn_ref[0, 0]

        @pl.when(i == n_used - 1)
        def _():
            _wait_units(xoth, gsem.at[1 - p])
            _scatter_units(ycur, (i + 1) * units, dst_ref, ys_ref, ssem.at[p])
            _wait_units(yoth, ssem.at[1 - p])
            _wait_units(ycur, ssem.at[p])

    @pl.when(i < n_used)
    def _():
        @pl.when(i == 0)
        def _():
            for cp in weight_copies(be_ref[0]):
                cp.start(priority=1)
            _gather_units(xs_ref, 0, src_ref, x0, gsem.at[0])
            y0[...] = jnp.zeros_like(y0)
            y1[...] = jnp.zeros_like(y1)

        new_expert = jnp.logical_or(i == 0, be_ref[i] != be_ref[jnp.maximum(i - 1, 0)])

        @pl.when(new_expert)
        def _():
            for cp in weight_copies(be_ref[i]):
                cp.wait()
            wgu_bf[...] = wgu_st[...].astype(BF16)
            wdn_bf[...] = wdn_st[...].astype(BF16)

            @pl.when(nxt_ref[i] != be_ref[i])
            def _():
                for cp in weight_copies(nxt_ref[i]):
                    cp.start(priority=1)

        n_gran = x0.shape[0] // MOE_GRANULE
        granules = jnp.clip((rows_ref[i] + MOE_GRANULE - 1) // MOE_GRANULE, 1, n_gran)
        for g in range(1, n_gran + 1):
            @pl.when(jnp.logical_and(i % 2 == 0, granules == g))
            def _():
                run_block(x0, x1, y0, y1, 0, g * MOE_GRANULE)

            @pl.when(jnp.logical_and(i % 2 == 1, granules == g))
            def _():
                run_block(x1, x0, y1, y0, 1, g * MOE_GRANULE)


def _experts(layer, xs, src_units, dst_units, block_e, block_rows, next_e, n_used, w_gu, b_gu, w_dn, b_dn,
             n_blocks):
    d = w_dn.shape[-1]
    depth, n_exp, _, two_f = w_gu.shape
    d_exp = two_f // 2
    n_prefetch = 6

    def bias_block(i, be, rows, nu, src, dst, nxt):
        return (layer, be[i], 0, 0)

    grid_spec = pltpu.PrefetchScalarGridSpec(
        num_scalar_prefetch=n_prefetch,
        grid=(n_blocks,),
        in_specs=[
            pl.BlockSpec(memory_space=pl.ANY),
            pl.BlockSpec(memory_space=pl.ANY),
            pl.BlockSpec((1, 1, 1, two_f), bias_block),
            pl.BlockSpec(memory_space=pl.ANY),
            pl.BlockSpec((1, 1, 1, d), bias_block),
        ],
        out_specs=pl.BlockSpec(memory_space=pl.ANY),
        scratch_shapes=[
            pltpu.VMEM((d, two_f), F32),
            pltpu.VMEM((d_exp, d), F32),
            pltpu.VMEM((d, two_f), BF16),
            pltpu.VMEM((d_exp, d), BF16),
            pltpu.VMEM((MOE_ROWS, d), F32),
            pltpu.VMEM((MOE_ROWS, d), F32),
            pltpu.VMEM((MOE_ROWS, d), F32),
            pltpu.VMEM((MOE_ROWS, d), F32),
            pltpu.SemaphoreType.DMA((2,)),
            pltpu.SemaphoreType.DMA((2,)),
            pltpu.SemaphoreType.DMA((2,)),
        ],
    )
    return pl.pallas_call(
        functools.partial(_expert_kernel, layer),
        grid_spec=grid_spec,
        out_shape=jax.ShapeDtypeStruct(xs.shape, xs.dtype),
        input_output_aliases={n_prefetch: 0},
        compiler_params=_params(),
        name="moe_experts",
    )(block_e, block_rows, n_used, src_units, dst_units, next_e, xs, w_gu, b_gu.reshape(depth, n_exp, 1, two_f),
      w_dn, b_dn.reshape(depth, n_exp, 1, d))


def _combine_kernel(n_first, y_ref, mod_ref, route_ref, ys_ref, *o_refs):
    route = route_ref[...]
    tm = route.shape[0]
    n_rows = ys_ref.shape[0]
    col = lax.broadcasted_iota(jnp.int32, (tm, n_rows), 1).astype(F32)
    weights = functools.reduce(jnp.add, [
        jnp.where(col == route[:, 2 * TOP_K + k:2 * TOP_K + k + 1], route[:, k:k + 1], 0.0)
        for k in range(TOP_K)])
    acc = _dot(weights.astype(BF16), ys_ref[...].astype(BF16))
    out = y_ref[...] + mod_ref[0, 5:6, :] * acc
    if n_first is None:
        o_refs[0][...] = out
    else:
        first = pl.program_id(0) < n_first

        @pl.when(first)
        def _():
            o_refs[0][...] = out

        @pl.when(jnp.logical_not(first))
        def _():
            o_refs[1][...] = out


def _combine(y, mod, route, ys, n_rows, mod_row, split_rows=None):
    t_all, d = y.shape
    if split_rows is None:
        n_first = None
        out_specs = pl.BlockSpec((TM, d), lambda i: (i, 0))
        out_shape = jax.ShapeDtypeStruct((t_all, d), F32)
    else:
        n_first = split_rows // TM
        out_specs = [pl.BlockSpec((TM, d), lambda i: (jnp.minimum(i, n_first - 1), 0)),
                     pl.BlockSpec((TM, d), lambda i: (jnp.maximum(i - n_first, 0), 0))]
        out_shape = [jax.ShapeDtypeStruct((split_rows, d), F32),
                     jax.ShapeDtypeStruct((t_all - split_rows, d), F32)]
    return pl.pallas_call(
        functools.partial(_combine_kernel, n_first),
        grid=(t_all // TM,),
        in_specs=[
            pl.BlockSpec((TM, d), lambda i: (i, 0)),
            pl.BlockSpec((1, 6, d), lambda i: (mod_row(i), 0, 0)),
            pl.BlockSpec((TM, LANES), lambda i: (i, 0)),
            pl.BlockSpec((n_rows, d), lambda i: (i, 0)),
        ],
        out_specs=out_specs,
        out_shape=out_shape,
        compiler_params=_params(),
        name="moe_combine",
    )(y, mod, route, ys)


def _moe(layer, y, mod, gain, router_w, router_b, w_gu, b_gu, w_dn, b_dn, mod_row, split_rows=None):
    t_all, d = y.shape
    n_exp = router_w.shape[1]
    n_tiles = t_all // TM
    n_rows = _sorted_rows(n_exp)
    n_blocks = -(-(t_all * TOP_K + n_tiles * n_exp * (UNIT - 1)) // MOE_ROWS) + n_exp

    xs, route, cnt = _route(y, mod, gain, router_w, router_b, mod_row)
    counts = jnp.pad(cnt.reshape(n_tiles, SUBLANES, LANES)[:, 0, :], ((0, LANES - n_tiles), (0, 0)))
    src_x, dst_x, block_e, block_rows, next_e, n_used = _plan(counts, n_tiles, n_exp, n_rows, n_blocks)
    ys = _experts(layer, xs, src_x, dst_x, block_e, block_rows, next_e, n_used, w_gu, b_gu, w_dn, b_dn, n_blocks)
    return _combine(y, mod, route, ys, n_rows, mod_row, split_rows)


def kernel(x_prompt, x_sample, cache_k, cache_v, c, c_ctx, w_mod, b_mod, norm_mix, norm_ffn, gm_w_in, gm_norm_v, gm_w_s, gm_b_s, gm_w_out, at_w_qkv, at_q_norm, at_k_norm, at_w_o, pool_w_grp, pool_scale, router_w, router_b, w_gate_up, b_gate_up, w_down, b_down):
    n_ctx, ctx_len, d = x_prompt.shape
    n_lat, lat_len, _ = x_sample.shape
    depth = w_mod.shape[0]
    tp = n_ctx * ctx_len
    assert tp % lat_len == 0 and ctx_len % TM == 0 and lat_len % TM == 0 and ctx_len == TM
    assert 1 + n_lat <= SUBLANES
    assert tp % (ROUTE_TILES * TM) == 0 and lat_len % (ROUTE_TILES * TM) == 0

    assert depth >= 1
    y = (x_prompt.reshape(tp, d), x_sample.reshape(n_lat * lat_len, d))

    cvecs = jnp.zeros((SUBLANES, d), F32).at[0].set(c_ctx).at[1:1 + n_lat].set(c)
    mod_all = jnp.transpose(_ada_params(cvecs, w_mod, b_mod), (0, 2, 1, 3))

    def mod_row(i):
        return jnp.where(i * TM < tp, 0, 1 + (i * TM - tp) // lat_len)

    half = HEAD_DIM // 4
    inv = ROPE_THETA ** (-jnp.arange(half, dtype=F32) / half)
    pos = jnp.arange(lat_len)
    ang_r = (pos // GRID_W).astype(F32)[:, None] * inv[None, :]
    ang_c = (pos % GRID_W).astype(F32)[:, None] * inv[None, :]
    cos_t = jnp.concatenate([jnp.ones((TM, HEAD_DIM), F32),
                             jnp.concatenate([jnp.cos(ang_r)] * 2 + [jnp.cos(ang_c)] * 2, axis=1)], axis=0)
    sin_t = jnp.concatenate([jnp.zeros((TM, HEAD_DIM), F32),
                             jnp.concatenate([-jnp.sin(ang_r), jnp.sin(ang_r),
                                              -jnp.sin(ang_c), jnp.sin(ang_c)], axis=1)], axis=0)

    def rope_row(i):
        return jnp.where(i * TM < tp, 0, 1 + ((i * TM - tp) % lat_len) // TM)

    new_k, new_v = [], []
    for l in range(depth):
        kind, j = l % 3, l // 3
        mod = mod_all[l]
        if kind == 0:
            y = _gmlp(y, mod, norm_mix[l], gm_w_in[j], gm_norm_v[j], gm_w_s[j], gm_b_s[j], gm_w_out[j], mod_row)
        elif kind == 1:
            q, k, v = _qkv(y, mod, norm_mix[l], at_w_qkv[j], at_q_norm[j], at_k_norm[j],
                           cos_t, sin_t, mod_row, rope_row)
            new_k.append(k[:tp].reshape(n_ctx, ctx_len, N_KV_HEADS, HEAD_DIM))
            new_v.append(v[:tp].reshape(n_ctx, ctx_len, N_KV_HEADS, HEAD_DIM))
            past = cache_k.shape[2]
            ck = cache_k[:, j].reshape(n_lat, past, N_KV_HEADS * HEAD_DIM)
            cv = cache_v[:, j].reshape(n_lat, past, N_KV_HEADS * HEAD_DIM)
            y = _attention(y, mod, q, k, v, ck, cv, at_w_o[j], n_ctx, ctx_len, n_lat, lat_len)
        else:
            y = _pool(y, mod, norm_mix[l], pool_w_grp[j], pool_scale[j], n_ctx, ctx_len, 0, lambda b: 0)
            y = _pool(y, mod, norm_mix[l], pool_w_grp[j], pool_scale[j], n_lat, lat_len, tp, lambda b: 1 + b)
        y = _moe(l, y, mod, norm_ffn[l], router_w[l], router_b[l], w_gate_up, b_gate_up,
                 w_down, b_down, mod_row, split_rows=tp if l == depth - 1 else None)

    y_ctx, y_lat = y
    return (y_ctx.reshape(n_ctx, ctx_len, d), y_lat.reshape(n_lat, lat_len, d),
            jnp.stack(new_k, axis=1), jnp.stack(new_v, axis=1))
```

```python
import functools

import jax
import jax.numpy as jnp
from jax import lax
from jax.experimental import pallas as pl
from jax.experimental.pallas import tpu as pltpu

F32 = jnp.float32
BF16 = jnp.bfloat16

LANES = 128
SUBLANES = 8
VMEM_LIMIT_BYTES = 56 * 1024 * 1024

EPS = 1e-6
GRID_W = 64
CHUNK = 128
HEAD_DIM = 128
N_KV_HEADS = 2
ROPE_THETA = 10000.0
POOL_WINDOWS = (2, 4, 8, 16)
TOP_K = 4
SWIGLU_ALPHA = 1.702
SWIGLU_LIMIT = 7.0

TM = 256
MOE_ROWS = 512
MOE_GRANULE = 128
ROUTE_TILES = 2
HIDDEN_TILE = 256
UNIT = SUBLANES


def _params(n_axes=1):
    return pltpu.CompilerParams(dimension_semantics=("arbitrary",) * n_axes,
                                vmem_limit_bytes=VMEM_LIMIT_BYTES)


def _rms(x, gain):
    return x * lax.rsqrt(jnp.mean(x * x, axis=-1, keepdims=True) + EPS) * gain


def _norm_mod(y, gain, shift, scale):
    return _rms(y, gain) * (1.0 + scale) + shift


def _gelu_tanh(x):
    return 0.5 * x * (1.0 + jnp.tanh(0.7978845608028654 * (x + 0.044715 * (x * x * x))))


def _dot(a, b):
    return jnp.dot(a, b, preferred_element_type=F32)


def _dot_nt(a, b):
    return lax.dot_general(a, b, (((1,), (1,)), ((), ())), preferred_element_type=F32)


def _ada_kernel(cv_ref, w_ref, b_ref, o_ref):
    cv = cv_ref[...]
    s = cv * jax.nn.sigmoid(cv)
    w = w_ref[0]
    s_hi = s.astype(BF16)
    s_lo = (s - s_hi.astype(F32)).astype(BF16)
    w_hi = w.astype(BF16)
    w_lo = (w - w_hi.astype(F32)).astype(BF16)
    o_ref[0, 0] = _dot(s_hi, w_hi) + _dot(s_lo, w_hi) + _dot(s_hi, w_lo) + b_ref[0, 0]


def _ada_params(cvecs, w_mod, b_mod):
    depth, d, _ = w_mod.shape
    rows = cvecs.shape[0]
    return pl.pallas_call(
        _ada_kernel,
        grid=(depth, 6),
        in_specs=[
            pl.BlockSpec((rows, d), lambda l, j: (0, 0)),
            pl.BlockSpec((1, d, d), lambda l, j: (l, 0, j)),
            pl.BlockSpec((1, 1, 1, d), lambda l, j: (l, j, 0, 0)),
        ],
        out_specs=pl.BlockSpec((1, 1, rows, d), lambda l, j: (l, j, 0, 0)),
        out_shape=jax.ShapeDtypeStruct((depth, 6, rows, d), F32),
        compiler_params=_params(2),
        name="ada_params",
    )(cvecs, w_mod, b_mod.reshape(depth, 6, 1, d))


def _gmlp_kernel(n_first, *refs):
    if n_first is None:
        y = refs[0][...]
        refs = refs[1:]
    else:
        y = jnp.where(pl.program_id(0) < n_first, refs[0][...], refs[1][...])
        refs = refs[2:]
    mod_ref, gain_ref, win_ref, nv_ref, ws_ref, bexp_ref, wout_ref, o_ref = refs
    inner = nv_ref.shape[-1]
    groups = ws_ref.shape[0]
    h = _norm_mod(y, gain_ref[...], mod_ref[0, 0:1, :], mod_ref[0, 1:2, :])
    z = _gelu_tanh(_dot(h.astype(BF16), win_ref[...]))
    u = z[:, :inner]
    v = _rms(z[:, inner:], nv_ref[...]).astype(BF16)
    bexp = bexp_ref[...]
    chunks = []
    for c in range(y.shape[0] // CHUNK):
        cols = []
        for g in range(groups):
            vg = v[c * CHUNK:(c + 1) * CHUNK, g * LANES:(g + 1) * LANES]
            cols.append(_dot(ws_ref[g], vg))
        chunks.append(jnp.concatenate(cols, axis=1) + bexp)
    mixed = jnp.concatenate(chunks, axis=0)
    o = _dot((u * mixed).astype(BF16), wout_ref[...])
    o_ref[...] = y + mod_ref[0, 2:3, :] * o


def _gmlp(y, mod, gain, w_in, norm_v, w_s, b_s, w_out, mod_row):
    if isinstance(y, tuple):
        n_first = y[0].shape[0] // TM
        t_all, d = y[0].shape[0] + y[1].shape[0], y[0].shape[1]
        y_specs = [pl.BlockSpec((TM, d), lambda i: (jnp.minimum(i, n_first - 1), 0)),
                   pl.BlockSpec((TM, d), lambda i: (jnp.maximum(i - n_first, 0), 0))]
    else:
        n_first, y = None, (y,)
        t_all, d = y[0].shape
        y_specs = [pl.BlockSpec((TM, d), lambda i: (i, 0))]
    inner = norm_v.shape[-1]
    groups = w_s.shape[0]
    bexp = jnp.repeat(b_s.T, inner // groups, axis=1)
    return pl.pallas_call(
        functools.partial(_gmlp_kernel, n_first),
        grid=(t_all // TM,),
        in_specs=y_specs + [
            pl.BlockSpec((1, 6, d), lambda i: (mod_row(i), 0, 0)),
            pl.BlockSpec((1, d), lambda i: (0, 0)),
            pl.BlockSpec((d, 2 * inner), lambda i: (0, 0)),
            pl.BlockSpec((1, inner), lambda i: (0, 0)),
            pl.BlockSpec((groups, CHUNK, CHUNK), lambda i: (0, 0, 0)),
            pl.BlockSpec((CHUNK, inner), lambda i: (0, 0)),
            pl.BlockSpec((inner, d), lambda i: (0, 0)),
        ],
        out_specs=pl.BlockSpec((TM, d), lambda i: (i, 0)),
        out_shape=jax.ShapeDtypeStruct((t_all, d), F32),
        compiler_params=_params(),
        name="gmlp",
    )(*y, mod, gain.reshape(1, d), w_in.astype(BF16), norm_v.reshape(1, inner),
      w_s.astype(BF16), bexp, w_out.astype(BF16))


def _qkv_kernel(y_ref, mod_ref, gain_ref, w_ref, qn_ref, kn_ref, cos_ref, sin_ref,
                q_ref, k_ref, v_ref):
    y = y_ref[...]
    n_q = q_ref.shape[-1] // HEAD_DIM
    n_kv = k_ref.shape[-1] // HEAD_DIM
    h = _norm_mod(y, gain_ref[...], mod_ref[0, 0:1, :], mod_ref[0, 1:2, :])
    qkv = _dot(h.astype(BF16), w_ref[...])
    cos = cos_ref[...]
    sin = sin_ref[...]
    lane = lax.broadcasted_iota(jnp.int32, cos.shape, 1)
    low_half = (lane % (HEAD_DIM // 2)) < (HEAD_DIM // 4)

    def head(idx, norm):
        x = _rms(qkv[:, idx * HEAD_DIM:(idx + 1) * HEAD_DIM], norm)
        partner = jnp.where(low_half,
                            pltpu.roll(x, HEAD_DIM - HEAD_DIM // 4, 1),
                            pltpu.roll(x, HEAD_DIM // 4, 1))
        return x * cos + partner * sin

    qn = qn_ref[...]
    kn = kn_ref[...]
    q_ref[...] = jnp.concatenate([head(i, qn) for i in range(n_q)], axis=1).astype(q_ref.dtype)
    k_ref[...] = jnp.concatenate([head(n_q + i, kn) for i in range(n_kv)], axis=1)
    v_ref[...] = qkv[:, (n_q + n_kv) * HEAD_DIM:]


def _qkv(y, mod, gain, w_qkv, q_norm, k_norm, cos_t, sin_t, mod_row, rope_row):
    t_all, d = y.shape
    n_kv_cols = N_KV_HEADS * HEAD_DIM
    return pl.pallas_call(
        _qkv_kernel,
        grid=(t_all // TM,),
        in_specs=[
            pl.BlockSpec((TM, d), lambda i: (i, 0)),
            pl.BlockSpec((1, 6, d), lambda i: (mod_row(i), 0, 0)),
            pl.BlockSpec((1, d), lambda i: (0, 0)),
            pl.BlockSpec(w_qkv.shape, lambda i: (0, 0)),
            pl.BlockSpec((1, HEAD_DIM), lambda i: (0, 0)),
            pl.BlockSpec((1, HEAD_DIM), lambda i: (0, 0)),
            pl.BlockSpec((TM, HEAD_DIM), lambda i: (rope_row(i), 0)),
            pl.BlockSpec((TM, HEAD_DIM), lambda i: (rope_row(i), 0)),
        ],
        out_specs=[
            pl.BlockSpec((TM, d), lambda i: (i, 0)),
            pl.BlockSpec((TM, n_kv_cols), lambda i: (i, 0)),
            pl.BlockSpec((TM, n_kv_cols), lambda i: (i, 0)),
        ],
        out_shape=[
            jax.ShapeDtypeStruct((t_all, d), BF16),
            jax.ShapeDtypeStruct((t_all, n_kv_cols), F32),
            jax.ShapeDtypeStruct((t_all, n_kv_cols), F32),
        ],
        compiler_params=_params(),
        name="qkv_project",
    )(y, mod, gain.reshape(1, d), w_qkv.astype(BF16), q_norm.reshape(1, HEAD_DIM),
      k_norm.reshape(1, HEAD_DIM), cos_t, sin_t)


def _attend(q, key_sets):
    n_heads = q.shape[1] // HEAD_DIM
    rep = n_heads // N_KV_HEADS
    scale = HEAD_DIM ** -0.5
    outs = []
    for hd in range(n_heads):
        g = hd // rep
        qh = q[:, hd * HEAD_DIM:(hd + 1) * HEAD_DIM]
        cols = slice(g * HEAD_DIM, (g + 1) * HEAD_DIM)
        scores = [_dot_nt(qh, k[:, cols]) * scale for k, _ in key_sets]
        m = functools.reduce(jnp.maximum, [jnp.max(s, axis=-1, keepdims=True) for s in scores])
        probs = [jnp.exp(s - m) for s in scores]
        denom = functools.reduce(jnp.add, [jnp.sum(p, axis=-1, keepdims=True) for p in probs])
        o = functools.reduce(jnp.add, [_dot(p.astype(BF16), v[:, cols])
                                       for p, (_, v) in zip(probs, key_sets)])
        outs.append(o / denom)
    return jnp.concatenate(outs, axis=1)


def _attn_ctx_kernel(y_ref, mod_ref, q_ref, k_ref, v_ref, wo_ref, o_ref):
    att = _attend(q_ref[...], [(k_ref[...].astype(BF16), v_ref[...].astype(BF16))])
    o_ref[...] = y_ref[...] + mod_ref[0, 2:3, :] * _dot(att.astype(BF16), wo_ref[...])


def _attn_lat_kernel(y_ref, mod_ref, q_ref, k_ref, v_ref, ck_ref, cv_ref, wo_ref, o_ref):
    sets = [(ck_ref[0].astype(BF16), cv_ref[0].astype(BF16)),
            (k_ref[...].astype(BF16), v_ref[...].astype(BF16))]
    att = _attend(q_ref[...], sets)
    o_ref[...] = y_ref[...] + mod_ref[0, 2:3, :] * _dot(att.astype(BF16), wo_ref[...])


def _attention(y, mod, q, k, v, cache_k, cache_v, w_o, n_ctx, ctx_len, n_lat, lat_len):
    t_all, d = y.shape
    kvc = k.shape[1]
    tp = n_ctx * ctx_len
    wo = w_o.astype(BF16)
    y = pl.pallas_call(
        _attn_ctx_kernel,
        grid=(n_ctx,),
        in_specs=[
            pl.BlockSpec((ctx_len, d), lambda b: (b, 0)),
            pl.BlockSpec((1, 6, d), lambda b: (0, 0, 0)),
            pl.BlockSpec((ctx_len, d), lambda b: (b, 0)),
            pl.BlockSpec((ctx_len, kvc), lambda b: (b, 0)),
            pl.BlockSpec((ctx_len, kvc), lambda b: (b, 0)),
            pl.BlockSpec((d, d), lambda b: (0, 0)),
        ],
        out_specs=pl.BlockSpec((ctx_len, d), lambda b: (b, 0)),
        out_shape=jax.ShapeDtypeStruct((t_all, d), F32),
        input_output_aliases={0: 0},
        compiler_params=_params(),
        name="attn_context",
    )(y, mod, q, k, v, wo)
    tq = TM
    qb = lat_len // tq
    past = cache_k.shape[1]
    y = pl.pallas_call(
        _attn_lat_kernel,
        grid=(n_lat, qb),
        in_specs=[
            pl.BlockSpec((tq, d), lambda b, i: (tp // tq + b * qb + i, 0)),
            pl.BlockSpec((1, 6, d), lambda b, i: (1 + b, 0, 0)),
            pl.BlockSpec((tq, d), lambda b, i: (tp // tq + b * qb + i, 0)),
            pl.BlockSpec((lat_len, kvc), lambda b, i: (tp // lat_len + b, 0)),
            pl.BlockSpec((lat_len, kvc), lambda b, i: (tp // lat_len + b, 0)),
            pl.BlockSpec((1, past, kvc), lambda b, i: (b, 0, 0)),
            pl.BlockSpec((1, past, kvc), lambda b, i: (b, 0, 0)),
            pl.BlockSpec((d, d), lambda b, i: (0, 0)),
        ],
        out_specs=pl.BlockSpec((tq, d), lambda b, i: (tp // tq + b * qb + i, 0)),
        out_shape=jax.ShapeDtypeStruct((t_all, d), F32),
        input_output_aliases={0: 0},
        compiler_params=_params(2),
        name="attn_latent",
    )(y, mod, q, k, v, cache_k, cache_v, wo)
    return y


def _pool_kernel(y_ref, mod_ref, gain_ref, wg_ref, scale_ref, o_ref):
    y = y_ref[...]
    s_len, d = y.shape
    gd = d // len(POOL_WINDOWS)
    h = _norm_mod(y, gain_ref[...], mod_ref[0, 0:1, :], mod_ref[0, 1:2, :])
    t_idx = lax.broadcasted_iota(jnp.int32, (s_len, s_len), 0)
    s_idx = lax.broadcasted_iota(jnp.int32, (s_len, s_len), 1)
    t_col = lax.broadcasted_iota(jnp.int32, (s_len, 1), 0)
    parts = []
    for g, w in enumerate(POOL_WINDOWS):
        lo = t_idx - w // 2
        window = ((s_idx >= lo) & (s_idx < lo + w)).astype(BF16)
        count = (jnp.minimum(t_col - w // 2 + w, s_len) - jnp.maximum(t_col - w // 2, 0)).astype(F32)
        hg = h[:, g * gd:(g + 1) * gd]
        h_hi = hg.astype(BF16)
        h_lo = (hg - h_hi.astype(F32)).astype(BF16)
        mean = (_dot(window, h_hi) + _dot(window, h_lo)) / count
        parts.append(_dot((mean - hg).astype(BF16), wg_ref[g]))
    mixed = jnp.concatenate(parts, axis=1) * scale_ref[...]
    o_ref[...] = y + mod_ref[0, 2:3, :] * mixed


def _pool(y, mod, gain, w_grp, scale, n_seq, s_len, row_off, mod_off):
    t_all, d = y.shape
    blk_off = row_off // s_len
    return pl.pallas_call(
        _pool_kernel,
        grid=(n_seq,),
        in_specs=[
            pl.BlockSpec((s_len, d), lambda b: (blk_off + b, 0)),
            pl.BlockSpec((1, 6, d), lambda b: (mod_off(b), 0, 0)),
            pl.BlockSpec((1, d), lambda b: (0, 0)),
            pl.BlockSpec(w_grp.shape, lambda b: (0, 0, 0)),
            pl.BlockSpec((1, d), lambda b: (0, 0)),
        ],
        out_specs=pl.BlockSpec((s_len, d), lambda b: (blk_off + b, 0)),
        out_shape=jax.ShapeDtypeStruct((t_all, d), F32),
        input_output_aliases={0: 0},
        compiler_params=_params(),
        name="pool_mixer",
    )(y, mod, gain.reshape(1, d), w_grp.astype(BF16), scale.reshape(1, d))


def _sorted_rows(n_exp):
    worst = TM * TOP_K + n_exp * (UNIT - 1) + UNIT
    return -(-worst // LANES) * LANES


def _route_tile(y, mod_ref, gain_ref, rw_ref, rb_ref, n_rows):
    tm = y.shape[0]
    h = _norm_mod(y, gain_ref[...], mod_ref[0, 3:4, :], mod_ref[0, 4:5, :])
    h_hi = h.astype(BF16)
    h_lo = (h - h_hi.astype(F32)).astype(BF16)
    hi_terms = _dot(h_hi, rw_ref[...])
    logits = hi_terms[:, :LANES] + hi_terms[:, LANES:] + _dot(h_lo, rw_ref[:, :LANES]) + rb_ref[...]
    lane = lax.broadcasted_iota(jnp.int32, logits.shape, 1).astype(F32)
    work = logits
    vals, ids = [], []
    for _ in range(TOP_K):
        m = jnp.max(work, axis=-1, keepdims=True)
        idx = jnp.min(jnp.where(work == m, lane, float(LANES)), axis=-1, keepdims=True)
        vals.append(m)
        ids.append(idx)
        work = jnp.where(lane == idx, -jnp.inf, work)
    exps = [jnp.exp(v - vals[0]) for v in vals]
    denom = functools.reduce(jnp.add, exps)
    onehot = functools.reduce(jnp.add, [(lane == idx).astype(F32) for idx in ids])

    counts = jnp.sum(onehot, axis=0, keepdims=True)
    seg = jnp.ceil(counts * (1.0 / UNIT)) * UNIT
    e_row = lax.broadcasted_iota(jnp.int32, (LANES, LANES), 0)
    e_col = lax.broadcasted_iota(jnp.int32, (LANES, LANES), 1)
    earlier = (e_row < e_col).astype(BF16)
    seg_off = _dot(jnp.broadcast_to(seg, (SUBLANES, LANES)).astype(BF16), earlier)[0:1, :]
    row = lax.broadcasted_iota(jnp.int32, (tm, tm), 0)
    col = lax.broadcasted_iota(jnp.int32, (tm, tm), 1)
    before = (col < row).astype(BF16)
    pos_mat = _dot(before, onehot.astype(BF16)) + seg_off

    route = jnp.zeros(logits.shape, F32)
    for k in range(TOP_K):
        pos_k = jnp.sum(jnp.where(lane == ids[k], pos_mat, 0.0), axis=-1, keepdims=True)
        route = jnp.where(lane == k, exps[k] / denom, route)
        route = jnp.where(lane == TOP_K + k, ids[k], route)
        route = jnp.where(lane == 2 * TOP_K + k, pos_k, route)

    route_t = jnp.transpose(route)
    out_row = lax.broadcasted_iota(jnp.int32, (n_rows, tm), 0).astype(F32)
    perm = functools.reduce(jnp.add, [(out_row == route_t[2 * TOP_K + k:2 * TOP_K + k + 1, :]).astype(F32)
                                      for k in range(TOP_K)])
    return _dot(perm.astype(BF16), h_hi), route, counts


def _route_kernel(y_ref, mod_ref, gain_ref, rw_ref, rb_ref, xs_ref, route_ref, cnt_ref):
    n_rows = xs_ref.shape[0] // ROUTE_TILES
    for s in range(ROUTE_TILES):
        xs, route, counts = _route_tile(y_ref[s * TM:(s + 1) * TM, :], mod_ref, gain_ref, rw_ref, rb_ref, n_rows)
        xs_ref[s * n_rows:(s + 1) * n_rows, :] = xs
        route_ref[s * TM:(s + 1) * TM, :] = route
        cnt_ref[s * SUBLANES:(s + 1) * SUBLANES, :] = jnp.broadcast_to(counts, (SUBLANES, LANES))


def _route(y, mod, gain, router_w, router_b, mod_row):
    t_all, d = y.shape
    n_exp = router_w.shape[1]
    n_tiles = t_all // TM
    n_rows = _sorted_rows(n_exp)
    rw = jnp.pad(router_w, ((0, 0), (0, LANES - n_exp)))
    rw_hi = rw.astype(BF16)
    rw = jnp.concatenate([rw_hi, (rw - rw_hi.astype(F32)).astype(BF16)], axis=1)
    rb = jnp.pad(router_b, (0, LANES - n_exp), constant_values=-1e30).reshape(1, LANES)
    return pl.pallas_call(
        _route_kernel,
        grid=(n_tiles // ROUTE_TILES,),
        in_specs=[
            pl.BlockSpec((ROUTE_TILES * TM, d), lambda i: (i, 0)),
            pl.BlockSpec((1, 6, d), lambda i: (mod_row(ROUTE_TILES * i), 0, 0)),
            pl.BlockSpec((1, d), lambda i: (0, 0)),
            pl.BlockSpec((d, 2 * LANES), lambda i: (0, 0)),
            pl.BlockSpec((1, LANES), lambda i: (0, 0)),
        ],
        out_specs=[
            pl.BlockSpec((ROUTE_TILES * n_rows, d), lambda i: (i, 0)),
            pl.BlockSpec((ROUTE_TILES * TM, LANES), lambda i: (i, 0)),
            pl.BlockSpec((ROUTE_TILES * SUBLANES, LANES), lambda i: (i, 0)),
        ],
        out_shape=[
            jax.ShapeDtypeStruct((n_tiles * n_rows + 2 * MOE_ROWS, d), F32),
            jax.ShapeDtypeStruct((t_all, LANES), F32),
            jax.ShapeDtypeStruct((n_tiles * SUBLANES, LANES), F32),
        ],
        compiler_params=_params(),
        name="moe_route",
    )(y, mod, gain.reshape(1, d), rw, rb)


def _plan_kernel(n_tiles, n_exp, n_rows, c_ref, srcx_ref, dstx_ref, be_ref, nxt_ref, nu_ref):
    nb = srcx_ref.shape[0]
    seg = jnp.ceil(c_ref[...] * (1.0 / UNIT)) * UNIT
    seg_b = seg.astype(BF16)
    sq_r = lax.broadcasted_iota(jnp.int32, (LANES, LANES), 0)
    sq_c = lax.broadcasted_iota(jnp.int32, (LANES, LANES), 1)
    upto = (sq_r <= sq_c).astype(BF16)
    seg_end_t = _dot(seg_b, upto)
    seg_off_t = seg_end_t - seg
    seg_end_e = _dot((sq_c <= sq_r).astype(BF16), seg_b)
    seg_off_e = seg_end_e - seg
    rows_e = seg_end_e[n_tiles - 1:n_tiles, :]
    nblk = jnp.ceil(rows_e * (1.0 / MOE_ROWS))
    end_blk = _dot(jnp.broadcast_to(nblk, (SUBLANES, LANES)).astype(BF16), upto)[0:1, :]
    start_blk = end_blk - nblk
    n_used = jnp.max(end_blk, axis=-1, keepdims=True)

    lane1 = lax.broadcasted_iota(jnp.int32, (1, LANES), 1).astype(F32)
    lane = lax.broadcasted_iota(jnp.int32, (nb, LANES), 1).astype(F32)
    blk = lax.broadcasted_iota(jnp.int32, (nb, LANES), 0).astype(F32)
    is_exp = lane1 < n_exp
    has_rows = (nblk > 0) & is_exp
    be = jnp.minimum(jnp.sum(((end_blk <= blk) & is_exp).astype(F32), axis=-1, keepdims=True), n_exp - 1.0)
    last_e = jnp.max(jnp.where(has_rows, lane1, 0.0), axis=-1, keepdims=True)
    used = blk[:, 0:1] < n_used
    be = jnp.where(used, be, last_e)
    nxt = jnp.min(jnp.where((lane > be) & has_rows, lane, float(LANES)), axis=-1, keepdims=True)
    nxt = jnp.where(nxt == LANES, be, nxt)
    onehot = (lane == be).astype(F32)

    def column_of_block(x):
        hi = jnp.floor(x * (1.0 / 256.0))
        lo = x - hi * 256.0
        oh = onehot.astype(BF16)
        return _dot_nt(oh, hi.astype(BF16)) * 256.0 + _dot_nt(oh, lo.astype(BF16))

    ends_b = column_of_block(seg_end_e)
    shift_b = column_of_block(seg_off_t) - column_of_block(seg_off_e)
    start_b = jnp.sum(onehot * start_blk, axis=-1, keepdims=True)
    rows_b = jnp.sum(onehot * rows_e, axis=-1, keepdims=True)
    r = (blk - start_b) * MOE_ROWS + lane * UNIT
    tile_of = jnp.zeros((nb, LANES), F32)
    for t in range(n_tiles):
        tile_of = tile_of + (ends_b[:, t:t + 1] <= r).astype(F32)
    tile_c = jnp.minimum(tile_of, n_tiles - 1.0)
    src = jnp.zeros((nb, LANES), F32)
    for t in range(n_tiles):
        src = src + jnp.where(tile_c == t, shift_b[:, t:t + 1] + float(t * n_rows), 0.0)
    valid = (r < rows_b) & used & (lane < MOE_ROWS // UNIT)
    srcx_ref[...] = jnp.where(valid, src + r, float(n_rows - UNIT)).astype(jnp.int32)
    spare = float(n_tiles * n_rows) + (blk - 2.0 * jnp.floor(blk * 0.5)) * MOE_ROWS + lane * UNIT
    dstx_ref[...] = jnp.where(valid, src + r, spare).astype(jnp.int32)

    rows_here = jnp.clip(rows_b - (blk[:, 0:1] - start_b) * MOE_ROWS, 0.0, float(MOE_ROWS))
    be_ref[...] = jnp.where(lane == 1, rows_here, be).astype(jnp.int32)
    nxt_ref[...] = jnp.broadcast_to(nxt, nxt_ref.shape).astype(jnp.int32)
    nu_ref[...] = jnp.broadcast_to(n_used, nu_ref.shape).astype(jnp.int32)


def _plan(counts, n_tiles, n_exp, n_rows, n_blocks):
    assert n_tiles <= LANES and n_exp <= LANES and n_tiles % SUBLANES == 0
    blk_units = MOE_ROWS // UNIT
    assert blk_units <= LANES
    nb = -(-n_blocks // SUBLANES) * SUBLANES
    srcx, dstx, be, nxt, nu = pl.pallas_call(
        functools.partial(_plan_kernel, n_tiles, n_exp, n_rows),
        out_shape=[
            jax.ShapeDtypeStruct((nb, LANES), jnp.int32),
            jax.ShapeDtypeStruct((nb, LANES), jnp.int32),
            jax.ShapeDtypeStruct((nb, LANES), jnp.int32),
            jax.ShapeDtypeStruct((nb, LANES), jnp.int32),
            jax.ShapeDtypeStruct((SUBLANES, LANES), jnp.int32),
        ],
        compiler_params=pltpu.CompilerParams(vmem_limit_bytes=VMEM_LIMIT_BYTES),
        name="moe_plan",
    )(counts)
    before = n_tiles * n_rows + MOE_ROWS + UNIT * jnp.arange(blk_units, dtype=jnp.int32)
    dst = jnp.concatenate([before, dstx[:n_blocks, :blk_units].reshape(-1)])
    return (srcx[:n_blocks, :blk_units].reshape(-1), dst, be[:n_blocks, 0], be[:n_blocks, 1],
            nxt[:n_blocks, 0], nu[0, :1])


def _gather_units(src_ref, base, table_ref, dst, sem):
    for u in range(dst.shape[0] // UNIT):
        start = pl.multiple_of(table_ref[base + u], UNIT)
        pltpu.make_async_copy(src_ref.at[pl.ds(start, UNIT)], dst.at[pl.ds(u * UNIT, UNIT)], sem).start()


def _scatter_units(src, base, table_ref, dst_ref, sem):
    for u in range(src.shape[0] // UNIT):
        start = pl.multiple_of(table_ref[base + u], UNIT)
        pltpu.make_async_copy(src.at[pl.ds(u * UNIT, UNIT)], dst_ref.at[pl.ds(start, UNIT)], sem).start()


def _wait_units(buf, sem):
    pltpu.make_async_copy(buf, buf, sem).wait()


def _expert_kernel(layer, be_ref, rows_ref, nu_ref, src_ref, dst_ref, nxt_ref, xs_ref, wgu_hbm, bgu_ref, wdn_hbm,
                   bdn_ref, ys_ref, wgu_st, wdn_st, wgu_bf, wdn_bf, x0, x1, y0, y1, gsem, ssem, wsem):
    i = pl.program_id(0)
    n_used = nu_ref[0]
    n_blocks = pl.num_programs(0)
    units = x0.shape[0] // UNIT
    d_exp = wdn_bf.shape[0]

    def weight_copies(e):
        return (pltpu.make_async_copy(wgu_hbm.at[layer, e], wgu_st, wsem.at[0]),
                pltpu.make_async_copy(wdn_hbm.at[layer, e], wdn_st, wsem.at[1]))

    def run_block(xcur, xoth, ycur, yoth, p, m):
        _wait_units(xcur, gsem.at[p])

        @pl.when(i >= 1)
        def _():
            _wait_units(ycur, ssem.at[p])

        _gather_units(xs_ref, jnp.minimum(i + 1, n_blocks - 1) * units, src_ref, xoth, gsem.at[1 - p])
        _scatter_units(yoth, i * units, dst_ref, ys_ref, ssem.at[1 - p])
        x = xcur[0:m, :].astype(BF16)
        gu = _dot(x, wgu_bf[...])
        bias = bgu_ref[0, 0]
        hid = []
        for c in range(d_exp // HIDDEN_TILE):
            lo, hi = c * HIDDEN_TILE, (c + 1) * HIDDEN_TILE
            gate = jnp.minimum(gu[:, 2 * lo:lo + hi] + bias[:, lo:hi], SWIGLU_LIMIT)
            up = jnp.clip(gu[:, lo + hi:2 * hi] + bias[:, d_exp + lo:d_exp + hi], -SWIGLU_LIMIT, SWIGLU_LIMIT)
            hid.append(((up + 1.0) * gate * jax.nn.sigmoid(SWIGLU_ALPHA * gate)).astype(BF16))
        ycur[0:m, :] = _dot(jnp.concatenate(hid, axis=1), wdn_bf[...]) + bdn_ref[0, 0]

        @pl.when(i == n_used - 1)
        def _():
            _wait_units(xoth, gsem.at[1 - p])
            _scatter_units(ycur, (i + 1) * units, dst_ref, ys_ref, ssem.at[p])
            _wait_units(yoth, ssem.at[1 - p])
            _wait_units(ycur, ssem.at[p])

    @pl.when(i < n_used)
    def _():
        @pl.when(i == 0)
        def _():
            for cp in weight_copies(be_ref[0]):
                cp.start(priority=1)
            _gather_units(xs_ref, 0, src_ref, x0, gsem.at[0])
            y0[...] = jnp.zeros_like(y0)
            y1[...] = jnp.zeros_like(y1)

        new_expert = jnp.logical_or(i == 0, be_ref[i] != be_ref[jnp.maximum(i - 1, 0)])

        @pl.when(new_expert)
        def _():
            for cp in weight_copies(be_ref[i]):
                cp.wait()
            for c in range(d_exp // HIDDEN_TILE):
                lo, hi = c * HIDDEN_TILE, (c + 1) * HIDDEN_TILE
                wgu_bf[:, 2 * lo:lo + hi] = wgu_st[:, lo:hi].astype(BF16)
                wgu_bf[:, lo + hi:2 * hi] = wgu_st[:, d_exp + lo:d_exp + hi].astype(BF16)
            wdn_bf[...] = wdn_st[...].astype(BF16)

            @pl.when(nxt_ref[i] != be_ref[i])
            def _():
                for cp in weight_copies(nxt_ref[i]):
                    cp.start(priority=1)

        n_gran = x0.shape[0] // MOE_GRANULE
        granules = jnp.clip((rows_ref[i] + MOE_GRANULE - 1) // MOE_GRANULE, 1, n_gran)
        for g in range(1, n_gran + 1):
            @pl.when(jnp.logical_and(i % 2 == 0, granules == g))
            def _():
                run_block(x0, x1, y0, y1, 0, g * MOE_GRANULE)

            @pl.when(jnp.logical_and(i % 2 == 1, granules == g))
            def _():
                run_block(x1, x0, y1, y0, 1, g * MOE_GRANULE)


def _experts(layer, xs, src_units, dst_units, block_e, block_rows, next_e, n_used, w_gu, b_gu, w_dn, b_dn,
             n_blocks):
    d = w_dn.shape[-1]
    depth, n_exp, _, two_f = w_gu.shape
    d_exp = two_f // 2
    n_prefetch = 6

    def bias_block(i, be, rows, nu, src, dst, nxt):
        return (layer, be[i], 0, 0)

    grid_spec = pltpu.PrefetchScalarGridSpec(
        num_scalar_prefetch=n_prefetch,
        grid=(n_blocks,),
        in_specs=[
            pl.BlockSpec(memory_space=pl.ANY),
            pl.BlockSpec(memory_space=pl.ANY),
            pl.BlockSpec((1, 1, 1, two_f), bias_block),
            pl.BlockSpec(memory_space=pl.ANY),
            pl.BlockSpec((1, 1, 1, d), bias_block),
        ],
        out_specs=pl.BlockSpec(memory_space=pl.ANY),
        scratch_shapes=[
            pltpu.VMEM((d, two_f), F32),
            pltpu.VMEM((d_exp, d), F32),
            pltpu.VMEM((d, two_f), BF16),
            pltpu.VMEM((d_exp, d), BF16),
            pltpu.VMEM((MOE_ROWS, d), F32),
            pltpu.VMEM((MOE_ROWS, d), F32),
            pltpu.VMEM((MOE_ROWS, d), F32),
            pltpu.VMEM((MOE_ROWS, d), F32),
            pltpu.SemaphoreType.DMA((2,)),
            pltpu.SemaphoreType.DMA((2,)),
            pltpu.SemaphoreType.DMA((2,)),
        ],
    )
    return pl.pallas_call(
        functools.partial(_expert_kernel, layer),
        grid_spec=grid_spec,
        out_shape=jax.ShapeDtypeStruct(xs.shape, xs.dtype),
        input_output_aliases={n_prefetch: 0},
        compiler_params=_params(),
        name="moe_experts",
    )(block_e, block_rows, n_used, src_units, dst_units, next_e, xs, w_gu, b_gu.reshape(depth, n_exp, 1, two_f),
      w_dn, b_dn.reshape(depth, n_exp, 1, d))


def _combine_kernel(n_first, y_ref, mod_ref, route_ref, ys_ref, *o_refs):
    route = route_ref[...]
    tm = route.shape[0]
    n_rows = ys_ref.shape[0]
    col = lax.broadcasted_iota(jnp.int32, (tm, n_rows), 1).astype(F32)
    weights = functools.reduce(jnp.add, [
        jnp.where(col == route[:, 2 * TOP_K + k:2 * TOP_K + k + 1], route[:, k:k + 1], 0.0)
        for k in range(TOP_K)])
    acc = _dot(weights.astype(BF16), ys_ref[...].astype(BF16))
    out = y_ref[...] + mod_ref[0, 5:6, :] * acc
    if n_first is None:
        o_refs[0][...] = out
    else:
        first = pl.program_id(0) < n_first

        @pl.when(first)
        def _():
            o_refs[0][...] = out

        @pl.when(jnp.logical_not(first))
        def _():
            o_refs[1][...] = out


def _combine(y, mod, route, ys, n_rows, mod_row, split_rows=None):
    t_all, d = y.shape
    if split_rows is None:
        n_first = None
        out_specs = pl.BlockSpec((TM, d), lambda i: (i, 0))
        out_shape = jax.ShapeDtypeStruct((t_all, d), F32)
    else:
        n_first = split_rows // TM
        out_specs = [pl.BlockSpec((TM, d), lambda i: (jnp.minimum(i, n_first - 1), 0)),
                     pl.BlockSpec((TM, d), lambda i: (jnp.maximum(i - n_first, 0), 0))]
        out_shape = [jax.ShapeDtypeStruct((split_rows, d), F32),
                     jax.ShapeDtypeStruct((t_all - split_rows, d), F32)]
    return pl.pallas_call(
        functools.partial(_combine_kernel, n_first),
        grid=(t_all // TM,),
        in_specs=[
            pl.BlockSpec((TM, d), lambda i: (i, 0)),
            pl.BlockSpec((1, 6, d), lambda i: (mod_row(i), 0, 0)),
            pl.BlockSpec((TM, LANES), lambda i: (i, 0)),
            pl.BlockSpec((n_rows, d), lambda i: (i, 0)),
        ],
        out_specs=out_specs,
        out_shape=out_shape,
        compiler_params=_params(),
        name="moe_combine",
    )(y, mod, route, ys)


def _moe(layer, y, mod, gain, router_w, router_b, w_gu, b_gu, w_dn, b_dn, mod_row, split_rows=None):
    t_all, d = y.shape
    n_exp = router_w.shape[1]
    n_tiles = t_all // TM
    n_rows = _sorted_rows(n_exp)
    n_blocks = -(-(t_all * TOP_K + n_tiles * n_exp * (UNIT - 1)) // MOE_ROWS) + n_exp

    xs, route, cnt = _route(y, mod, gain, router_w, router_b, mod_row)
    counts = jnp.pad(cnt.reshape(n_tiles, SUBLANES, LANES)[:, 0, :], ((0, LANES - n_tiles), (0, 0)))
    src_x, dst_x, block_e, block_rows, next_e, n_used = _plan(counts, n_tiles, n_exp, n_rows, n_blocks)
    ys = _experts(layer, xs, src_x, dst_x, block_e, block_rows, next_e, n_used, w_gu, b_gu, w_dn, b_dn, n_blocks)
    return _combine(y, mod, route, ys, n_rows, mod_row, split_rows)


def kernel(x_prompt, x_sample, cache_k, cache_v, c, c_ctx, w_mod, b_mod, norm_mix, norm_ffn, gm_w_in, gm_norm_v, gm_w_s, gm_b_s, gm_w_out, at_w_qkv, at_q_norm, at_k_norm, at_w_o, pool_w_grp, pool_scale, router_w, router_b, w_gate_up, b_gate_up, w_down, b_down):
    n_ctx, ctx_len, d = x_prompt.shape
    n_lat, lat_len, _ = x_sample.shape
    depth = w_mod.shape[0]
    tp = n_ctx * ctx_len
    assert tp % lat_len == 0 and ctx_len % TM == 0 and lat_len % TM == 0 and ctx_len == TM
    assert 1 + n_lat <= SUBLANES
    assert tp % (ROUTE_TILES * TM) == 0 and lat_len % (ROUTE_TILES * TM) == 0

    assert depth >= 1
    y = (x_prompt.reshape(tp, d), x_sample.reshape(n_lat * lat_len, d))

    cvecs = jnp.zeros((SUBLANES, d), F32).at[0].set(c_ctx).at[1:1 + n_lat].set(c)
    mod_all = jnp.transpose(_ada_params(cvecs, w_mod, b_mod), (0, 2, 1, 3))

    def mod_row(i):
        return jnp.where(i * TM < tp, 0, 1 + (i * TM - tp) // lat_len)

    half = HEAD_DIM // 4
    inv = ROPE_THETA ** (-jnp.arange(half, dtype=F32) / half)
    pos = jnp.arange(lat_len)
    ang_r = (pos // GRID_W).astype(F32)[:, None] * inv[None, :]
    ang_c = (pos % GRID_W).astype(F32)[:, None] * inv[None, :]
    cos_t = jnp.concatenate([jnp.ones((TM, HEAD_DIM), F32),
                             jnp.concatenate([jnp.cos(ang_r)] * 2 + [jnp.cos(ang_c)] * 2, axis=1)], axis=0)
    sin_t = jnp.concatenate([jnp.zeros((TM, HEAD_DIM), F32),
                             jnp.concatenate([-jnp.sin(ang_r), jnp.sin(ang_r),
                                              -jnp.sin(ang_c), jnp.sin(ang_c)], axis=1)], axis=0)

    def rope_row(i):
        return jnp.where(i * TM < tp, 0, 1 + ((i * TM - tp) % lat_len) // TM)

    new_k, new_v = [], []
    for l in range(depth):
        kind, j = l % 3, l // 3
        mod = mod_all[l]
        if kind == 0:
            y = _gmlp(y, mod, norm_mix[l], gm_w_in[j], gm_norm_v[j], gm_w_s[j], gm_b_s[j], gm_w_out[j], mod_row)
        elif kind == 1:
            q, k, v = _qkv(y, mod, norm_mix[l], at_w_qkv[j], at_q_norm[j], at_k_norm[j],
                           cos_t, sin_t, mod_row, rope_row)
            new_k.append(k[:tp].reshape(n_ctx, ctx_len, N_KV_HEADS, HEAD_DIM))
            new_v.append(v[:tp].reshape(n_ctx, ctx_len, N_KV_HEADS, HEAD_DIM))
            past = cache_k.shape[2]
            ck = cache_k[:, j].reshape(n_lat, past, N_KV_HEADS * HEAD_DIM)
            cv = cache_v[:, j].reshape(n_lat, past, N_KV_HEADS * HEAD_DIM)
            y = _attention(y, mod, q, k, v, ck, cv, at_w_o[j], n_ctx, ctx_len, n_lat, lat_len)
        else:
            y = _pool(y, mod, norm_mix[l], pool_w_grp[j], pool_scale[j], n_ctx, ctx_len, 0, lambda b: 0)
            y = _pool(y, mod, norm_mix[l], pool_w_grp[j], pool_scale[j], n_lat, lat_len, tp, lambda b: 1 + b)
        y = _moe(l, y, mod, norm_ffn[l], router_w[l], router_b[l], w_gate_up, b_gate_up,
                 w_down, b_down, mod_row, split_rows=tp if l == depth - 1 else None)

    y_ctx, y_lat = y
    return (y_ctx.reshape(n_ctx, ctx_len, d), y_lat.reshape(n_lat, lat_len, d),
            jnp.stack(new_k, axis=1), jnp.stack(new_v, axis=1))
```

```python
import functools

import jax
import jax.numpy as jnp
from jax import lax
from jax.experimental import pallas as pl
from jax.experimental.pallas import tpu as pltpu

F32 = jnp.float32
BF16 = jnp.bfloat16

LANES = 128
SUBLANES = 8
VMEM_LIMIT_BYTES = 56 * 1024 * 1024

EPS = 1e-6
GRID_W = 64
CHUNK = 128
HEAD_DIM = 128
N_KV_HEADS = 2
ROPE_THETA = 10000.0
POOL_WINDOWS = (2, 4, 8, 16)
TOP_K = 4
SWIGLU_ALPHA = 1.702
SWIGLU_LIMIT = 7.0

TM = 256
MOE_ROWS = 512
MOE_GRANULE = 64
UNIT = SUBLANES


def _params(n_axes=1):
    return pltpu.CompilerParams(dimension_semantics=("arbitrary",) * n_axes,
                                vmem_limit_bytes=VMEM_LIMIT_BYTES)


def _rms(x, gain):
    return x * lax.rsqrt(jnp.mean(x * x, axis=-1, keepdims=True) + EPS) * gain


def _norm_mod(y, gain, shift, scale):
    return _rms(y, gain) * (1.0 + scale) + shift


def _gelu_tanh(x):
    return 0.5 * x * (1.0 + jnp.tanh(0.7978845608028654 * (x + 0.044715 * (x * x * x))))


def _dot(a, b):
    return jnp.dot(a, b, preferred_element_type=F32)


def _dot_nt(a, b):
    return lax.dot_general(a, b, (((1,), (1,)), ((), ())), preferred_element_type=F32)


def _ada_kernel(cv_ref, w_ref, b_ref, o_ref):
    cv = cv_ref[...]
    s = cv * jax.nn.sigmoid(cv)
    w = w_ref[0]
    s_hi = s.astype(BF16)
    s_lo = (s - s_hi.astype(F32)).astype(BF16)
    w_hi = w.astype(BF16)
    w_lo = (w - w_hi.astype(F32)).astype(BF16)
    o_ref[0, 0] = _dot(s_hi, w_hi) + _dot(s_lo, w_hi) + _dot(s_hi, w_lo) + b_ref[0, 0]


def _ada_params(cvecs, w_mod, b_mod):
    depth, d, _ = w_mod.shape
    rows = cvecs.shape[0]
    return pl.pallas_call(
        _ada_kernel,
        grid=(depth, 6),
        in_specs=[
            pl.BlockSpec((rows, d), lambda l, j: (0, 0)),
            pl.BlockSpec((1, d, d), lambda l, j: (l, 0, j)),
            pl.BlockSpec((1, 1, 1, d), lambda l, j: (l, j, 0, 0)),
        ],
        out_specs=pl.BlockSpec((1, 1, rows, d), lambda l, j: (l, j, 0, 0)),
        out_shape=jax.ShapeDtypeStruct((depth, 6, rows, d), F32),
        compiler_params=_params(2),
        name="ada_params",
    )(cvecs, w_mod, b_mod.reshape(depth, 6, 1, d))


def _gmlp_kernel(n_first, *refs):
    if n_first is None:
        y = refs[0][...]
        refs = refs[1:]
    else:
        y = jnp.where(pl.program_id(0) < n_first, refs[0][...], refs[1][...])
        refs = refs[2:]
    mod_ref, gain_ref, win_ref, nv_ref, ws_ref, bexp_ref, wout_ref, o_ref = refs
    inner = nv_ref.shape[-1]
    groups = ws_ref.shape[0]
    h = _norm_mod(y, gain_ref[...], mod_ref[0, 0:1, :], mod_ref[0, 1:2, :])
    z = _gelu_tanh(_dot(h.astype(BF16), win_ref[...]))
    u = z[:, :inner]
    v = _rms(z[:, inner:], nv_ref[...]).astype(BF16)
    bexp = bexp_ref[...]
    chunks = []
    for c in range(y.shape[0] // CHUNK):
        cols = []
        for g in range(groups):
            vg = v[c * CHUNK:(c + 1) * CHUNK, g * LANES:(g + 1) * LANES]
            cols.append(_dot(ws_ref[g], vg))
        chunks.append(jnp.concatenate(cols, axis=1) + bexp)
    mixed = jnp.concatenate(chunks, axis=0)
    o = _dot((u * mixed).astype(BF16), wout_ref[...])
    o_ref[...] = y + mod_ref[0, 2:3, :] * o


def _gmlp(y, mod, gain, w_in, norm_v, w_s, b_s, w_out, mod_row):
    if isinstance(y, tuple):
        n_first = y[0].shape[0] // TM
        t_all, d = y[0].shape[0] + y[1].shape[0], y[0].shape[1]
        y_specs = [pl.BlockSpec((TM, d), lambda i: (jnp.minimum(i, n_first - 1), 0)),
                   pl.BlockSpec((TM, d), lambda i: (jnp.maximum(i - n_first, 0), 0))]
    else:
        n_first, y = None, (y,)
        t_all, d = y[0].shape
        y_specs = [pl.BlockSpec((TM, d), lambda i: (i, 0))]
    inner = norm_v.shape[-1]
    groups = w_s.shape[0]
    bexp = jnp.repeat(b_s.T, inner // groups, axis=1)
    return pl.pallas_call(
        functools.partial(_gmlp_kernel, n_first),
        grid=(t_all // TM,),
        in_specs=y_specs + [
            pl.BlockSpec((1, 6, d), lambda i: (mod_row(i), 0, 0)),
            pl.BlockSpec((1, d), lambda i: (0, 0)),
            pl.BlockSpec((d, 2 * inner), lambda i: (0, 0)),
            pl.BlockSpec((1, inner), lambda i: (0, 0)),
            pl.BlockSpec((groups, CHUNK, CHUNK), lambda i: (0, 0, 0)),
            pl.BlockSpec((CHUNK, inner), lambda i: (0, 0)),
            pl.BlockSpec((inner, d), lambda i: (0, 0)),
        ],
        out_specs=pl.BlockSpec((TM, d), lambda i: (i, 0)),
        out_shape=jax.ShapeDtypeStruct((t_all, d), F32),
        compiler_params=_params(),
        name="gmlp",
    )(*y, mod, gain.reshape(1, d), w_in.astype(BF16), norm_v.reshape(1, inner),
      w_s.astype(BF16), bexp, w_out.astype(BF16))


def _qkv_kernel(y_ref, mod_ref, gain_ref, w_ref, qn_ref, kn_ref, cos_ref, sin_ref,
                q_ref, k_ref, v_ref):
    y = y_ref[...]
    n_q = q_ref.shape[-1] // HEAD_DIM
    n_kv = k_ref.shape[-1] // HEAD_DIM
    h = _norm_mod(y, gain_ref[...], mod_ref[0, 0:1, :], mod_ref[0, 1:2, :])
    qkv = _dot(h.astype(BF16), w_ref[...])
    cos = cos_ref[...]
    sin = sin_ref[...]
    lane = lax.broadcasted_iota(jnp.int32, cos.shape, 1)
    low_half = (lane % (HEAD_DIM // 2)) < (HEAD_DIM // 4)

    def head(idx, norm):
        x = _rms(qkv[:, idx * HEAD_DIM:(idx + 1) * HEAD_DIM], norm)
        partner = jnp.where(low_half,
                            pltpu.roll(x, HEAD_DIM - HEAD_DIM // 4, 1),
                            pltpu.roll(x, HEAD_DIM // 4, 1))
        return x * cos + partner * sin

    qn = qn_ref[...]
    kn = kn_ref[...]
    q_ref[...] = jnp.concatenate([head(i, qn) for i in range(n_q)], axis=1).astype(q_ref.dtype)
    k_ref[...] = jnp.concatenate([head(n_q + i, kn) for i in range(n_kv)], axis=1)
    v_ref[...] = qkv[:, (n_q + n_kv) * HEAD_DIM:]


def _qkv(y, mod, gain, w_qkv, q_norm, k_norm, cos_t, sin_t, mod_row, rope_row):
    t_all, d = y.shape
    n_kv_cols = N_KV_HEADS * HEAD_DIM
    return pl.pallas_call(
        _qkv_kernel,
        grid=(t_all // TM,),
        in_specs=[
            pl.BlockSpec((TM, d), lambda i: (i, 0)),
            pl.BlockSpec((1, 6, d), lambda i: (mod_row(i), 0, 0)),
            pl.BlockSpec((1, d), lambda i: (0, 0)),
            pl.BlockSpec(w_qkv.shape, lambda i: (0, 0)),
            pl.BlockSpec((1, HEAD_DIM), lambda i: (0, 0)),
            pl.BlockSpec((1, HEAD_DIM), lambda i: (0, 0)),
            pl.BlockSpec((TM, HEAD_DIM), lambda i: (rope_row(i), 0)),
            pl.BlockSpec((TM, HEAD_DIM), lambda i: (rope_row(i), 0)),
        ],
        out_specs=[
            pl.BlockSpec((TM, d), lambda i: (i, 0)),
            pl.BlockSpec((TM, n_kv_cols), lambda i: (i, 0)),
            pl.BlockSpec((TM, n_kv_cols), lambda i: (i, 0)),
        ],
        out_shape=[
            jax.ShapeDtypeStruct((t_all, d), BF16),
            jax.ShapeDtypeStruct((t_all, n_kv_cols), F32),
            jax.ShapeDtypeStruct((t_all, n_kv_cols), F32),
        ],
        compiler_params=_params(),
        name="qkv_project",
    )(y, mod, gain.reshape(1, d), w_qkv.astype(BF16), q_norm.reshape(1, HEAD_DIM),
      k_norm.reshape(1, HEAD_DIM), cos_t, sin_t)


def _attend(q, key_sets):
    n_heads = q.shape[1] // HEAD_DIM
    rep = n_heads // N_KV_HEADS
    scale = HEAD_DIM ** -0.5
    outs = []
    for hd in range(n_heads):
        g = hd // rep
        qh = q[:, hd * HEAD_DIM:(hd + 1) * HEAD_DIM]
        cols = slice(g * HEAD_DIM, (g + 1) * HEAD_DIM)
        scores = [_dot_nt(qh, k[:, cols]) * scale for k, _ in key_sets]
        m = functools.reduce(jnp.maximum, [jnp.max(s, axis=-1, keepdims=True) for s in scores])
        probs = [jnp.exp(s - m) for s in scores]
        denom = functools.reduce(jnp.add, [jnp.sum(p, axis=-1, keepdims=True) for p in probs])
        o = functools.reduce(jnp.add, [_dot(p.astype(BF16), v[:, cols])
                                       for p, (_, v) in zip(probs, key_sets)])
        outs.append(o / denom)
    return jnp.concatenate(outs, axis=1)


def _attn_ctx_kernel(y_ref, mod_ref, q_ref, k_ref, v_ref, wo_ref, o_ref):
    att = _attend(q_ref[...], [(k_ref[...].astype(BF16), v_ref[...].astype(BF16))])
    o_ref[...] = y_ref[...] + mod_ref[0, 2:3, :] * _dot(att.astype(BF16), wo_ref[...])


def _attn_lat_kernel(y_ref, mod_ref, q_ref, k_ref, v_ref, ck_ref, cv_ref, wo_ref, o_ref):
    sets = [(ck_ref[0].astype(BF16), cv_ref[0].astype(BF16)),
            (k_ref[...].astype(BF16), v_ref[...].astype(BF16))]
    att = _attend(q_ref[...], sets)
    o_ref[...] = y_ref[...] + mod_ref[0, 2:3, :] * _dot(att.astype(BF16), wo_ref[...])


def _attention(y, mod, q, k, v, cache_k, cache_v, w_o, n_ctx, ctx_len, n_lat, lat_len):
    t_all, d = y.shape
    kvc = k.shape[1]
    tp = n_ctx * ctx_len
    wo = w_o.astype(BF16)
    y = pl.pallas_call(
        _attn_ctx_kernel,
        grid=(n_ctx,),
        in_specs=[
            pl.BlockSpec((ctx_len, d), lambda b: (b, 0)),
            pl.BlockSpec((1, 6, d), lambda b: (0, 0, 0)),
            pl.BlockSpec((ctx_len, d), lambda b: (b, 0)),
            pl.BlockSpec((ctx_len, kvc), lambda b: (b, 0)),
            pl.BlockSpec((ctx_len, kvc), lambda b: (b, 0)),
            pl.BlockSpec((d, d), lambda b: (0, 0)),
        ],
        out_specs=pl.BlockSpec((ctx_len, d), lambda b: (b, 0)),
        out_shape=jax.ShapeDtypeStruct((t_all, d), F32),
        input_output_aliases={0: 0},
        compiler_params=_params(),
        name="attn_context",
    )(y, mod, q, k, v, wo)
    tq = TM
    qb = lat_len // tq
    past = cache_k.shape[1]
    y = pl.pallas_call(
        _attn_lat_kernel,
        grid=(n_lat, qb),
        in_specs=[
            pl.BlockSpec((tq, d), lambda b, i: (tp // tq + b * qb + i, 0)),
            pl.BlockSpec((1, 6, d), lambda b, i: (1 + b, 0, 0)),
            pl.BlockSpec((tq, d), lambda b, i: (tp // tq + b * qb + i, 0)),
            pl.BlockSpec((lat_len, kvc), lambda b, i: (tp // lat_len + b, 0)),
            pl.BlockSpec((lat_len, kvc), lambda b, i: (tp // lat_len + b, 0)),
            pl.BlockSpec((1, past, kvc), lambda b, i: (b, 0, 0)),
            pl.BlockSpec((1, past, kvc), lambda b, i: (b, 0, 0)),
            pl.BlockSpec((d, d), lambda b, i: (0, 0)),
        ],
        out_specs=pl.BlockSpec((tq, d), lambda b, i: (tp // tq + b * qb + i, 0)),
        out_shape=jax.ShapeDtypeStruct((t_all, d), F32),
        input_output_aliases={0: 0},
        compiler_params=_params(2),
        name="attn_latent",
    )(y, mod, q, k, v, cache_k, cache_v, wo)
    return y


def _pool_kernel(y_ref, mod_ref, gain_ref, wg_ref, scale_ref, o_ref):
    y = y_ref[...]
    s_len, d = y.shape
    gd = d // len(POOL_WINDOWS)
    h = _norm_mod(y, gain_ref[...], mod_ref[0, 0:1, :], mod_ref[0, 1:2, :])
    t_idx = lax.broadcasted_iota(jnp.int32, (s_len, s_len), 0)
    s_idx = lax.broadcasted_iota(jnp.int32, (s_len, s_len), 1)
    t_col = lax.broadcasted_iota(jnp.int32, (s_len, 1), 0)
    parts = []
    for g, w in enumerate(POOL_WINDOWS):
        lo = t_idx - w // 2
        window = ((s_idx >= lo) & (s_idx < lo + w)).astype(BF16)
        count = (jnp.minimum(t_col - w // 2 + w, s_len) - jnp.maximum(t_col - w // 2, 0)).astype(F32)
        hg = h[:, g * gd:(g + 1) * gd]
        h_hi = hg.astype(BF16)
        h_lo = (hg - h_hi.astype(F32)).astype(BF16)
        mean = (_dot(window, h_hi) + _dot(window, h_lo)) / count
        parts.append(_dot((mean - hg).astype(BF16), wg_ref[g]))
    mixed = jnp.concatenate(parts, axis=1) * scale_ref[...]
    o_ref[...] = y + mod_ref[0, 2:3, :] * mixed


def _pool(y, mod, gain, w_grp, scale, n_seq, s_len, row_off, mod_off):
    t_all, d = y.shape
    blk_off = row_off // s_len
    return pl.pallas_call(
        _pool_kernel,
        grid=(n_seq,),
        in_specs=[
            pl.BlockSpec((s_len, d), lambda b: (blk_off + b, 0)),
            pl.BlockSpec((1, 6, d), lambda b: (mod_off(b), 0, 0)),
            pl.BlockSpec((1, d), lambda b: (0, 0)),
            pl.BlockSpec(w_grp.shape, lambda b: (0, 0, 0)),
            pl.BlockSpec((1, d), lambda b: (0, 0)),
        ],
        out_specs=pl.BlockSpec((s_len, d), lambda b: (blk_off + b, 0)),
        out_shape=jax.ShapeDtypeStruct((t_all, d), F32),
        input_output_aliases={0: 0},
        compiler_params=_params(),
        name="pool_mixer",
    )(y, mod, gain.reshape(1, d), w_grp.astype(BF16), scale.reshape(1, d))


def _sorted_rows(n_exp):
    worst = TM * TOP_K + n_exp * (UNIT - 1) + UNIT
    return -(-worst // LANES) * LANES


def _route_kernel(y_ref, mod_ref, gain_ref, rw_ref, rb_ref, xs_ref, route_ref, cnt_ref):
    y = y_ref[...]
    tm = y.shape[0]
    n_rows = xs_ref.shape[0]
    h = _norm_mod(y, gain_ref[...], mod_ref[0, 3:4, :], mod_ref[0, 4:5, :])
    h_hi = h.astype(BF16)
    h_lo = (h - h_hi.astype(F32)).astype(BF16)
    hi_terms = _dot(h_hi, rw_ref[...])
    logits = hi_terms[:, :LANES] + hi_terms[:, LANES:] + _dot(h_lo, rw_ref[:, :LANES]) + rb_ref[...]
    lane = lax.broadcasted_iota(jnp.int32, logits.shape, 1).astype(F32)
    work = logits
    vals, ids = [], []
    for _ in range(TOP_K):
        m = jnp.max(work, axis=-1, keepdims=True)
        idx = jnp.min(jnp.where(work == m, lane, float(LANES)), axis=-1, keepdims=True)
        vals.append(m)
        ids.append(idx)
        work = jnp.where(lane == idx, -jnp.inf, work)
    exps = [jnp.exp(v - vals[0]) for v in vals]
    denom = functools.reduce(jnp.add, exps)
    onehot = functools.reduce(jnp.add, [(lane == idx).astype(F32) for idx in ids])

    counts = jnp.sum(onehot, axis=0, keepdims=True)
    seg = jnp.ceil(counts * (1.0 / UNIT)) * UNIT
    e_row = lax.broadcasted_iota(jnp.int32, (LANES, LANES), 0)
    e_col = lax.broadcasted_iota(jnp.int32, (LANES, LANES), 1)
    earlier = (e_row < e_col).astype(BF16)
    seg_off = _dot(jnp.broadcast_to(seg, (SUBLANES, LANES)).astype(BF16), earlier)[0:1, :]
    row = lax.broadcasted_iota(jnp.int32, (tm, tm), 0)
    col = lax.broadcasted_iota(jnp.int32, (tm, tm), 1)
    before = (col < row).astype(BF16)
    pos_mat = _dot(before, onehot.astype(BF16)) + seg_off

    route = jnp.zeros(logits.shape, F32)
    for k in range(TOP_K):
        pos_k = jnp.sum(jnp.where(lane == ids[k], pos_mat, 0.0), axis=-1, keepdims=True)
        route = jnp.where(lane == k, exps[k] / denom, route)
        route = jnp.where(lane == TOP_K + k, ids[k], route)
        route = jnp.where(lane == 2 * TOP_K + k, pos_k, route)
    route_ref[...] = route
    cnt_ref[...] = jnp.broadcast_to(counts, cnt_ref.shape)

    route_t = jnp.transpose(route)
    out_row = lax.broadcasted_iota(jnp.int32, (n_rows, tm), 0).astype(F32)
    perm = functools.reduce(jnp.add, [(out_row == route_t[2 * TOP_K + k:2 * TOP_K + k + 1, :]).astype(F32)
                                      for k in range(TOP_K)])
    xs_ref[...] = _dot(perm.astype(BF16), h_hi)


def _route(y, mod, gain, router_w, router_b, mod_row):
    t_all, d = y.shape
    n_exp = router_w.shape[1]
    n_tiles = t_all // TM
    n_rows = _sorted_rows(n_exp)
    rw = jnp.pad(router_w, ((0, 0), (0, LANES - n_exp)))
    rw_hi = rw.astype(BF16)
    rw = jnp.concatenate([rw_hi, (rw - rw_hi.astype(F32)).astype(BF16)], axis=1)
    rb = jnp.pad(router_b, (0, LANES - n_exp), constant_values=-1e30).reshape(1, LANES)
    return pl.pallas_call(
        _route_kernel,
        grid=(n_tiles,),
        in_specs=[
            pl.BlockSpec((TM, d), lambda i: (i, 0)),
            pl.BlockSpec((1, 6, d), lambda i: (mod_row(i), 0, 0)),
            pl.BlockSpec((1, d), lambda i: (0, 0)),
            pl.BlockSpec((d, 2 * LANES), lambda i: (0, 0)),
            pl.BlockSpec((1, LANES), lambda i: (0, 0)),
        ],
        out_specs=[
            pl.BlockSpec((n_rows, d), lambda i: (i, 0)),
            pl.BlockSpec((TM, LANES), lambda i: (i, 0)),
            pl.BlockSpec((SUBLANES, LANES), lambda i: (i, 0)),
        ],
        out_shape=[
            jax.ShapeDtypeStruct((n_tiles * n_rows + 2 * MOE_ROWS, d), F32),
            jax.ShapeDtypeStruct((t_all, LANES), F32),
            jax.ShapeDtypeStruct((n_tiles * SUBLANES, LANES), F32),
        ],
        compiler_params=_params(),
        name="moe_route",
    )(y, mod, gain.reshape(1, d), rw, rb)


def _plan_kernel(n_tiles, n_exp, n_rows, c_ref, srcx_ref, dstx_ref, be_ref, nxt_ref, nu_ref):
    nb = srcx_ref.shape[0]
    seg = jnp.ceil(c_ref[...] * (1.0 / UNIT)) * UNIT
    seg_b = seg.astype(BF16)
    sq_r = lax.broadcasted_iota(jnp.int32, (LANES, LANES), 0)
    sq_c = lax.broadcasted_iota(jnp.int32, (LANES, LANES), 1)
    upto = (sq_r <= sq_c).astype(BF16)
    seg_end_t = _dot(seg_b, upto)
    seg_off_t = seg_end_t - seg
    seg_end_e = _dot((sq_c <= sq_r).astype(BF16), seg_b)
    seg_off_e = seg_end_e - seg
    rows_e = seg_end_e[n_tiles - 1:n_tiles, :]
    nblk = jnp.ceil(rows_e * (1.0 / MOE_ROWS))
    end_blk = _dot(jnp.broadcast_to(nblk, (SUBLANES, LANES)).astype(BF16), upto)[0:1, :]
    start_blk = end_blk - nblk
    n_used = jnp.max(end_blk, axis=-1, keepdims=True)

    lane1 = lax.broadcasted_iota(jnp.int32, (1, LANES), 1).astype(F32)
    lane = lax.broadcasted_iota(jnp.int32, (nb, LANES), 1).astype(F32)
    blk = lax.broadcasted_iota(jnp.int32, (nb, LANES), 0).astype(F32)
    is_exp = lane1 < n_exp
    has_rows = (nblk > 0) & is_exp
    be = jnp.minimum(jnp.sum(((end_blk <= blk) & is_exp).astype(F32), axis=-1, keepdims=True), n_exp - 1.0)
    last_e = jnp.max(jnp.where(has_rows, lane1, 0.0), axis=-1, keepdims=True)
    used = blk[:, 0:1] < n_used
    be = jnp.where(used, be, last_e)
    nxt = jnp.min(jnp.where((lane > be) & has_rows, lane, float(LANES)), axis=-1, keepdims=True)
    nxt = jnp.where(nxt == LANES, be, nxt)
    onehot = (lane == be).astype(F32)

    def column_of_block(x):
        hi = jnp.floor(x * (1.0 / 256.0))
        lo = x - hi * 256.0
        oh = onehot.astype(BF16)
        return _dot_nt(oh, hi.astype(BF16)) * 256.0 + _dot_nt(oh, lo.astype(BF16))

    ends_b = column_of_block(seg_end_e)
    shift_b = column_of_block(seg_off_t) - column_of_block(seg_off_e)
    start_b = jnp.sum(onehot * start_blk, axis=-1, keepdims=True)
    rows_b = jnp.sum(onehot * rows_e, axis=-1, keepdims=True)
    r = (blk - start_b) * MOE_ROWS + lane * UNIT
    tile_of = jnp.zeros((nb, LANES), F32)
    for t in range(n_tiles):
        tile_of = tile_of + (ends_b[:, t:t + 1] <= r).astype(F32)
    tile_c = jnp.minimum(tile_of, n_tiles - 1.0)
    src = jnp.zeros((nb, LANES), F32)
    for t in range(n_tiles):
        src = src + jnp.where(tile_c == t, shift_b[:, t:t + 1] + float(t * n_rows), 0.0)
    valid = (r < rows_b) & used & (lane < MOE_ROWS // UNIT)
    srcx_ref[...] = jnp.where(valid, src + r, float(n_rows - UNIT)).astype(jnp.int32)
    spare = float(n_tiles * n_rows) + (blk - 2.0 * jnp.floor(blk * 0.5)) * MOE_ROWS + lane * UNIT
    dstx_ref[...] = jnp.where(valid, src + r, spare).astype(jnp.int32)

    rows_here = jnp.clip(rows_b - (blk[:, 0:1] - start_b) * MOE_ROWS, 0.0, float(MOE_ROWS))
    be_ref[...] = jnp.where(lane == 1, rows_here, be).astype(jnp.int32)
    nxt_ref[...] = jnp.broadcast_to(nxt, nxt_ref.shape).astype(jnp.int32)
    nu_ref[...] = jnp.broadcast_to(n_used, nu_ref.shape).astype(jnp.int32)


def _plan(counts, n_tiles, n_exp, n_rows, n_blocks):
    assert n_tiles <= LANES and n_exp <= LANES and n_tiles % SUBLANES == 0
    blk_units = MOE_ROWS // UNIT
    assert blk_units <= LANES
    nb = -(-n_blocks // SUBLANES) * SUBLANES
    srcx, dstx, be, nxt, nu = pl.pallas_call(
        functools.partial(_plan_kernel, n_tiles, n_exp, n_rows),
        out_shape=[
            jax.ShapeDtypeStruct((nb, LANES), jnp.int32),
            jax.ShapeDtypeStruct((nb, LANES), jnp.int32),
            jax.ShapeDtypeStruct((nb, LANES), jnp.int32),
            jax.ShapeDtypeStruct((nb, LANES), jnp.int32),
            jax.ShapeDtypeStruct((SUBLANES, LANES), jnp.int32),
        ],
        compiler_params=pltpu.CompilerParams(vmem_limit_bytes=VMEM_LIMIT_BYTES),
        name="moe_plan",
    )(counts)
    before = n_tiles * n_rows + MOE_ROWS + UNIT * jnp.arange(blk_units, dtype=jnp.int32)
    dst = jnp.concatenate([before, dstx[:n_blocks, :blk_units].reshape(-1)])
    return (srcx[:n_blocks, :blk_units].reshape(-1), dst, be[:n_blocks, 0], be[:n_blocks, 1],
            nxt[:n_blocks, 0], nu[0, :1])


def _gather_units(src_ref, base, table_ref, dst, sem):
    for u in range(dst.shape[0] // UNIT):
        start = pl.multiple_of(table_ref[base + u], UNIT)
        pltpu.make_async_copy(src_ref.at[pl.ds(start, UNIT)], dst.at[pl.ds(u * UNIT, UNIT)], sem).start()


def _scatter_units(src, base, table_ref, dst_ref, sem):
    for u in range(src.shape[0] // UNIT):
        start = pl.multiple_of(table_ref[base + u], UNIT)
        pltpu.make_async_copy(src.at[pl.ds(u * UNIT, UNIT)], dst_ref.at[pl.ds(start, UNIT)], sem).start()


def _wait_units(buf, sem):
    pltpu.make_async_copy(buf, buf, sem).wait()


def _expert_kernel(layer, be_ref, rows_ref, nu_ref, src_ref, dst_ref, nxt_ref, xs_ref, wgu_hbm, bgu_ref, wdn_hbm,
                   bdn_ref, ys_ref, wgu_st, wdn_st, wgu_bf, wdn_bf, x0, x1, y0, y1, gsem, ssem, wsem):
    i = pl.program_id(0)
    n_used = nu_ref[0]
    n_blocks = pl.num_programs(0)
    units = x0.shape[0] // UNIT
    d_exp = wdn_bf.shape[0]

    def weight_copies(e):
        return (pltpu.make_async_copy(wgu_hbm.at[layer, e], wgu_st, wsem.at[0]),
                pltpu.make_async_copy(wdn_hbm.at[layer, e], wdn_st, wsem.at[1]))

    def run_block(xcur, xoth, ycur, yoth, p, m):
        _wait_units(xcur, gsem.at[p])

        @pl.when(i >= 1)
        def _():
            _wait_units(ycur, ssem.at[p])

        _gather_units(xs_ref, jnp.minimum(i + 1, n_blocks - 1) * units, src_ref, xoth, gsem.at[1 - p])
        _scatter_units(yoth, i * units, dst_ref, ys_ref, ssem.at[1 - p])
        x = xcur[0:m, :].astype(BF16)
        gu = _dot(x, wgu_bf[...]) + bgu_ref[0, 0]
        gate = jnp.minimum(gu[:, :d_exp], SWIGLU_LIMIT)
        up = jnp.clip(gu[:, d_exp:], -SWIGLU_LIMIT, SWIGLU_LIMIT)
        hid = (up + 1.0) * gate * jax.nn.sigmoid(SWIGLU_ALPHA * gate)
        ycur[0:m, :] = _dot(hid.astype(BF16), wdn_bf[...]) + bdn_ref[0, 0]

        @pl.when(i == n_used - 1)
        def _():
            _wait_units(xoth, gsem.at[1 - p])
            _scatter_units(ycur, (i + 1) * units, dst_ref, ys_ref, ssem.at[p])
            _wait_units(yoth, ssem.at[1 - p])
            _wait_units(ycur, ssem.at[p])

    @pl.when(i < n_used)
    def _():
        @pl.when(i == 0)
        def _():
            for cp in weight_copies(be_ref[0]):
                cp.start(priority=1)
            _gather_units(xs_ref, 0, src_ref, x0, gsem.at[0])
            y0[...] = jnp.zeros_like(y0)
            y1[...] = jnp.zeros_like(y1)

        new_expert = jnp.logical_or(i == 0, be_ref[i] != be_ref[jnp.maximum(i - 1, 0)])

        @pl.when(new_expert)
        def _():
            for cp in weight_copies(be_ref[i]):
                cp.wait()
            wgu_bf[...] = wgu_st[...].astype(BF16)
            wdn_bf[...] = wdn_st[...].astype(BF16)

            @pl.when(nxt_ref[i] != be_ref[i])
            def _():
                for cp in weight_copies(nxt_ref[i]):
                    cp.start(priority=1)

        n_gran = x0.shape[0] // MOE_GRANULE
        granules = jnp.clip((rows_ref[i] + MOE_GRANULE - 1) // MOE_GRANULE, 1, n_gran)
        for g in range(1, n_gran + 1):
            @pl.when(jnp.logical_and(i % 2 == 0, granules == g))
            def _():
                run_block(x0, x1, y0, y1, 0, g * MOE_GRANULE)

            @pl.when(jnp.logical_and(i % 2 == 1, granules == g))
            def _():
                run_block(x1, x0, y1, y0, 1, g * MOE_GRANULE)


def _experts(layer, xs, src_units, dst_units, block_e, block_rows, next_e, n_used, w_gu, b_gu, w_dn, b_dn,
             n_blocks):
    d = w_dn.shape[-1]
    depth, n_exp, _, two_f = w_gu.shape
    d_exp = two_f // 2
    n_prefetch = 6

    def bias_block(i, be, rows, nu, src, dst, nxt):
        return (layer, be[i], 0, 0)

    grid_spec = pltpu.PrefetchScalarGridSpec(
        num_scalar_prefetch=n_prefetch,
        grid=(n_blocks,),
        in_specs=[
            pl.BlockSpec(memory_space=pl.ANY),
            pl.BlockSpec(memory_space=pl.ANY),
            pl.BlockSpec((1, 1, 1, two_f), bias_block),
            pl.BlockSpec(memory_space=pl.ANY),
            pl.BlockSpec((1, 1, 1, d), bias_block),
        ],
        out_specs=pl.BlockSpec(memory_space=pl.ANY),
        scratch_shapes=[
            pltpu.VMEM((d, two_f), F32),
            pltpu.VMEM((d_exp, d), F32),
            pltpu.VMEM((d, two_f), BF16),
            pltpu.VMEM((d_exp, d), BF16),
            pltpu.VMEM((MOE_ROWS, d), F32),
            pltpu.VMEM((MOE_ROWS, d), F32),
            pltpu.VMEM((MOE_ROWS, d), F32),
            pltpu.VMEM((MOE_ROWS, d), F32),
            pltpu.SemaphoreType.DMA((2,)),
            pltpu.SemaphoreType.DMA((2,)),
            pltpu.SemaphoreType.DMA((2,)),
        ],
    )
    return pl.pallas_call(
        functools.partial(_expert_kernel, layer),
        grid_spec=grid_spec,
        out_shape=jax.ShapeDtypeStruct(xs.shape, xs.dtype),
        input_output_aliases={n_prefetch: 0},
        compiler_params=_params(),
        name="moe_experts",
    )(block_e, block_rows, n_used, src_units, dst_units, next_e, xs, w_gu, b_gu.reshape(depth, n_exp, 1, two_f),
      w_dn, b_dn.reshape(depth, n_exp, 1, d))


def _combine_kernel(n_first, y_ref, mod_ref, route_ref, ys_ref, *o_refs):
    route = route_ref[...]
    tm = route.shape[0]
    n_rows = ys_ref.shape[0]
    col = lax.broadcasted_iota(jnp.int32, (tm, n_rows), 1).astype(F32)
    weights = functools.reduce(jnp.add, [
        jnp.where(col == route[:, 2 * TOP_K + k:2 * TOP_K + k + 1], route[:, k:k + 1], 0.0)
        for k in range(TOP_K)])
    acc = _dot(weights.astype(BF16), ys_ref[...].astype(BF16))
    out = y_ref[...] + mod_ref[0, 5:6, :] * acc
    if n_first is None:
        o_refs[0][...] = out
    else:
        first = pl.program_id(0) < n_first

        @pl.when(first)
        def _():
            o_refs[0][...] = out

        @pl.when(jnp.logical_not(first))
        def _():
            o_refs[1][...] = out


def _combine(y, mod, route, ys, n_rows, mod_row, split_rows=None):
    t_all, d = y.shape
    if split_rows is None:
        n_first = None
        out_specs = pl.BlockSpec((TM, d), lambda i: (i, 0))
        out_shape = jax.ShapeDtypeStruct((t_all, d), F32)
    else:
        n_first = split_rows // TM
        out_specs = [pl.BlockSpec((TM, d), lambda i: (jnp.minimum(i, n_first - 1), 0)),
                     pl.BlockSpec((TM, d), lambda i: (jnp.maximum(i - n_first, 0), 0))]
        out_shape = [jax.ShapeDtypeStruct((split_rows, d), F32),
                     jax.ShapeDtypeStruct((t_all - split_rows, d), F32)]
    return pl.pallas_call(
        functools.partial(_combine_kernel, n_first),
        grid=(t_all // TM,),
        in_specs=[
            pl.BlockSpec((TM, d), lambda i: (i, 0)),
            pl.BlockSpec((1, 6, d), lambda i: (mod_row(i), 0, 0)),
            pl.BlockSpec((TM, LANES), lambda i: (i, 0)),
            pl.BlockSpec((n_rows, d), lambda i: (i, 0)),
        ],
        out_specs=out_specs,
        out_shape=out_shape,
        compiler_params=_params(),
        name="moe_combine",
    )(y, mod, route, ys)


def _moe(layer, y, mod, gain, router_w, router_b, w_gu, b_gu, w_dn, b_dn, mod_row, split_rows=None):
    t_all, d = y.shape
    n_exp = router_w.shape[1]
    n_tiles = t_all // TM
    n_rows = _sorted_rows(n_exp)
    n_blocks = -(-(t_all * TOP_K + n_tiles * n_exp * (UNIT - 1)) // MOE_ROWS) + n_exp

    xs, route, cnt = _route(y, mod, gain, router_w, router_b, mod_row)
    counts = jnp.pad(cnt.reshape(n_tiles, SUBLANES, LANES)[:, 0, :], ((0, LANES - n_tiles), (0, 0)))
    src_x, dst_x, block_e, block_rows, next_e, n_used = _plan(counts, n_tiles, n_exp, n_rows, n_blocks)
    ys = _experts(layer, xs, src_x, dst_x, block_e, block_rows, next_e, n_used, w_gu, b_gu, w_dn, b_dn, n_blocks)
    return _combine(y, mod, route, ys, n_rows, mod_row, split_rows)


def kernel(x_prompt, x_sample, cache_k, cache_v, c, c_ctx, w_mod, b_mod, norm_mix, norm_ffn, gm_w_in, gm_norm_v, gm_w_s, gm_b_s, gm_w_out, at_w_qkv, at_q_norm, at_k_norm, at_w_o, pool_w_grp, pool_scale, router_w, router_b, w_gate_up, b_gate_up, w_down, b_down):
    n_ctx, ctx_len, d = x_prompt.shape
    n_lat, lat_len, _ = x_sample.shape
    depth = w_mod.shape[0]
    tp = n_ctx * ctx_len
    assert tp % lat_len == 0 and ctx_len % TM == 0 and lat_len % TM == 0 and ctx_len == TM
    assert 1 + n_lat <= SUBLANES

    assert depth >= 1
    y = (x_prompt.reshape(tp, d), x_sample.reshape(n_lat * lat_len, d))

    cvecs = jnp.zeros((SUBLANES, d), F32).at[0].set(c_ctx).at[1:1 + n_lat].set(c)
    mod_all = jnp.transpose(_ada_params(cvecs, w_mod, b_mod), (0, 2, 1, 3))

    def mod_row(i):
        return jnp.where(i * TM < tp, 0, 1 + (i * TM - tp) // lat_len)

    half = HEAD_DIM // 4
    inv = ROPE_THETA ** (-jnp.arange(half, dtype=F32) / half)
    pos = jnp.arange(lat_len)
    ang_r = (pos // GRID_W).astype(F32)[:, None] * inv[None, :]
    ang_c = (pos % GRID_W).astype(F32)[:, None] * inv[None, :]
    cos_t = jnp.concatenate([jnp.ones((TM, HEAD_DIM), F32),
                             jnp.concatenate([jnp.cos(ang_r)] * 2 + [jnp.cos(ang_c)] * 2, axis=1)], axis=0)
    sin_t = jnp.concatenate([jnp.zeros((TM, HEAD_DIM), F32),
                             jnp.concatenate([-jnp.sin(ang_r), jnp.sin(ang_r),
                                              -jnp.sin(ang_c), jnp.sin(ang_c)], axis=1)], axis=0)

    def rope_row(i):
        return jnp.where(i * TM < tp, 0, 1 + ((i * TM - tp) % lat_len) // TM)

    new_k, new_v = [], []
    for l in range(depth):
        kind, j = l % 3, l // 3
        mod = mod_all[l]
        if kind == 0:
            y = _gmlp(y, mod, norm_mix[l], gm_w_in[j], gm_norm_v[j], gm_w_s[j], gm_b_s[j], gm_w_out[j], mod_row)
        elif kind == 1:
            q, k, v = _qkv(y, mod, norm_mix[l], at_w_qkv[j], at_q_norm[j], at_k_norm[j],
                           cos_t, sin_t, mod_row, rope_row)
            new_k.append(k[:tp].reshape(n_ctx, ctx_len, N_KV_HEADS, HEAD_DIM))
            new_v.append(v[:tp].reshape(n_ctx, ctx_len, N_KV_HEADS, HEAD_DIM))
            past = cache_k.shape[2]
            ck = cache_k[:, j].reshape(n_lat, past, N_KV_HEADS * HEAD_DIM)
            cv = cache_v[:, j].reshape(n_lat, past, N_KV_HEADS * HEAD_DIM)
            y = _attention(y, mod, q, k, v, ck, cv, at_w_o[j], n_ctx, ctx_len, n_lat, lat_len)
        else:
            y = _pool(y, mod, norm_mix[l], pool_w_grp[j], pool_scale[j], n_ctx, ctx_len, 0, lambda b: 0)
            y = _pool(y, mod, norm_mix[l], pool_w_grp[j], pool_scale[j], n_lat, lat_len, tp, lambda b: 1 + b)
        y = _moe(l, y, mod, norm_ffn[l], router_w[l], router_b[l], w_gate_up, b_gate_up,
                 w_down, b_down, mod_row, split_rows=tp if l == depth - 1 else None)

    y_ctx, y_lat = y
    return (y_ctx.reshape(n_ctx, ctx_len, d), y_lat.reshape(n_lat, lat_len, d),
            jnp.stack(new_k, axis=1), jnp.stack(new_v, axis=1))
```

```python
import functools

import jax
import jax.numpy as jnp
from jax import lax
from jax.experimental import pallas as pl
from jax.experimental.pallas import tpu as pltpu

F32 = jnp.float32
BF16 = jnp.bfloat16

LANES = 128
SUBLANES = 8
VMEM_LIMIT_BYTES = 56 * 1024 * 1024

EPS = 1e-6
GRID_W = 64
CHUNK = 128
HEAD_DIM = 128
N_KV_HEADS = 2
ROPE_THETA = 10000.0
POOL_WINDOWS = (2, 4, 8, 16)
TOP_K = 4
SWIGLU_ALPHA = 1.702
SWIGLU_LIMIT = 7.0

TM = 256
MOE_ROWS = 512
MOE_GRANULE = 128
UNIT = SUBLANES


def _params(n_axes=1):
    return pltpu.CompilerParams(dimension_semantics=("arbitrary",) * n_axes,
                                vmem_limit_bytes=VMEM_LIMIT_BYTES)


def _rms(x, gain):
    return x * lax.rsqrt(jnp.mean(x * x, axis=-1, keepdims=True) + EPS) * gain


def _norm_mod(y, gain, shift, scale):
    return _rms(y, gain) * (1.0 + scale) + shift


def _gelu_tanh(x):
    return 0.5 * x * (1.0 + jnp.tanh(0.7978845608028654 * (x + 0.044715 * (x * x * x))))


def _dot(a, b):
    return jnp.dot(a, b, preferred_element_type=F32)


def _dot_nt(a, b):
    return lax.dot_general(a, b, (((1,), (1,)), ((), ())), preferred_element_type=F32)


def _ada_kernel(cv_ref, w_ref, b_ref, o_ref):
    cv = cv_ref[...]
    s = cv * jax.nn.sigmoid(cv)
    w = w_ref[0]
    s_hi = s.astype(BF16)
    s_lo = (s - s_hi.astype(F32)).astype(BF16)
    w_hi = w.astype(BF16)
    w_lo = (w - w_hi.astype(F32)).astype(BF16)
    o_ref[0, 0] = _dot(s_hi, w_hi) + _dot(s_lo, w_hi) + _dot(s_hi, w_lo) + b_ref[0, 0]


def _ada_params(cvecs, w_mod, b_mod):
    depth, d, _ = w_mod.shape
    rows = cvecs.shape[0]
    return pl.pallas_call(
        _ada_kernel,
        grid=(depth, 6),
        in_specs=[
            pl.BlockSpec((rows, d), lambda l, j: (0, 0)),
            pl.BlockSpec((1, d, d), lambda l, j: (l, 0, j)),
            pl.BlockSpec((1, 1, 1, d), lambda l, j: (l, j, 0, 0)),
        ],
        out_specs=pl.BlockSpec((1, 1, rows, d), lambda l, j: (l, j, 0, 0)),
        out_shape=jax.ShapeDtypeStruct((depth, 6, rows, d), F32),
        compiler_params=_params(2),
        name="ada_params",
    )(cvecs, w_mod, b_mod.reshape(depth, 6, 1, d))


def _gmlp_kernel(n_first, *refs):
    if n_first is None:
        y = refs[0][...]
        refs = refs[1:]
    else:
        y = jnp.where(pl.program_id(0) < n_first, refs[0][...], refs[1][...])
        refs = refs[2:]
    mod_ref, gain_ref, win_ref, nv_ref, ws_ref, bexp_ref, wout_ref, o_ref = refs
    inner = nv_ref.shape[-1]
    groups = ws_ref.shape[0]
    h = _norm_mod(y, gain_ref[...], mod_ref[0, 0:1, :], mod_ref[0, 1:2, :])
    z = _gelu_tanh(_dot(h.astype(BF16), win_ref[...]))
    u = z[:, :inner]
    v = _rms(z[:, inner:], nv_ref[...]).astype(BF16)
    bexp = bexp_ref[...]
    chunks = []
    for c in range(y.shape[0] // CHUNK):
        cols = []
        for g in range(groups):
            vg = v[c * CHUNK:(c + 1) * CHUNK, g * LANES:(g + 1) * LANES]
            cols.append(_dot(ws_ref[g], vg))
        chunks.append(jnp.concatenate(cols, axis=1) + bexp)
    mixed = jnp.concatenate(chunks, axis=0)
    o = _dot((u * mixed).astype(BF16), wout_ref[...])
    o_ref[...] = y + mod_ref[0, 2:3, :] * o


def _gmlp(y, mod, gain, w_in, norm_v, w_s, b_s, w_out, mod_row):
    if isinstance(y, tuple):
        n_first = y[0].shape[0] // TM
        t_all, d = y[0].shape[0] + y[1].shape[0], y[0].shape[1]
        y_specs = [pl.BlockSpec((TM, d), lambda i: (jnp.minimum(i, n_first - 1), 0)),
                   pl.BlockSpec((TM, d), lambda i: (jnp.maximum(i - n_first, 0), 0))]
    else:
        n_first, y = None, (y,)
        t_all, d = y[0].shape
        y_specs = [pl.BlockSpec((TM, d), lambda i: (i, 0))]
    inner = norm_v.shape[-1]
    groups = w_s.shape[0]
    bexp = jnp.repeat(b_s.T, inner // groups, axis=1)
    return pl.pallas_call(
        functools.partial(_gmlp_kernel, n_first),
        grid=(t_all // TM,),
        in_specs=y_specs + [
            pl.BlockSpec((1, 6, d), lambda i: (mod_row(i), 0, 0)),
            pl.BlockSpec((1, d), lambda i: (0, 0)),
            pl.BlockSpec((d, 2 * inner), lambda i: (0, 0)),
            pl.BlockSpec((1, inner), lambda i: (0, 0)),
            pl.BlockSpec((groups, CHUNK, CHUNK), lambda i: (0, 0, 0)),
            pl.BlockSpec((CHUNK, inner), lambda i: (0, 0)),
            pl.BlockSpec((inner, d), lambda i: (0, 0)),
        ],
        out_specs=pl.BlockSpec((TM, d), lambda i: (i, 0)),
        out_shape=jax.ShapeDtypeStruct((t_all, d), F32),
        compiler_params=_params(),
        name="gmlp",
    )(*y, mod, gain.reshape(1, d), w_in.astype(BF16), norm_v.reshape(1, inner),
      w_s.astype(BF16), bexp, w_out.astype(BF16))


def _qkv_kernel(y_ref, mod_ref, gain_ref, w_ref, qn_ref, kn_ref, cos_ref, sin_ref,
                q_ref, k_ref, v_ref):
    y = y_ref[...]
    n_q = q_ref.shape[-1] // HEAD_DIM
    n_kv = k_ref.shape[-1] // HEAD_DIM
    h = _norm_mod(y, gain_ref[...], mod_ref[0, 0:1, :], mod_ref[0, 1:2, :])
    qkv = _dot(h.astype(BF16), w_ref[...])
    cos = cos_ref[...]
    sin = sin_ref[...]
    lane = lax.broadcasted_iota(jnp.int32, cos.shape, 1)
    low_half = (lane % (HEAD_DIM // 2)) < (HEAD_DIM // 4)

    def head(idx, norm):
        x = _rms(qkv[:, idx * HEAD_DIM:(idx + 1) * HEAD_DIM], norm)
        partner = jnp.where(low_half,
                            pltpu.roll(x, HEAD_DIM - HEAD_DIM // 4, 1),
                            pltpu.roll(x, HEAD_DIM // 4, 1))
        return x * cos + partner * sin

    qn = qn_ref[...]
    kn = kn_ref[...]
    q_ref[...] = jnp.concatenate([head(i, qn) for i in range(n_q)], axis=1).astype(q_ref.dtype)
    k_ref[...] = jnp.concatenate([head(n_q + i, kn) for i in range(n_kv)], axis=1)
    v_ref[...] = qkv[:, (n_q + n_kv) * HEAD_DIM:]


def _qkv(y, mod, gain, w_qkv, q_norm, k_norm, cos_t, sin_t, mod_row, rope_row):
    t_all, d = y.shape
    n_kv_cols = N_KV_HEADS * HEAD_DIM
    return pl.pallas_call(
        _qkv_kernel,
        grid=(t_all // TM,),
        in_specs=[
            pl.BlockSpec((TM, d), lambda i: (i, 0)),
            pl.BlockSpec((1, 6, d), lambda i: (mod_row(i), 0, 0)),
            pl.BlockSpec((1, d), lambda i: (0, 0)),
            pl.BlockSpec(w_qkv.shape, lambda i: (0, 0)),
            pl.BlockSpec((1, HEAD_DIM), lambda i: (0, 0)),
            pl.BlockSpec((1, HEAD_DIM), lambda i: (0, 0)),
            pl.BlockSpec((TM, HEAD_DIM), lambda i: (rope_row(i), 0)),
            pl.BlockSpec((TM, HEAD_DIM), lambda i: (rope_row(i), 0)),
        ],
        out_specs=[
            pl.BlockSpec((TM, d), lambda i: (i, 0)),
            pl.BlockSpec((TM, n_kv_cols), lambda i: (i, 0)),
            pl.BlockSpec((TM, n_kv_cols), lambda i: (i, 0)),
        ],
        out_shape=[
            jax.ShapeDtypeStruct((t_all, d), BF16),
            jax.ShapeDtypeStruct((t_all, n_kv_cols), F32),
            jax.ShapeDtypeStruct((t_all, n_kv_cols), F32),
        ],
        compiler_params=_params(),
        name="qkv_project",
    )(y, mod, gain.reshape(1, d), w_qkv.astype(BF16), q_norm.reshape(1, HEAD_DIM),
      k_norm.reshape(1, HEAD_DIM), cos_t, sin_t)


def _attend(q, key_sets):
    n_heads = q.shape[1] // HEAD_DIM
    rep = n_heads // N_KV_HEADS
    scale = HEAD_DIM ** -0.5
    outs = []
    for hd in range(n_heads):
        g = hd // rep
        qh = q[:, hd * HEAD_DIM:(hd + 1) * HEAD_DIM]
        cols = slice(g * HEAD_DIM, (g + 1) * HEAD_DIM)
        scores = [_dot_nt(qh, k[:, cols]) * scale for k, _ in key_sets]
        m = functools.reduce(jnp.maximum, [jnp.max(s, axis=-1, keepdims=True) for s in scores])
        probs = [jnp.exp(s - m) for s in scores]
        denom = functools.reduce(jnp.add, [jnp.sum(p, axis=-1, keepdims=True) for p in probs])
        o = functools.reduce(jnp.add, [_dot(p.astype(BF16), v[:, cols])
                                       for p, (_, v) in zip(probs, key_sets)])
        outs.append(o / denom)
    return jnp.concatenate(outs, axis=1)


def _attn_ctx_kernel(y_ref, mod_ref, q_ref, k_ref, v_ref, wo_ref, o_ref):
    att = _attend(q_ref[...], [(k_ref[...].astype(BF16), v_ref[...].astype(BF16))])
    o_ref[...] = y_ref[...] + mod_ref[0, 2:3, :] * _dot(att.astype(BF16), wo_ref[...])


def _attn_lat_kernel(y_ref, mod_ref, q_ref, k_ref, v_ref, ck_ref, cv_ref, wo_ref, o_ref):
    sets = [(ck_ref[0].astype(BF16), cv_ref[0].astype(BF16)),
            (k_ref[...].astype(BF16), v_ref[...].astype(BF16))]
    att = _attend(q_ref[...], sets)
    o_ref[...] = y_ref[...] + mod_ref[0, 2:3, :] * _dot(att.astype(BF16), wo_ref[...])


def _attention(y, mod, q, k, v, cache_k, cache_v, w_o, n_ctx, ctx_len, n_lat, lat_len):
    t_all, d = y.shape
    kvc = k.shape[1]
    tp = n_ctx * ctx_len
    wo = w_o.astype(BF16)
    y = pl.pallas_call(
        _attn_ctx_kernel,
        grid=(n_ctx,),
        in_specs=[
            pl.BlockSpec((ctx_len, d), lambda b: (b, 0)),
            pl.BlockSpec((1, 6, d), lambda b: (0, 0, 0)),
            pl.BlockSpec((ctx_len, d), lambda b: (b, 0)),
            pl.BlockSpec((ctx_len, kvc), lambda b: (b, 0)),
            pl.BlockSpec((ctx_len, kvc), lambda b: (b, 0)),
            pl.BlockSpec((d, d), lambda b: (0, 0)),
        ],
        out_specs=pl.BlockSpec((ctx_len, d), lambda b: (b, 0)),
        out_shape=jax.ShapeDtypeStruct((t_all, d), F32),
        input_output_aliases={0: 0},
        compiler_params=_params(),
        name="attn_context",
    )(y, mod, q, k, v, wo)
    tq = TM
    qb = lat_len // tq
    past = cache_k.shape[1]
    y = pl.pallas_call(
        _attn_lat_kernel,
        grid=(n_lat, qb),
        in_specs=[
            pl.BlockSpec((tq, d), lambda b, i: (tp // tq + b * qb + i, 0)),
            pl.BlockSpec((1, 6, d), lambda b, i: (1 + b, 0, 0)),
            pl.BlockSpec((tq, d), lambda b, i: (tp // tq + b * qb + i, 0)),
            pl.BlockSpec((lat_len, kvc), lambda b, i: (tp // lat_len + b, 0)),
            pl.BlockSpec((lat_len, kvc), lambda b, i: (tp // lat_len + b, 0)),
            pl.BlockSpec((1, past, kvc), lambda b, i: (b, 0, 0)),
            pl.BlockSpec((1, past, kvc), lambda b, i: (b, 0, 0)),
            pl.BlockSpec((d, d), lambda b, i: (0, 0)),
        ],
        out_specs=pl.BlockSpec((tq, d), lambda b, i: (tp // tq + b * qb + i, 0)),
        out_shape=jax.ShapeDtypeStruct((t_all, d), F32),
        input_output_aliases={0: 0},
        compiler_params=_params(2),
        name="attn_latent",
    )(y, mod, q, k, v, cache_k, cache_v, wo)
    return y


def _pool_kernel(y_ref, mod_ref, gain_ref, wg_ref, scale_ref, o_ref):
    y = y_ref[...]
    s_len, d = y.shape
    gd = d // len(POOL_WINDOWS)
    h = _norm_mod(y, gain_ref[...], mod_ref[0, 0:1, :], mod_ref[0, 1:2, :])
    t_idx = lax.broadcasted_iota(jnp.int32, (s_len, s_len), 0)
    s_idx = lax.broadcasted_iota(jnp.int32, (s_len, s_len), 1)
    t_col = lax.broadcasted_iota(jnp.int32, (s_len, 1), 0)
    parts = []
    for g, w in enumerate(POOL_WINDOWS):
        lo = t_idx - w // 2
        window = ((s_idx >= lo) & (s_idx < lo + w)).astype(BF16)
        count = (jnp.minimum(t_col - w // 2 + w, s_len) - jnp.maximum(t_col - w // 2, 0)).astype(F32)
        hg = h[:, g * gd:(g + 1) * gd]
        h_hi = hg.astype(BF16)
        h_lo = (hg - h_hi.astype(F32)).astype(BF16)
        mean = (_dot(window, h_hi) + _dot(window, h_lo)) / count
        parts.append(_dot((mean - hg).astype(BF16), wg_ref[g]))
    mixed = jnp.concatenate(parts, axis=1) * scale_ref[...]
    o_ref[...] = y + mod_ref[0, 2:3, :] * mixed


def _pool(y, mod, gain, w_grp, scale, n_seq, s_len, row_off, mod_off):
    t_all, d = y.shape
    blk_off = row_off // s_len
    return pl.pallas_call(
        _pool_kernel,
        grid=(n_seq,),
        in_specs=[
            pl.BlockSpec((s_len, d), lambda b: (blk_off + b, 0)),
            pl.BlockSpec((1, 6, d), lambda b: (mod_off(b), 0, 0)),
            pl.BlockSpec((1, d), lambda b: (0, 0)),
            pl.BlockSpec(w_grp.shape, lambda b: (0, 0, 0)),
            pl.BlockSpec((1, d), lambda b: (0, 0)),
        ],
        out_specs=pl.BlockSpec((s_len, d), lambda b: (blk_off + b, 0)),
        out_shape=jax.ShapeDtypeStruct((t_all, d), F32),
        input_output_aliases={0: 0},
        compiler_params=_params(),
        name="pool_mixer",
    )(y, mod, gain.reshape(1, d), w_grp.astype(BF16), scale.reshape(1, d))


def _sorted_rows(n_exp):
    worst = TM * TOP_K + n_exp * (UNIT - 1) + UNIT
    return -(-worst // LANES) * LANES


def _route_kernel(y_ref, mod_ref, gain_ref, rw_ref, rb_ref, xs_ref, route_ref, cnt_ref):
    y = y_ref[...]
    tm = y.shape[0]
    n_rows = xs_ref.shape[0]
    h = _norm_mod(y, gain_ref[...], mod_ref[0, 3:4, :], mod_ref[0, 4:5, :])
    h_hi = h.astype(BF16)
    h_lo = (h - h_hi.astype(F32)).astype(BF16)
    hi_terms = _dot(h_hi, rw_ref[...])
    logits = hi_terms[:, :LANES] + hi_terms[:, LANES:] + _dot(h_lo, rw_ref[:, :LANES]) + rb_ref[...]
    lane = lax.broadcasted_iota(jnp.int32, logits.shape, 1).astype(F32)
    work = logits
    vals, ids = [], []
    for _ in range(TOP_K):
        m = jnp.max(work, axis=-1, keepdims=True)
        idx = jnp.min(jnp.where(work == m, lane, float(LANES)), axis=-1, keepdims=True)
        vals.append(m)
        ids.append(idx)
        work = jnp.where(lane == idx, -jnp.inf, work)
    exps = [jnp.exp(v - vals[0]) for v in vals]
    denom = functools.reduce(jnp.add, exps)
    onehot = functools.reduce(jnp.add, [(lane == idx).astype(F32) for idx in ids])

    counts = jnp.sum(onehot, axis=0, keepdims=True)
    seg = jnp.ceil(counts * (1.0 / UNIT)) * UNIT
    e_row = lax.broadcasted_iota(jnp.int32, (LANES, LANES), 0)
    e_col = lax.broadcasted_iota(jnp.int32, (LANES, LANES), 1)
    earlier = (e_row < e_col).astype(BF16)
    seg_off = _dot(jnp.broadcast_to(seg, (SUBLANES, LANES)).astype(BF16), earlier)[0:1, :]
    row = lax.broadcasted_iota(jnp.int32, (tm, tm), 0)
    col = lax.broadcasted_iota(jnp.int32, (tm, tm), 1)
    before = (col < row).astype(BF16)
    pos_mat = _dot(before, onehot.astype(BF16)) + seg_off

    route = jnp.zeros(logits.shape, F32)
    for k in range(TOP_K):
        pos_k = jnp.sum(jnp.where(lane == ids[k], pos_mat, 0.0), axis=-1, keepdims=True)
        route = jnp.where(lane == k, exps[k] / denom, route)
        route = jnp.where(lane == TOP_K + k, ids[k], route)
        route = jnp.where(lane == 2 * TOP_K + k, pos_k, route)
    route_ref[...] = route
    cnt_ref[...] = jnp.broadcast_to(counts, cnt_ref.shape)

    route_t = jnp.transpose(route)
    out_row = lax.broadcasted_iota(jnp.int32, (n_rows, tm), 0).astype(F32)
    perm = functools.reduce(jnp.add, [(out_row == route_t[2 * TOP_K + k:2 * TOP_K + k + 1, :]).astype(F32)
                                      for k in range(TOP_K)])
    xs_ref[...] = _dot(perm.astype(BF16), h_hi)


def _route(y, mod, gain, router_w, router_b, mod_row):
    t_all, d = y.shape
    n_exp = router_w.shape[1]
    n_tiles = t_all // TM
    n_rows = _sorted_rows(n_exp)
    rw = jnp.pad(router_w, ((0, 0), (0, LANES - n_exp)))
    rw_hi = rw.astype(BF16)
    rw = jnp.concatenate([rw_hi, (rw - rw_hi.astype(F32)).astype(BF16)], axis=1)
    rb = jnp.pad(router_b, (0, LANES - n_exp), constant_values=-1e30).reshape(1, LANES)
    return pl.pallas_call(
        _route_kernel,
        grid=(n_tiles,),
        in_specs=[
            pl.BlockSpec((TM, d), lambda i: (i, 0)),
            pl.BlockSpec((1, 6, d), lambda i: (mod_row(i), 0, 0)),
            pl.BlockSpec((1, d), lambda i: (0, 0)),
            pl.BlockSpec((d, 2 * LANES), lambda i: (0, 0)),
            pl.BlockSpec((1, LANES), lambda i: (0, 0)),
        ],
        out_specs=[
            pl.BlockSpec((n_rows, d), lambda i: (i, 0)),
            pl.BlockSpec((TM, LANES), lambda i: (i, 0)),
            pl.BlockSpec((SUBLANES, LANES), lambda i: (i, 0)),
        ],
        out_shape=[
            jax.ShapeDtypeStruct((n_tiles * n_rows + 2 * MOE_ROWS, d), F32),
            jax.ShapeDtypeStruct((t_all, LANES), F32),
            jax.ShapeDtypeStruct((n_tiles * SUBLANES, LANES), F32),
        ],
        compiler_params=_params(),
        name="moe_route",
    )(y, mod, gain.reshape(1, d), rw, rb)


def _plan_kernel(n_tiles, n_exp, n_rows, c_ref, srcx_ref, dstx_ref, be_ref, nxt_ref, nu_ref):
    nb = srcx_ref.shape[0]
    seg = jnp.ceil(c_ref[...] * (1.0 / UNIT)) * UNIT
    seg_b = seg.astype(BF16)
    sq_r = lax.broadcasted_iota(jnp.int32, (LANES, LANES), 0)
    sq_c = lax.broadcasted_iota(jnp.int32, (LANES, LANES), 1)
    upto = (sq_r <= sq_c).astype(BF16)
    seg_end_t = _dot(seg_b, upto)
    seg_off_t = seg_end_t - seg
    seg_end_e = _dot((sq_c <= sq_r).astype(BF16), seg_b)
    seg_off_e = seg_end_e - seg
    rows_e = seg_end_e[n_tiles - 1:n_tiles, :]
    nblk = jnp.ceil(rows_e * (1.0 / MOE_ROWS))
    end_blk = _dot(jnp.broadcast_to(nblk, (SUBLANES, LANES)).astype(BF16), upto)[0:1, :]
    start_blk = end_blk - nblk
    n_used = jnp.max(end_blk, axis=-1, keepdims=True)

    lane1 = lax.broadcasted_iota(jnp.int32, (1, LANES), 1).astype(F32)
    lane = lax.broadcasted_iota(jnp.int32, (nb, LANES), 1).astype(F32)
    blk = lax.broadcasted_iota(jnp.int32, (nb, LANES), 0).astype(F32)
    is_exp = lane1 < n_exp
    has_rows = (nblk > 0) & is_exp
    be = jnp.minimum(jnp.sum(((end_blk <= blk) & is_exp).astype(F32), axis=-1, keepdims=True), n_exp - 1.0)
    last_e = jnp.max(jnp.where(has_rows, lane1, 0.0), axis=-1, keepdims=True)
    used = blk[:, 0:1] < n_used
    be = jnp.where(used, be, last_e)
    nxt = jnp.min(jnp.where((lane > be) & has_rows, lane, float(LANES)), axis=-1, keepdims=True)
    nxt = jnp.where(nxt == LANES, be, nxt)
    onehot = (lane == be).astype(F32)

    def column_of_block(x):
        hi = jnp.floor(x * (1.0 / 256.0))
        lo = x - hi * 256.0
        oh = onehot.astype(BF16)
        return _dot_nt(oh, hi.astype(BF16)) * 256.0 + _dot_nt(oh, lo.astype(BF16))

    ends_b = column_of_block(seg_end_e)
    shift_b = column_of_block(seg_off_t) - column_of_block(seg_off_e)
    start_b = jnp.sum(onehot * start_blk, axis=-1, keepdims=True)
    rows_b = jnp.sum(onehot * rows_e, axis=-1, keepdims=True)
    r = (blk - start_b) * MOE_ROWS + lane * UNIT
    tile_of = jnp.zeros((nb, LANES), F32)
    for t in range(n_tiles):
        tile_of = tile_of + (ends_b[:, t:t + 1] <= r).astype(F32)
    tile_c = jnp.minimum(tile_of, n_tiles - 1.0)
    src = jnp.zeros((nb, LANES), F32)
    for t in range(n_tiles):
        src = src + jnp.where(tile_c == t, shift_b[:, t:t + 1] + float(t * n_rows), 0.0)
    valid = (r < rows_b) & used & (lane < MOE_ROWS // UNIT)
    srcx_ref[...] = jnp.where(valid, src + r, float(n_rows - UNIT)).astype(jnp.int32)
    spare = float(n_tiles * n_rows) + (blk - 2.0 * jnp.floor(blk * 0.5)) * MOE_ROWS + lane * UNIT
    dstx_ref[...] = jnp.where(valid, src + r, spare).astype(jnp.int32)

    rows_here = jnp.clip(rows_b - (blk[:, 0:1] - start_b) * MOE_ROWS, 0.0, float(MOE_ROWS))
    be_ref[...] = jnp.where(lane == 1, rows_here, be).astype(jnp.int32)
    nxt_ref[...] = jnp.broadcast_to(nxt, nxt_ref.shape).astype(jnp.int32)
    nu_ref[...] = jnp.broadcast_to(n_used, nu_ref.shape).astype(jnp.int32)


def _plan(counts, n_tiles, n_exp, n_rows, n_blocks):
    assert n_tiles <= LANES and n_exp <= LANES and n_tiles % SUBLANES == 0
    blk_units = MOE_ROWS // UNIT
    assert blk_units <= LANES
    nb = -(-n_blocks // SUBLANES) * SUBLANES
    srcx, dstx, be, nxt, nu = pl.pallas_call(
        functools.partial(_plan_kernel, n_tiles, n_exp, n_rows),
        out_shape=[
            jax.ShapeDtypeStruct((nb, LANES), jnp.int32),
            jax.ShapeDtypeStruct((nb, LANES), jnp.int32),
            jax.ShapeDtypeStruct((nb, LANES), jnp.int32),
            jax.ShapeDtypeStruct((nb, LANES), jnp.int32),
            jax.ShapeDtypeStruct((SUBLANES, LANES), jnp.int32),
        ],
        compiler_params=pltpu.CompilerParams(vmem_limit_bytes=VMEM_LIMIT_BYTES),
        name="moe_plan",
    )(counts)
    before = n_tiles * n_rows + MOE_ROWS + UNIT * jnp.arange(blk_units, dtype=jnp.int32)
    dst = jnp.concatenate([before, dstx[:n_blocks, :blk_units].reshape(-1)])
    return (srcx[:n_blocks, :blk_units].reshape(-1), dst, be[:n_blocks, 0], be[:n_blocks, 1],
            nxt[:n_blocks, 0], nu[0, :1])


def _gather_units(src_ref, base, table_ref, dst, sem):
    for u in range(dst.shape[0] // UNIT):
        start = pl.multiple_of(table_ref[base + u], UNIT)
        pltpu.make_async_copy(src_ref.at[pl.ds(start, UNIT)], dst.at[pl.ds(u * UNIT, UNIT)], sem).start()


def _scatter_units(src, base, table_ref, dst_ref, sem):
    for u in range(src.shape[0] // UNIT):
        start = pl.multiple_of(table_ref[base + u], UNIT)
        pltpu.make_async_copy(src.at[pl.ds(u * UNIT, UNIT)], dst_ref.at[pl.ds(start, UNIT)], sem).start()


def _wait_units(buf, sem):
    pltpu.make_async_copy(buf, buf, sem).wait()


def _expert_kernel(layer, be_ref, rows_ref, nu_ref, src_ref, dst_ref, nxt_ref, xs_ref, wgu_hbm, bgu_ref, wdn_hbm,
                   bdn_ref, ys_ref, wgu_st, wdn_st, wgu_bf, wdn_bf, x0, x1, y0, y1, gsem, ssem, wsem):
    i = pl.program_id(0)
    n_used = nu_ref[0]
    n_blocks = pl.num_programs(0)
    units = x0.shape[0] // UNIT
    d_exp = wdn_bf.shape[0]

    def weight_copies(e):
        return (pltpu.make_async_copy(wgu_hbm.at[layer, e], wgu_st, wsem.at[0]),
                pltpu.make_async_copy(wdn_hbm.at[layer, e], wdn_st, wsem.at[1]))

    def run_block(xcur, xoth, ycur, yoth, p, m):
        _wait_units(xcur, gsem.at[p])

        @pl.when(i >= 1)
        def _():
            _wait_units(ycur, ssem.at[p])

        x = xcur[0:m, :].astype(BF16)
        gu = _dot(x, wgu_bf[...]) + bgu_ref[0, 0]
        _gather_units(xs_ref, jnp.minimum(i + 1, n_blocks - 1) * units, src_ref, xoth, gsem.at[1 - p])
        _scatter_units(yoth, i * units, dst_ref, ys_ref, ssem.at[1 - p])
        gate = jnp.minimum(gu[:, :d_exp], SWIGLU_LIMIT)
        up = jnp.clip(gu[:, d_exp:], -SWIGLU_LIMIT, SWIGLU_LIMIT)
        hid = (up + 1.0) * gate * jax.nn.sigmoid(SWIGLU_ALPHA * gate)
        ycur[0:m, :] = _dot(hid.astype(BF16), wdn_bf[...]) + bdn_ref[0, 0]

        @pl.when(i == n_used - 1)
        def _():
            _wait_units(xoth, gsem.at[1 - p])
            _scatter_units(ycur, (i + 1) * units, dst_ref, ys_ref, ssem.at[p])
            _wait_units(yoth, ssem.at[1 - p])
            _wait_units(ycur, ssem.at[p])

    @pl.when(i < n_used)
    def _():
        @pl.when(i == 0)
        def _():
            for cp in weight_copies(be_ref[0]):
                cp.start(priority=1)
            _gather_units(xs_ref, 0, src_ref, x0, gsem.at[0])
            y0[...] = jnp.zeros_like(y0)
            y1[...] = jnp.zeros_like(y1)

        new_expert = jnp.logical_or(i == 0, be_ref[i] != be_ref[jnp.maximum(i - 1, 0)])

        @pl.when(new_expert)
        def _():
            for cp in weight_copies(be_ref[i]):
                cp.wait()
            wgu_bf[...] = wgu_st[...].astype(BF16)
            wdn_bf[...] = wdn_st[...].astype(BF16)

            @pl.when(nxt_ref[i] != be_ref[i])
            def _():
                for cp in weight_copies(nxt_ref[i]):
                    cp.start(priority=1)

        n_gran = x0.shape[0] // MOE_GRANULE
        granules = jnp.clip((rows_ref[i] + MOE_GRANULE - 1) // MOE_GRANULE, 1, n_gran)
        for g in range(1, n_gran + 1):
            @pl.when(jnp.logical_and(i % 2 == 0, granules == g))
            def _():
                run_block(x0, x1, y0, y1, 0, g * MOE_GRANULE)

            @pl.when(jnp.logical_and(i % 2 == 1, granules == g))
            def _():
                run_block(x1, x0, y1, y0, 1, g * MOE_GRANULE)


def _experts(layer, xs, src_units, dst_units, block_e, block_rows, next_e, n_used, w_gu, b_gu, w_dn, b_dn,
             n_blocks):
    d = w_dn.shape[-1]
    depth, n_exp, _, two_f = w_gu.shape
    d_exp = two_f // 2
    n_prefetch = 6

    def bias_block(i, be, rows, nu, src, dst, nxt):
        return (layer, be[i], 0, 0)

    grid_spec = pltpu.PrefetchScalarGridSpec(
        num_scalar_prefetch=n_prefetch,
        grid=(n_blocks,),
        in_specs=[
            pl.BlockSpec(memory_space=pl.ANY),
            pl.BlockSpec(memory_space=pl.ANY),
            pl.BlockSpec((1, 1, 1, two_f), bias_block),
            pl.BlockSpec(memory_space=pl.ANY),
            pl.BlockSpec((1, 1, 1, d), bias_block),
        ],
        out_specs=pl.BlockSpec(memory_space=pl.ANY),
        scratch_shapes=[
            pltpu.VMEM((d, two_f), F32),
            pltpu.VMEM((d_exp, d), F32),
            pltpu.VMEM((d, two_f), BF16),
            pltpu.VMEM((d_exp, d), BF16),
            pltpu.VMEM((MOE_ROWS, d), F32),
            pltpu.VMEM((MOE_ROWS, d), F32),
            pltpu.VMEM((MOE_ROWS, d), F32),
            pltpu.VMEM((MOE_ROWS, d), F32),
            pltpu.SemaphoreType.DMA((2,)),
            pltpu.SemaphoreType.DMA((2,)),
            pltpu.SemaphoreType.DMA((2,)),
        ],
    )
    return pl.pallas_call(
        functools.partial(_expert_kernel, layer),
        grid_spec=grid_spec,
        out_shape=jax.ShapeDtypeStruct(xs.shape, xs.dtype),
        input_output_aliases={n_prefetch: 0},
        compiler_params=_params(),
        name="moe_experts",
    )(block_e, block_rows, n_used, src_units, dst_units, next_e, xs, w_gu, b_gu.reshape(depth, n_exp, 1, two_f),
      w_dn, b_dn.reshape(depth, n_exp, 1, d))


def _combine_kernel(n_first, y_ref, mod_ref, route_ref, ys_ref, *o_refs):
    route = route_ref[...]
    tm = route.shape[0]
    n_rows = ys_ref.shape[0]
    col = lax.broadcasted_iota(jnp.int32, (tm, n_rows), 1).astype(F32)
    weights = functools.reduce(jnp.add, [
        jnp.where(col == route[:, 2 * TOP_K + k:2 * TOP_K + k + 1], route[:, k:k + 1], 0.0)
        for k in range(TOP_K)])
    acc = _dot(weights.astype(BF16), ys_ref[...].astype(BF16))
    out = y_ref[...] + mod_ref[0, 5:6, :] * acc
    if n_first is None:
        o_refs[0][...] = out
    else:
        first = pl.program_id(0) < n_first

        @pl.when(first)
        def _():
            o_refs[0][...] = out

        @pl.when(jnp.logical_not(first))
        def _():
            o_refs[1][...] = out


def _combine(y, mod, route, ys, n_rows, mod_row, split_rows=None):
    t_all, d = y.shape
    if split_rows is None:
        n_first = None
        out_specs = pl.BlockSpec((TM, d), lambda i: (i, 0))
        out_shape = jax.ShapeDtypeStruct((t_all, d), F32)
    else:
        n_first = split_rows // TM
        out_specs = [pl.BlockSpec((TM, d), lambda i: (jnp.minimum(i, n_first - 1), 0)),
                     pl.BlockSpec((TM, d), lambda i: (jnp.maximum(i - n_first, 0), 0))]
        out_shape = [jax.ShapeDtypeStruct((split_rows, d), F32),
                     jax.ShapeDtypeStruct((t_all - split_rows, d), F32)]
    return pl.pallas_call(
        functools.partial(_combine_kernel, n_first),
        grid=(t_all // TM,),
        in_specs=[
            pl.BlockSpec((TM, d), lambda i: (i, 0)),
            pl.BlockSpec((1, 6, d), lambda i: (mod_row(i), 0, 0)),
            pl.BlockSpec((TM, LANES), lambda i: (i, 0)),
            pl.BlockSpec((n_rows, d), lambda i: (i, 0)),
        ],
        out_specs=out_specs,
        out_shape=out_shape,
        compiler_params=_params(),
        name="moe_combine",
    )(y, mod, route, ys)


def _moe(layer, y, mod, gain, router_w, router_b, w_gu, b_gu, w_dn, b_dn, mod_row, split_rows=None):
    t_all, d = y.shape
    n_exp = router_w.shape[1]
    n_tiles = t_all // TM
    n_rows = _sorted_rows(n_exp)
    n_blocks = -(-(t_all * TOP_K + n_tiles * n_exp * (UNIT - 1)) // MOE_ROWS) + n_exp

    xs, route, cnt = _route(y, mod, gain, router_w, router_b, mod_row)
    counts = jnp.pad(cnt.reshape(n_tiles, SUBLANES, LANES)[:, 0, :], ((0, LANES - n_tiles), (0, 0)))
    src_x, dst_x, block_e, block_rows, next_e, n_used = _plan(counts, n_tiles, n_exp, n_rows, n_blocks)
    ys = _experts(layer, xs, src_x, dst_x, block_e, block_rows, next_e, n_used, w_gu, b_gu, w_dn, b_dn, n_blocks)
    return _combine(y, mod, route, ys, n_rows, mod_row, split_rows)


def kernel(x_prompt, x_sample, cache_k, cache_v, c, c_ctx, w_mod, b_mod, norm_mix, norm_ffn, gm_w_in, gm_norm_v, gm_w_s, gm_b_s, gm_w_out, at_w_qkv, at_q_norm, at_k_norm, at_w_o, pool_w_grp, pool_scale, router_w, router_b, w_gate_up, b_gate_up, w_down, b_down):
    n_ctx, ctx_len, d = x_prompt.shape
    n_lat, lat_len, _ = x_sample.shape
    depth = w_mod.shape[0]
    tp = n_ctx * ctx_len
    assert tp % lat_len == 0 and ctx_len % TM == 0 and lat_len % TM == 0 and ctx_len == TM
    assert 1 + n_lat <= SUBLANES

    assert depth >= 1
    y = (x_prompt.reshape(tp, d), x_sample.reshape(n_lat * lat_len, d))

    cvecs = jnp.zeros((SUBLANES, d), F32).at[0].set(c_ctx).at[1:1 + n_lat].set(c)
    mod_all = jnp.transpose(_ada_params(cvecs, w_mod, b_mod), (0, 2, 1, 3))

    def mod_row(i):
        return jnp.where(i * TM < tp, 0, 1 + (i * TM - tp) // lat_len)

    half = HEAD_DIM // 4
    inv = ROPE_THETA ** (-jnp.arange(half, dtype=F32) / half)
    pos = jnp.arange(lat_len)
    ang_r = (pos // GRID_W).astype(F32)[:, None] * inv[None, :]
    ang_c = (pos % GRID_W).astype(F32)[:, None] * inv[None, :]
    cos_t = jnp.concatenate([jnp.ones((TM, HEAD_DIM), F32),
                             jnp.concatenate([jnp.cos(ang_r)] * 2 + [jnp.cos(ang_c)] * 2, axis=1)], axis=0)
    sin_t = jnp.concatenate([jnp.zeros((TM, HEAD_DIM), F32),
                             jnp.concatenate([-jnp.sin(ang_r), jnp.sin(ang_r),
                                              -jnp.sin(ang_c), jnp.sin(ang_c)], axis=1)], axis=0)

    def rope_row(i):
        return jnp.where(i * TM < tp, 0, 1 + ((i * TM - tp) % lat_len) // TM)

    new_k, new_v = [], []
    for l in range(depth):
        kind, j = l % 3, l // 3
        mod = mod_all[l]
        if kind == 0:
            y = _gmlp(y, mod, norm_mix[l], gm_w_in[j], gm_norm_v[j], gm_w_s[j], gm_b_s[j], gm_w_out[j], mod_row)
        elif kind == 1:
            q, k, v = _qkv(y, mod, norm_mix[l], at_w_qkv[j], at_q_norm[j], at_k_norm[j],
                           cos_t, sin_t, mod_row, rope_row)
            new_k.append(k[:tp].reshape(n_ctx, ctx_len, N_KV_HEADS, HEAD_DIM))
            new_v.append(v[:tp].reshape(n_ctx, ctx_len, N_KV_HEADS, HEAD_DIM))
            past = cache_k.shape[2]
            ck = cache_k[:, j].reshape(n_lat, past, N_KV_HEADS * HEAD_DIM)
            cv = cache_v[:, j].reshape(n_lat, past, N_KV_HEADS * HEAD_DIM)
            y = _attention(y, mod, q, k, v, ck, cv, at_w_o[j], n_ctx, ctx_len, n_lat, lat_len)
        else:
            y = _pool(y, mod, norm_mix[l], pool_w_grp[j], pool_scale[j], n_ctx, ctx_len, 0, lambda b: 0)
            y = _pool(y, mod, norm_mix[l], pool_w_grp[j], pool_scale[j], n_lat, lat_len, tp, lambda b: 1 + b)
        y = _moe(l, y, mod, norm_ffn[l], router_w[l], router_b[l], w_gate_up, b_gate_up,
                 w_down, b_down, mod_row, split_rows=tp if l == depth - 1 else None)

    y_ctx, y_lat = y
    return (y_ctx.reshape(n_ctx, ctx_len, d), y_lat.reshape(n_lat, lat_len, d),
            jnp.stack(new_k, axis=1), jnp.stack(new_v, axis=1))
```
